```python
import jax, jax.numpy as jnp
from jax import lax
import numpy as np

D_MODEL = 1024
BATCH = 8
SEQ = 2048
DEPTH = 2
DEC_BATCH = 128
DEC_SEQ = 1
PAST_LEN = 16384
PAGE_SIZE = 128

HG_WIDTH = D_MODEL
HG_DK = 128
HG_HEADS = HG_WIDTH // HG_DK
HG_DV = HG_WIDTH // HG_HEADS
HG_QF = HG_HEADS * HG_DK
HG_CHUNK = 64
HG_F_MIN = 1e-20
RW_WIDTH = D_MODEL
RW_N = 64
RW_HEADS = RW_WIDTH // RW_N
RW_DECAY_LORA = 64
RW_AAA_LORA = 64
RW_SHIFT = 3 * RW_WIDTH + RW_DECAY_LORA + RW_AAA_LORA
RW_LN_EPS = 64e-5
MB_WIDTH = D_MODEL
MB_P = 64
MB_HEADS = MB_WIDTH // MB_P
MB_GROUPS = 4
MB_N = 128
MB_CONV = 4
MB_XBC = MB_WIDTH + 2 * MB_GROUPS * MB_N
MB_CHUNK = 64
N_BRANCH = 3
RMS_EPS = 1e-6
IN_SPLITS = (HG_QF, HG_QF, HG_WIDTH, HG_WIDTH, RW_SHIFT, RW_WIDTH, MB_WIDTH, MB_XBC, MB_HEADS, N_BRANCH * D_MODEL)
IN_COLS = 2 * HG_QF + 2 * HG_WIDTH + RW_SHIFT + RW_WIDTH + MB_WIDTH + MB_XBC + MB_HEADS + N_BRANCH * D_MODEL

kernel_name = "hgrn2_rwkv7_mamba2_gated_parallel_step"

F32 = jnp.float32


def rms_norm(x, w, eps=RMS_EPS):
    xf = x.astype(F32)
    y = xf * lax.rsqrt(jnp.mean(xf * xf, axis=-1, keepdims=True) + eps)
    return y * w.astype(F32)


def shift_lerp(z, prev, mu):
    z_prev = jnp.concatenate([prev[:, None, :].astype(z.dtype), z[:, :-1]], axis=1)
    return z + (z_prev - z) * mu, z[:, -1]


def causal_dwconv(u, buf, w, b):
    full = jnp.concatenate([buf.astype(u.dtype), u], axis=1)
    out = lax.conv_general_dilated(full, w[:, None, :].astype(u.dtype), (1,), 'VALID',
                                   dimension_numbers=('NWC', 'WIO', 'NWC'),
                                   feature_group_count=u.shape[-1])
    return out + b, full[:, full.shape[1] - (MB_CONV - 1):]


def gla_chunked(q, k, v, log_f, S0):
    Bt, T, H, K = q.shape
    C = HG_CHUNK if T % HG_CHUNK == 0 else T
    n = T // C
    to_chunks = lambda a: a.astype(F32).reshape(Bt, n, C, H, a.shape[-1]).transpose(1, 0, 3, 2, 4)
    mask = jnp.tril(jnp.ones((C, C), bool))[:, :, None]

    def step(S, inp):
        qb, kb, vb, gb = inp
        b = jnp.cumsum(gb, axis=2)
        diff = b[:, :, :, None, :] - b[:, :, None, :, :]
        decay = jnp.where(mask, jnp.exp(jnp.where(mask, diff, 0.0)), 0.0)
        A = jnp.einsum('bhik,bhijk,bhjk->bhij', qb, decay, kb)
        o = jnp.einsum('bhij,bhjv->bhiv', A, vb) + jnp.einsum('bhik,bhkv->bhiv', qb * jnp.exp(b), S)
        bC = b[:, :, -1:]
        S_new = jnp.exp(bC[:, :, 0])[..., None] * S + jnp.einsum('bhjk,bhjv->bhkv', kb * jnp.exp(bC - b), vb)
        return S_new, o

    S, o = lax.scan(step, S0.astype(F32), (to_chunks(q), to_chunks(k), to_chunks(v), to_chunks(log_f)))
    return o.transpose(1, 0, 3, 2, 4).reshape(Bt, T, H, -1), S


def rwkv7_recurrence(r, decay, k, v, a, b, S0):
    to_t = lambda t: jnp.moveaxis(t.astype(F32), 1, 0)

    def step(S, inp):
        rt, wt, kt, vt, at, bt = inp
        Sa = jnp.einsum('bhvk,bhk->bhv', S, at)
        S = S * wt[:, :, None, :] + Sa[..., :, None] * bt[..., None, :] + vt[..., :, None] * kt[..., None, :]
        return S, jnp.einsum('bhvk,bhk->bhv', S, rt)

    S, o = lax.scan(step, S0.astype(F32), (to_t(r), to_t(decay), to_t(k), to_t(v), to_t(a), to_t(b)))
    return jnp.moveaxis(o, 0, 1), S


def seg_decay(a):
    c = jnp.cumsum(a, axis=-1)
    n = a.shape[-1]
    mask = jnp.tril(jnp.ones((n, n), bool))
    return jnp.where(mask, jnp.exp(jnp.where(mask, c[..., :, None] - c[..., None, :], 0.0)), 0.0)


def ssd_chunked(xdt, a, Bg, Cg, h0):
    Bt, T, H, P = xdt.shape
    G, N = Bg.shape[2], Bg.shape[3]
    E = H // G
    L = MB_CHUNK if T % MB_CHUNK == 0 else T
    c = T // L
    x = xdt.reshape(Bt, c, L, G, E, P)
    Bc = Bg.reshape(Bt, c, L, G, N)
    Cc = Cg.reshape(Bt, c, L, G, N)
    A = a.reshape(Bt, c, L, G, E).transpose(0, 3, 4, 1, 2)
    A_cum = jnp.cumsum(A, axis=-1)
    Lmat = seg_decay(A)
    y_diag = jnp.einsum('bclgn,bcsgn,bgecls,bcsgep->bclgep', Cc, Bc, Lmat, x)
    decay_in = jnp.exp(A_cum[..., -1:] - A_cum)
    states = jnp.einsum('bclgn,bgecl,bclgep->bcgepn', Bc, decay_in, x)
    states = jnp.concatenate([h0.astype(F32).reshape(Bt, 1, G, E, P, N), states], axis=1)
    chunk_tot = jnp.pad(A_cum[..., -1], ((0, 0), (0, 0), (0, 0), (1, 0)))
    chunk_decay = seg_decay(chunk_tot)
    states = jnp.einsum('bgezc,bcgepn->bzgepn', chunk_decay, states)
    final = states[:, -1].reshape(Bt, H, P, N)
    y_off = jnp.einsum('bclgn,bcgepn,bgecl->bclgep', Cc, states[:, :-1], jnp.exp(A_cum))
    return (y_diag + y_off).reshape(Bt, T, H, P), final


def mixer_layer(h, s_hg, s_rw, s_sh, s_ssm, s_conv, lb,
                norm_w, w_in, hg_norm_w, rw_mu, rw_w0, rw_w2, rw_a0, rw_a2, rw_k_k, rw_k_a, rw_r_k,
                rw_ln_w, rw_ln_b, mb_conv_w, mb_conv_b, mb_dt_bias, mb_A_log, mb_D, mb_norm_w,
                w_o_hg, w_o_rw, w_o_mb, w_out):
    Bt, T, _ = h.shape
    dt_ = h.dtype
    xn = rms_norm(h, norm_w).astype(dt_)
    proj = xn @ w_in
    (q_hg, f_hg, i_hg, g_hg, z_rw, g_rw, z_mb, xbc_mb, dt_mb, gate_raw) = jnp.split(
        proj, np.cumsum(IN_SPLITS)[:-1].tolist(), axis=-1)

    f_raw = f_hg.astype(F32).reshape(Bt, T, HG_HEADS, HG_DK)
    lb = lb.reshape(HG_HEADS, HG_DK)
    f_gate = lb + (1.0 - lb) * jax.nn.sigmoid(f_raw)
    log_f = jnp.log(jnp.maximum(f_gate, HG_F_MIN))
    k_hg = (1.0 - lb) * jax.nn.sigmoid(-f_raw)
    q = jax.nn.silu(q_hg.astype(F32)).reshape(Bt, T, HG_HEADS, HG_DK) * (HG_DK ** -0.5)
    v_hg = i_hg.reshape(Bt, T, HG_HEADS, HG_DV)
    o_hg, s_hg_new = gla_chunked(q, k_hg, v_hg, log_f, s_hg)
    o_hg = rms_norm(o_hg, hg_norm_w).reshape(Bt, T, HG_WIDTH) * jax.nn.silu(g_hg.astype(F32))

    zs, sh_new = shift_lerp(z_rw, s_sh, rw_mu)
    r, k, v, wd, ad = jnp.split(zs.astype(F32), [RW_WIDTH, 2 * RW_WIDTH, 3 * RW_WIDTH,
                                                 3 * RW_WIDTH + RW_DECAY_LORA], axis=-1)
    w_log = -jax.nn.softplus(-(rw_w0 + jnp.tanh(wd) @ rw_w2)) - 0.5
    decay = jnp.exp(-jnp.exp(w_log))
    a = jax.nn.sigmoid(rw_a0 + ad @ rw_a2)
    heads = lambda t: t.reshape(Bt, T, RW_HEADS, RW_N)
    kk = heads(k * rw_k_k)
    kk = kk / jnp.maximum(jnp.linalg.norm(kk, axis=-1, keepdims=True), 1e-12)
    k = k * (1.0 + (a - 1.0) * rw_k_a)
    rh, kh, vh, ah = heads(r), heads(k), heads(v), heads(a)
    o, s_rw_new = rwkv7_recurrence(rh, heads(decay), kh, vh, -kk, kk * ah, s_rw)
    mu = jnp.mean(o, axis=-1, keepdims=True)
    var = jnp.mean(jnp.square(o - mu), axis=-1, keepdims=True)
    o = ((o - mu) * lax.rsqrt(var + RW_LN_EPS)).reshape(Bt, T, RW_WIDTH) * rw_ln_w + rw_ln_b
    o = o + (jnp.sum(rh * kh * rw_r_k, axis=-1, keepdims=True) * vh).reshape(Bt, T, RW_WIDTH)
    o_rw = o * jax.nn.silu(g_rw.astype(F32))

    xbc, conv_new = causal_dwconv(xbc_mb, s_conv, mb_conv_w, mb_conv_b)
    xbc = jax.nn.silu(xbc.astype(F32))
    xm, Bm, Cm = jnp.split(xbc, [MB_WIDTH, MB_WIDTH + MB_GROUPS * MB_N], axis=-1)
    xm = xm.reshape(Bt, T, MB_HEADS, MB_P)
    dt = jax.nn.softplus(dt_mb.astype(F32) + mb_dt_bias)
    A = -jnp.exp(mb_A_log.astype(F32))
    y, ssm_new = ssd_chunked(xm * dt[..., None], dt * A, Bm.reshape(Bt, T, MB_GROUPS, MB_N),
                             Cm.reshape(Bt, T, MB_GROUPS, MB_N), s_ssm)
    y = (y + mb_D[:, None] * xm).reshape(Bt, T, MB_WIDTH) * jax.nn.silu(z_mb.astype(F32))
    o_mb = rms_norm(y.reshape(Bt, T, MB_GROUPS, -1), mb_norm_w.reshape(MB_GROUPS, -1)).reshape(Bt, T, MB_WIDTH)

    gates = jax.nn.sigmoid(gate_raw.astype(F32)).reshape(Bt, T, N_BRANCH, D_MODEL)
    u = (gates[:, :, 0] * (o_hg.astype(dt_) @ w_o_hg)
         + gates[:, :, 1] * (o_rw.astype(dt_) @ w_o_rw)
         + gates[:, :, 2] * (o_mb.astype(dt_) @ w_o_mb))
    h = h + u.astype(dt_) @ w_out
    return h, (s_hg_new.astype(s_hg.dtype), s_rw_new.astype(s_rw.dtype), sh_new.astype(s_sh.dtype),
               ssm_new.astype(s_ssm.dtype), conv_new.astype(s_conv.dtype))


def run_trunk(x, states, lbs, layer_params, final_norm_w):
    h = x
    new = [[] for _ in states]
    for l in range(DEPTH):
        h, ns = mixer_layer(h, *[s[l] for s in states], lbs[l], *[p[l] for p in layer_params])
        for lst, s in zip(new, ns):
            lst.append(s)
    y = rms_norm(h, final_norm_w).astype(x.dtype)
    return y, [jnp.stack(lst) for lst in new]


def setup_inputs(seed: int = 0) -> dict:
    key = jax.random.key(seed)
    ks = iter(jax.random.split(key, 48))
    nrm = lambda shape, scale: jax.random.normal(next(ks), shape, F32) * scale
    L = DEPTH
    d = {}
    d['x_prompt'] = nrm((BATCH, SEQ, D_MODEL), 1.0)
    d['x_sample'] = nrm((DEC_BATCH, DEC_SEQ, D_MODEL), 1.0)
    d['state_hgrn'] = nrm((L, DEC_BATCH, HG_HEADS, HG_DK, HG_DV), 0.5)
    d['state_rwkv'] = nrm((L, DEC_BATCH, RW_HEADS, RW_N, RW_N), 0.3)
    d['state_rwkv_shift'] = nrm((L, DEC_BATCH, RW_SHIFT), 1.0)
    d['state_ssm'] = nrm((L, DEC_BATCH, MB_HEADS, MB_P, MB_N), 0.3)
    d['state_conv'] = nrm((L, DEC_BATCH, MB_CONV - 1, MB_XBC), 1.0)
    d['norm_w'] = 1.0 + nrm((L, D_MODEL), 0.02)
    d['w_in'] = nrm((L, D_MODEL, IN_COLS), D_MODEL ** -0.5)
    d['hg_lb'] = nrm((L, HG_QF), 1.0)
    d['hg_norm_w'] = 1.0 + nrm((L, HG_DV), 0.02)
    d['rw_mu'] = jax.random.uniform(next(ks), (L, RW_SHIFT), F32, 0.2, 0.8)
    d['rw_w0'] = nrm((L, RW_WIDTH), 0.5)
    d['rw_w2'] = nrm((L, RW_DECAY_LORA, RW_WIDTH), RW_DECAY_LORA ** -0.5)
    d['rw_a0'] = nrm((L, RW_WIDTH), 0.1)
    d['rw_a2'] = nrm((L, RW_AAA_LORA, RW_WIDTH), RW_AAA_LORA ** -0.5)
    d['rw_k_k'] = 0.85 + nrm((L, RW_WIDTH), 0.02)
    d['rw_k_a'] = 1.0 + nrm((L, RW_WIDTH), 0.02)
    d['rw_r_k'] = nrm((L, RW_HEADS, RW_N), 0.1)
    d['rw_ln_w'] = 1.0 + nrm((L, RW_WIDTH), 0.02)
    d['rw_ln_b'] = nrm((L, RW_WIDTH), 0.02)
    d['mb_conv_w'] = nrm((L, MB_CONV, MB_XBC), MB_CONV ** -0.5)
    d['mb_conv_b'] = nrm((L, MB_XBC), 0.02)
    dt0 = jnp.exp(jax.random.uniform(next(ks), (L, MB_HEADS), F32, np.log(1e-3), np.log(1e-1)))
    d['mb_dt_bias'] = dt0 + jnp.log(-jnp.expm1(-dt0))
    d['mb_A_log'] = jnp.log(jax.random.uniform(next(ks), (L, MB_HEADS), F32, 1.0, 16.0))
    d['mb_D'] = 1.0 + nrm((L, MB_HEADS), 0.02)
    d['mb_norm_w'] = 1.0 + nrm((L, MB_WIDTH), 0.02)
    d['w_o_hg'] = nrm((L, HG_WIDTH, D_MODEL), HG_WIDTH ** -0.5)
    d['w_o_rw'] = nrm((L, RW_WIDTH, D_MODEL), RW_WIDTH ** -0.5)
    d['w_o_mb'] = nrm((L, MB_WIDTH, D_MODEL), MB_WIDTH ** -0.5)
    d['w_out'] = nrm((L, D_MODEL, D_MODEL), D_MODEL ** -0.5)
    d['final_norm_w'] = 1.0 + nrm((D_MODEL,), 0.02)
    return d


def reference(x_prompt, x_sample, state_hgrn, state_rwkv, state_rwkv_shift, state_ssm, state_conv,
              norm_w, w_in, hg_lb, hg_norm_w, rw_mu, rw_w0, rw_w2, rw_a0, rw_a2, rw_k_k, rw_k_a, rw_r_k,
              rw_ln_w, rw_ln_b, mb_conv_w, mb_conv_b, mb_dt_bias, mb_A_log, mb_D, mb_norm_w,
              w_o_hg, w_o_rw, w_o_mb, w_out, final_norm_w):
    sm = jax.nn.softmax(hg_lb.astype(F32), axis=0)
    lbs = jnp.cumsum(sm, axis=0) - sm[0:1]
    layer_params = (norm_w, w_in, hg_norm_w, rw_mu, rw_w0, rw_w2, rw_a0, rw_a2, rw_k_k, rw_k_a, rw_r_k,
                    rw_ln_w, rw_ln_b, mb_conv_w, mb_conv_b, mb_dt_bias, mb_A_log, mb_D, mb_norm_w,
                    w_o_hg, w_o_rw, w_o_mb, w_out)
    Bp = x_prompt.shape[0]
    zdt = x_prompt.dtype
    p_init = (jnp.zeros((DEPTH, Bp, HG_HEADS, HG_DK, HG_DV), zdt),
              jnp.zeros((DEPTH, Bp, RW_HEADS, RW_N, RW_N), zdt),
              jnp.zeros((DEPTH, Bp, RW_SHIFT), zdt),
              jnp.zeros((DEPTH, Bp, MB_HEADS, MB_P, MB_N), zdt),
              jnp.zeros((DEPTH, Bp, MB_CONV - 1, MB_XBC), zdt))
    y_prompt, p_new = run_trunk(x_prompt, p_init, lbs, layer_params, final_norm_w)
    y_sample, s_new = run_trunk(x_sample, (state_hgrn, state_rwkv, state_rwkv_shift, state_ssm, state_conv),
                                lbs, layer_params, final_norm_w)
    p_hgrn, p_rwkv, p_shift, p_ssm, p_conv = p_new
    s_hgrn, s_rwkv, s_shift, s_ssm, s_conv = s_new
    return (y_prompt, y_sample, p_hgrn, p_rwkv, p_shift, p_ssm, p_conv, s_hgrn, s_rwkv, s_shift, s_ssm, s_conv)
```

```python
import functools

import numpy as np
import jax
import jax.numpy as jnp
from jax import lax
from jax.experimental import pallas as pl
from jax.experimental.pallas import tpu as pltpu

F32 = jnp.float32
BF16 = jnp.bfloat16
HI = lax.Precision.HIGHEST

D_MODEL = 1024
DEPTH = 2
HG_HEADS, HG_DK, HG_DV = 8, 128, 128
HG_F_MIN = 1e-20
RW_HEADS, RW_N = 16, 64
RW_LORA = 64
RW_SHIFT = 3 * D_MODEL + 2 * RW_LORA
RW_LN_EPS = 64e-5
MB_HEADS, MB_P, MB_GROUPS, MB_N, MB_CONV = 16, 64, 4, 128, 4
MB_XBC = D_MODEL + 2 * MB_GROUPS * MB_N
RMS_EPS = 1e-6

C_Q, C_F, C_I, C_GH, C_R, C_K, C_V, C_GR, C_X, C_BC, C_Z, C_G0, C_G1, C_G2, C_DT = range(15)
C_WDAD = 15 * 8
NCOL = 15 * 1024 + 128

HG_BLK = 16
RW_CHUNK = 64
MB_CHUNK = 64
SAMPLE_TB = 8

VMEM_LIMIT = 48 * 1024 * 1024

NN = (((1,), (0,)), ((), ()))
NT = (((1,), (1,)), ((), ()))
TN = (((0,), (0,)), ((), ()))


def _mm(a, b, dims=NN, hi=False):
    if hi:
        return lax.dot_general(a.astype(F32), b.astype(F32), dims, precision=HI, preferred_element_type=F32)
    return lax.dot_general(a.astype(BF16), b.astype(BF16), dims, preferred_element_type=F32)


def _sigmoid(x):
    return jax.nn.sigmoid(x)


def _silu(x):
    return x * jax.nn.sigmoid(x)


def _softplus(x):
    return jnp.maximum(x, 0.0) + jnp.log1p(jnp.exp(-jnp.abs(x)))


def _col_bcast(rows, width):
    ones = jnp.ones((rows.shape[0], width), BF16)
    p1 = rows.astype(BF16)
    r1 = rows - p1.astype(F32)
    p2 = r1.astype(BF16)
    p3 = (r1 - p2.astype(F32)).astype(BF16)
    out = lax.dot_general(p1, ones, TN, preferred_element_type=F32)
    out = out + lax.dot_general(p2, ones, TN, preferred_element_type=F32)
    return out + lax.dot_general(p3, ones, TN, preferred_element_type=F32)


def _params(sem):
    return pltpu.CompilerParams(dimension_semantics=sem, vmem_limit_bytes=VMEM_LIMIT)


def _inproj_kernel(x_ref, nw_ref, w_ref, o_ref, xn_ref):
    @pl.when(pl.program_id(1) == 0)
    def _():
        x = x_ref[...]
        ms = jnp.mean(x * x, axis=-1, keepdims=True)
        xn_ref[...] = (x * lax.rsqrt(ms + RMS_EPS) * nw_ref[...]).astype(BF16)

    o_ref[...] = jnp.dot(xn_ref[...], w_ref[...], preferred_element_type=F32)


def _inproj(x2d, nw, w):
    m = x2d.shape[0]
    tm = min(1024, m)
    tn = 1408
    return pl.pallas_call(
        _inproj_kernel,
        grid=(m // tm, NCOL // tn),
        in_specs=[
            pl.BlockSpec((tm, D_MODEL), lambda i, j: (i, 0)),
            pl.BlockSpec((1, D_MODEL), lambda i, j: (0, 0)),
            pl.BlockSpec((D_MODEL, tn), lambda i, j: (0, j)),
        ],
        out_specs=pl.BlockSpec((tm, tn), lambda i, j: (i, j)),
        out_shape=jax.ShapeDtypeStruct((m, NCOL), F32),
        scratch_shapes=[pltpu.VMEM((tm, D_MODEL), BF16)],
        compiler_params=_params(("parallel", "arbitrary")),
        name="inproj",
    )(x2d, nw, w)


def _merge_kernel(ohg, orw, omb, g0, g1, g2, h_ref, whg, wrw, wmb, wout, fw_ref, hn_ref, *y_ref):
    u = _sigmoid(g0[...]) * _mm(ohg[...], whg[...])
    u = u + _sigmoid(g1[...]) * _mm(orw[...], wrw[...])
    u = u + _sigmoid(g2[...]) * _mm(omb[...], wmb[...])
    hn = h_ref[...] + _mm(u, wout[...])
    hn_ref[...] = hn
    if y_ref:
        ms = jnp.mean(hn * hn, axis=-1, keepdims=True)
        y_ref[0][...] = hn * lax.rsqrt(ms + RMS_EPS) * fw_ref[...]


def _merge(ohg, orw, omb, proj, h2d, whg, wrw, wmb, wout, fw, final):
    m = h2d.shape[0]
    tm = min(256, m)
    row = lambda i: (i, 0)
    const = lambda i: (0, 0)
    wspec = pl.BlockSpec((D_MODEL, D_MODEL), const)
    out_shape = [jax.ShapeDtypeStruct((m, D_MODEL), F32)]
    out_specs = [pl.BlockSpec((tm, D_MODEL), row)]
    if final:
        out_shape.append(jax.ShapeDtypeStruct((m, D_MODEL), F32))
        out_specs.append(pl.BlockSpec((tm, D_MODEL), row))
    return pl.pallas_call(
        _merge_kernel,
        grid=(m // tm,),
        in_specs=[
            pl.BlockSpec((tm, D_MODEL), row), pl.BlockSpec((tm, D_MODEL), row), pl.BlockSpec((tm, D_MODEL), row),
            pl.BlockSpec((tm, D_MODEL), lambda i: (i, C_G0)),
            pl.BlockSpec((tm, D_MODEL), lambda i: (i, C_G1)),
            pl.BlockSpec((tm, D_MODEL), lambda i: (i, C_G2)),
            pl.BlockSpec((tm, D_MODEL), row),
            wspec, wspec, wspec, wspec,
            pl.BlockSpec((1, D_MODEL), const),
        ],
        out_specs=out_specs,
        out_shape=out_shape,
        compiler_params=_params(("parallel",)),
        name="merge",
    )(ohg, orw, omb, proj, proj, proj, h2d, whg, wrw, wmb, wout, fw)


def _hgrn_prep(q, f, lb):
    fg = lb + (1.0 - lb) * _sigmoid(f)
    logf = jnp.log(jnp.maximum(fg, HG_F_MIN))
    kx = (1.0 - lb) * _sigmoid(-f)
    qx = _silu(q) * (HG_DK ** -0.5)
    return qx, kx, logf


def _hgrn_post(o, g, nw):
    ms = jnp.mean(o * o, axis=-1, keepdims=True)
    return o * lax.rsqrt(ms + RMS_EPS) * nw * _silu(g)


def _hgrn_prompt_kernel(q_ref, f_ref, i_ref, g_ref, lb_ref, nw_ref, tril_ref, o_ref, s_ref,
                        st_ref, qs_ref, ks_ref, bs_ref, ob_ref, *, nblk):
    c = pl.program_id(2)

    @pl.when(c == 0)
    def _():
        st_ref[...] = jnp.zeros_like(st_ref)

    qx, kx, logf = _hgrn_prep(q_ref[...], f_ref[...], lb_ref[...])
    qs_ref[...] = qx
    ks_ref[...] = kx
    bs_ref[...] = _mm(tril_ref[...], logf, hi=True)
    row = lax.broadcasted_iota(jnp.int32, (HG_BLK, 1), 0)

    def blk(j, carry):
        r0 = pl.multiple_of(j * HG_BLK, HG_BLK)
        qb = qs_ref[pl.ds(r0, HG_BLK), :]
        kb = ks_ref[pl.ds(r0, HG_BLK), :]
        bb = bs_ref[pl.ds(r0, HG_BLK), :]
        vb = i_ref[pl.ds(r0, HG_BLK), :]
        acc = jnp.zeros((HG_BLK, HG_DV), F32)
        for jj in range(HG_BLK):
            t = qb * jnp.exp(jnp.minimum(bb - bb[jj:jj + 1, :], 0.0)) * kb[jj:jj + 1, :]
            a = jnp.sum(t, axis=-1, keepdims=True)
            acc = acc + jnp.where(row >= jj, a, 0.0) * vb[jj:jj + 1, :]
        st = st_ref[...]
        inter = _mm(qb * jnp.exp(bb), st, NT)
        ob_ref[pl.ds(r0, HG_BLK), :] = acc + inter
        b_end = bb[HG_BLK - 1:HG_BLK, :]
        kd = kb * jnp.exp(b_end - bb)
        st_ref[...] = st * jnp.exp(b_end) + _mm(vb, kd, TN)
        return carry

    lax.fori_loop(0, nblk, blk, 0)
    o_ref[...] = _hgrn_post(ob_ref[...], g_ref[...], nw_ref[...]).astype(o_ref.dtype)

    @pl.when(c == pl.num_programs(2) - 1)
    def _():
        s_ref[0, 0] = st_ref[...].T


def _hgrn_prompt(proj, lb, nw, bsz, t):
    tt = min(256, t)
    n_t = t // tt
    tril = jnp.asarray(np.kron(np.eye(tt // HG_BLK), np.tril(np.ones((HG_BLK, HG_BLK)))), F32)
    col = lambda blk: (lambda b, h, c: (b * n_t + c, blk * 8 + h))
    vec = pl.BlockSpec((1, 128), lambda b, h, c: (0, h))
    tile = pltpu.VMEM((tt, 128), F32)
    return pl.pallas_call(
        functools.partial(_hgrn_prompt_kernel, nblk=tt // HG_BLK),
        grid=(bsz, HG_HEADS, n_t),
        in_specs=[
            pl.BlockSpec((tt, 128), col(C_Q)), pl.BlockSpec((tt, 128), col(C_F)),
            pl.BlockSpec((tt, 128), col(C_I)), pl.BlockSpec((tt, 128), col(C_GH)),
            vec,
            pl.BlockSpec((1, 128), lambda b, h, c: (0, 0)),
            pl.BlockSpec((tt, tt), lambda b, h, c: (0, 0)),
        ],
        out_specs=[
            pl.BlockSpec((tt, 128), lambda b, h, c: (b * n_t + c, h)),
            pl.BlockSpec((1, 1, HG_DK, HG_DV), lambda b, h, c: (b, h, 0, 0)),
        ],
        out_shape=[
            jax.ShapeDtypeStruct((bsz * t, D_MODEL), BF16),
            jax.ShapeDtypeStruct((bsz, HG_HEADS, HG_DK, HG_DV), F32),
        ],
        scratch_shapes=[pltpu.VMEM((HG_DV, HG_DK), F32), tile, tile, tile, tile],
        compiler_params=_params(("parallel", "parallel", "arbitrary")),
        name="hgrn_prompt",
    )(proj, proj, proj, proj, lb, nw, tril)


def _hgrn_sample_kernel(q_ref, f_ref, i_ref, g_ref, lb_ref, nw_ref, s_ref, o_ref, so_ref,
                        qs_ref, ks_ref, fs_ref, ob_ref):
    qx, kx, logf = _hgrn_prep(q_ref[...], f_ref[...], lb_ref[...])
    qs_ref[...] = qx
    ks_ref[...] = kx
    fs_ref[...] = jnp.exp(logf)
    ob_ref[...] = jnp.zeros_like(ob_ref)
    rows = lax.broadcasted_iota(jnp.int32, (SAMPLE_TB, 1), 0)

    def per_seq(b, carry):
        sel = rows == b
        for h in range(HG_HEADS):
            sl = slice(h * 128, (h + 1) * 128)
            fcol = _col_bcast(jnp.where(sel, fs_ref[:, sl], 0.0), HG_DV)
            kv = _mm(jnp.where(sel, ks_ref[:, sl], 0.0), i_ref[:, sl], TN)
            s_new = fcol * s_ref[b, h] + kv
            so_ref[b, h] = s_new
            ob_ref[:, sl] += _mm(jnp.where(sel, qs_ref[:, sl], 0.0), s_new)
        return carry

    lax.fori_loop(0, SAMPLE_TB, per_seq, 0)
    nw = nw_ref[...]
    for h in range(HG_HEADS):
        sl = slice(h * 128, (h + 1) * 128)
        o_ref[:, sl] = _hgrn_post(ob_ref[:, sl], g_ref[:, sl], nw)


def _hgrn_sample(proj, lb, nw, state):
    bsz = state.shape[0]
    tb = SAMPLE_TB
    col = lambda blk: (lambda i: (i, blk))
    tile = pltpu.VMEM((tb, D_MODEL), F32)
    sspec = pl.BlockSpec((tb, HG_HEADS, HG_DK, HG_DV), lambda i: (i, 0, 0, 0))
    return pl.pallas_call(
        _hgrn_sample_kernel,
        grid=(bsz // tb,),
        in_specs=[
            pl.BlockSpec((tb, D_MODEL), col(C_Q)), pl.BlockSpec((tb, D_MODEL), col(C_F)),
            pl.BlockSpec((tb, D_MODEL), col(C_I)), pl.BlockSpec((tb, D_MODEL), col(C_GH)),
            pl.BlockSpec((1, D_MODEL), lambda i: (0, 0)),
            pl.BlockSpec((1, 128), lambda i: (0, 0)),
            sspec,
        ],
        out_specs=[pl.BlockSpec((tb, D_MODEL), lambda i: (i, 0)), sspec],
        out_shape=[
            jax.ShapeDtypeStruct((bsz, D_MODEL), F32),
            jax.ShapeDtypeStruct(state.shape, F32),
        ],
        scratch_shapes=[tile, tile, tile, tile],
        compiler_params=_params(("parallel",)),
        name="hgrn_sample",
    )(proj, proj, proj, proj, lb, nw, state)


def _rw_prep(r, k0, v, wdad, w0, a0, kk_w, ka_w, w2a2, ones_bd):
    lane = lax.broadcasted_iota(jnp.int32, wdad.shape, 1)
    lo = lane < RW_LORA
    lora_w = _mm(jnp.where(lo, jnp.tanh(wdad), 0.0), w2a2, hi=True)
    lora_a = _mm(jnp.where(lo, 0.0, wdad), w2a2, hi=True)
    w_log = -_softplus(-(w0 + lora_w)) - 0.5
    lw = -jnp.exp(w_log)
    a_sig = _sigmoid(a0 + lora_a)
    kk = k0 * kk_w
    nrm = jnp.sqrt(_mm(kk * kk, ones_bd, hi=True))
    kk = kk / jnp.maximum(nrm, 1e-12)
    k = k0 * (1.0 + (a_sig - 1.0) * ka_w)
    return r, k, v, lw, -kk, kk * a_sig


def _rw_post(o, r, k, v, g, rk_w, lnw, lnb, ones_bd):
    mu = _mm(o, ones_bd, hi=True) * (1.0 / RW_N)
    d = o - mu
    var = _mm(d * d, ones_bd, hi=True) * (1.0 / RW_N)
    on = d * lax.rsqrt(var + RW_LN_EPS) * lnw + lnb
    on = on + _mm(r * k * rk_w, ones_bd, hi=True) * v
    return on * _silu(g)


def _rwkv_prompt_kernel(r_ref, k_ref, v_ref, g_ref, wdad_ref, mur, muk, muv, muwd, w0_ref, a0_ref, kkw_ref,
                        kaw_ref, rkw_ref, lnw_ref, lnb_ref, w2a2_ref, tril_ref, ones_ref, o_ref, s_ref,
                        sbd_ref, pr_ref, pk_ref, pv_ref, pwd_ref):
    c = pl.program_id(2)
    n = RW_CHUNK

    @pl.when(c == 0)
    def _():
        sbd_ref[...] = jnp.zeros_like(sbd_ref)
        pr_ref[...] = jnp.zeros_like(pr_ref)
        pk_ref[...] = jnp.zeros_like(pk_ref)
        pv_ref[...] = jnp.zeros_like(pv_ref)
        pwd_ref[...] = jnp.zeros_like(pwd_ref)

    row = lax.broadcasted_iota(jnp.int32, (n, 128), 0)

    def shift(z_ref, prev_ref, mu_ref):
        z = z_ref[...]
        zp = jnp.where(row == 0, prev_ref[...], pltpu.roll(z, 1, axis=0))
        prev_ref[...] = z[n - 1:n, :]
        return z + (zp - z) * mu_ref[...]

    ones_bd = ones_ref[...]
    r, k, v, lw, av, bv = _rw_prep(
        shift(r_ref, pr_ref, mur), shift(k_ref, pk_ref, muk), shift(v_ref, pv_ref, muv),
        shift(wdad_ref, pwd_ref, muwd), w0_ref[...], a0_ref[...], kkw_ref[...], kaw_ref[...],
        w2a2_ref[...], ones_bd)

    tril = tril_ref[...]
    cl = _mm(tril, lw, hi=True)
    cl_end = cl[n - 1:n, :]
    e_in = jnp.exp(cl)
    e_out = jnp.exp(-cl)
    e_end = jnp.exp(cl_end - cl)
    a_h = av * jnp.exp(cl - lw)
    r_h = r * e_in
    b_c = bv * e_out
    k_c = k * e_out

    lane = lax.broadcasted_iota(jnp.int32, (n, 128), 1)
    ti = lax.broadcasted_iota(jnp.int32, (n, n), 0)
    si = lax.broadcasted_iota(jnp.int32, (n, n), 1)
    strict = si < ti
    incl = si <= ti
    eye = (si == ti).astype(F32)

    sbd = sbd_ref[...]
    w_all = _mm(a_h, sbd, NT)
    o_all = _mm(r_h, sbd, NT)
    per_head = []
    for e in range(2):
        me = (lane >= RW_N) if e else (lane < RW_N)
        lhs = jnp.concatenate([jnp.where(me, a_h, 0.0), jnp.where(me, r_h, 0.0)], axis=0)
        m_b = _mm(lhs, b_c, NT)
        m_k = _mm(lhs, k_c, NT)
        m_ab = jnp.where(strict, m_b[:n], 0.0)
        m_ak = jnp.where(strict, m_k[:n], 0.0)
        m_rb = jnp.where(incl, m_b[n:], 0.0)
        m_rk = jnp.where(incl, m_k[n:], 0.0)
        x = eye + m_ab
        mp = m_ab
        for _ in range(5):
            mp = _mm(mp, mp)
            x = x + _mm(x, mp)
        v_e = jnp.where(me, v, 0.0)
        w_all = w_all + _mm(m_ak, v_e)
        per_head.append((me, x, m_rb, m_rk, v_e))
    u_all = jnp.zeros((n, 128), F32)
    for me, x, _, _, _ in per_head:
        u_all = u_all + _mm(x, jnp.where(me, w_all, 0.0))
    for me, _, m_rb, m_rk, v_e in per_head:
        o_all = o_all + _mm(m_rb, jnp.where(me, u_all, 0.0)) + _mm(m_rk, v_e)

    upd = _mm(u_all, bv * e_end, TN) + _mm(v, k * e_end, TN)
    vi = lax.broadcasted_iota(jnp.int32, (128, 128), 0)
    ki = lax.broadcasted_iota(jnp.int32, (128, 128), 1)
    same_head = (vi < RW_N) == (ki < RW_N)
    sbd_new = sbd * jnp.exp(cl_end) + jnp.where(same_head, upd, 0.0)
    sbd_ref[...] = sbd_new

    o_ref[...] = _rw_post(o_all, r, k, v, g_ref[...], rkw_ref[...], lnw_ref[...], lnb_ref[...],
                          ones_bd).astype(o_ref.dtype)

    @pl.when(c == pl.num_programs(2) - 1)
    def _():
        s_ref[0, 0] = sbd_new


def _rw_consts():
    tril = jnp.asarray(np.tril(np.ones((RW_CHUNK, RW_CHUNK))), F32)
    ones_bd = jnp.asarray(np.kron(np.eye(2), np.ones((RW_N, RW_N))), F32)
    return tril, ones_bd


def _rwkv_prompt(proj, p, bsz, t):
    n = RW_CHUNK
    n_t = t // n
    tril, ones_bd = _rw_consts()
    col = lambda blk: (lambda b, q, c: (b * n_t + c, blk * 8 + q))
    vec = lambda off: pl.BlockSpec((1, 128), lambda b, q, c: (0, off + q))
    const = lambda shape: pl.BlockSpec(shape, lambda b, q, c: (0, 0))
    prev = pltpu.VMEM((1, 128), F32)
    o, sbd = pl.pallas_call(
        _rwkv_prompt_kernel,
        grid=(bsz, RW_HEADS // 2, n_t),
        in_specs=[
            pl.BlockSpec((n, 128), col(C_R)), pl.BlockSpec((n, 128), col(C_K)),
            pl.BlockSpec((n, 128), col(C_V)), pl.BlockSpec((n, 128), col(C_GR)),
            pl.BlockSpec((n, 128), lambda b, q, c: (b * n_t + c, C_WDAD)),
            vec(0), vec(8), vec(16),
            pl.BlockSpec((1, 128), lambda b, q, c: (0, 24)),
            vec(0), vec(0), vec(0), vec(0), vec(0), vec(0), vec(0),
            pl.BlockSpec((128, 128), lambda b, q, c: (0, q)),
            const((n, n)), const((128, 128)),
        ],
        out_specs=[
            pl.BlockSpec((n, 128), lambda b, q, c: (b * n_t + c, q)),
            pl.BlockSpec((1, 1, 128, 128), lambda b, q, c: (b, q, 0, 0)),
        ],
        out_shape=[
            jax.ShapeDtypeStruct((bsz * t, D_MODEL), BF16),
            jax.ShapeDtypeStruct((bsz, RW_HEADS // 2, 128, 128), F32),
        ],
        scratch_shapes=[pltpu.VMEM((128, 128), F32), prev, prev, prev, prev],
        compiler_params=_params(("parallel", "parallel", "arbitrary")),
        name="rwkv_prompt",
    )(proj, proj, proj, proj, proj, p["mu"], p["mu"], p["mu"], p["mu"], p["w0"], p["a0"], p["kk"], p["ka"],
      p["rk"], p["lnw"], p["lnb"], p["w2a2"], tril, ones_bd)
    s6 = sbd.reshape(bsz, RW_HEADS // 2, 2, RW_N, 2, RW_N)
    state = jnp.stack([s6[:, :, 0, :, 0, :], s6[:, :, 1, :, 1, :]], axis=2)
    return o, state.reshape(bsz, RW_HEADS, RW_N, RW_N)


def _rwkv_sample_kernel(r_ref, k_ref, v_ref, g_ref, wdad_ref, sh_ref, mu_ref, w0_ref, a0_ref, kkw_ref, kaw_ref,
                        rkw_ref, lnw_ref, lnb_ref, w2a2_ref, ones_ref, s_ref, o_ref, so_ref,
                        rs_ref, ks_ref, vs_ref, ob_ref, kr_ref, wr_ref, ar_ref, br_ref):
    ones_bd = ones_ref[...]
    tb = SAMPLE_TB

    def store_rows(ref, sl, x):
        for i in range(tb):
            ref[i, :, sl] = x[i:i + 1, :]

    def shift(z, lo, width):
        return z + (sh_ref[:, lo:lo + width] - z) * mu_ref[:, lo:lo + width]

    wdad = shift(wdad_ref[...], 3 * D_MODEL, 128)
    for q in range(RW_HEADS // 2):
        sl = slice(q * 128, (q + 1) * 128)
        r, k, v, lw, av, bv = _rw_prep(
            shift(r_ref[:, sl], q * 128, 128), shift(k_ref[:, sl], D_MODEL + q * 128, 128),
            shift(v_ref[:, sl], 2 * D_MODEL + q * 128, 128), wdad, w0_ref[:, sl], a0_ref[:, sl],
            kkw_ref[:, sl], kaw_ref[:, sl], w2a2_ref[:, sl], ones_bd)
        rs_ref[:, sl] = r
        ks_ref[:, sl] = k
        vs_ref[:, sl] = v
        store_rows(kr_ref, sl, k)
        store_rows(wr_ref, sl, jnp.exp(lw))
        store_rows(ar_ref, sl, av)
        store_rows(br_ref, sl, bv)
    ob_ref[...] = jnp.zeros_like(ob_ref)
    rows = lax.broadcasted_iota(jnp.int32, (tb, 1), 0)

    def per_seq(b, carry):
        sel = rows == b
        for q in range(RW_HEADS // 2):
            pair = slice(q * 128, (q + 1) * 128)
            a_row, w_row = ar_ref[b, :, pair], wr_ref[b, :, pair]
            b_row, k_row = br_ref[b, :, pair], kr_ref[b, :, pair]
            for e in range(2):
                h = 2 * q + e
                sl = slice(h * RW_N, (h + 1) * RW_N)
                half = slice(e * RW_N, (e + 1) * RW_N)
                s = s_ref[b, h]
                sa = jnp.sum(s * a_row[:, half], axis=-1, keepdims=True)
                vcol = _col_bcast(jnp.where(sel, vs_ref[:, sl], 0.0), RW_N)
                s_new = s * w_row[:, half] + sa * b_row[:, half] + vcol * k_row[:, half]
                so_ref[b, h] = s_new
                ob_ref[:, sl] += _mm(jnp.where(sel, rs_ref[:, sl], 0.0), s_new, NT)
        return carry

    lax.fori_loop(0, tb, per_seq, 0)
    for q in range(RW_HEADS // 2):
        sl = slice(q * 128, (q + 1) * 128)
        o_ref[:, sl] = _rw_post(ob_ref[:, sl], rs_ref[:, sl], ks_ref[:, sl], vs_ref[:, sl], g_ref[:, sl],
                                rkw_ref[:, sl], lnw_ref[:, sl], lnb_ref[:, sl], ones_bd)


def _rwkv_sample(proj, p, shift_state, state):
    bsz = state.shape[0]
    tb = SAMPLE_TB
    _, ones_bd = _rw_consts()
    col = lambda blk: (lambda i: (i, blk))
    vec = pl.BlockSpec((1, D_MODEL), lambda i: (0, 0))
    tile = pltpu.VMEM((tb, D_MODEL), F32)
    sspec = pl.BlockSpec((tb, RW_HEADS, RW_N, RW_N), lambda i: (i, 0, 0, 0))
    return pl.pallas_call(
        _rwkv_sample_kernel,
        grid=(bsz // tb,),
        in_specs=[
            pl.BlockSpec((tb, D_MODEL), col(C_R)), pl.BlockSpec((tb, D_MODEL), col(C_K)),
            pl.BlockSpec((tb, D_MODEL), col(C_V)), pl.BlockSpec((tb, D_MODEL), col(C_GR)),
            pl.BlockSpec((tb, 128), lambda i: (i, C_WDAD)),
            pl.BlockSpec((tb, RW_SHIFT), lambda i: (i, 0)),
            pl.BlockSpec((1, RW_SHIFT), lambda i: (0, 0)),
            vec, vec, vec, vec, vec, vec, vec,
            pl.BlockSpec((128, D_MODEL), lambda i: (0, 0)),
            pl.BlockSpec((128, 128), lambda i: (0, 0)),
            sspec,
        ],
        out_specs=[pl.BlockSpec((tb, D_MODEL), lambda i: (i, 0)), sspec],
        out_shape=[
            jax.ShapeDtypeStruct((bsz, D_MODEL), F32),
            jax.ShapeDtypeStruct(state.shape, F32),
        ],
        scratch_shapes=[tile] * 4 + [pltpu.VMEM((tb, 1, D_MODEL), F32)] * 4,
        compiler_params=_params(("parallel",)),
        name="rwkv_sample",
    )(proj, proj, proj, proj, proj, shift_state, p["mu"], p["w0"], p["a0"], p["kk"], p["ka"], p["rk"],
      p["lnw"], p["lnb"], p["w2a2"], ones_bd, state)


def _mb_post(y, xs, z, d_rep, nw):
    y = (y + d_rep * xs) * _silu(z)
    ms = jnp.mean(y * y, axis=-1, keepdims=True)
    return y * lax.rsqrt(ms + RMS_EPS) * nw


def _ssd_prompt_kernel(x_ref, b_ref, c_ref, z_ref, dt_ref, cwx, cwb, cwc, cbx, cbb, cbc, dtb_ref, alog_ref,
                       d_ref, nw_ref, tril_ref, o_ref, s_ref, h_ref, px_ref, pb_ref, pc_ref):
    c = pl.program_id(2)
    n = MB_CHUNK

    @pl.when(c == 0)
    def _():
        h_ref[...] = jnp.zeros_like(h_ref)
        px_ref[...] = jnp.zeros_like(px_ref)
        pb_ref[...] = jnp.zeros_like(pb_ref)
        pc_ref[...] = jnp.zeros_like(pc_ref)

    def conv(u_ref, prev_ref, w_ref, bias_ref):
        u = u_ref[...]
        prev = prev_ref[...]
        w = w_ref[...]
        row = lax.broadcasted_iota(jnp.int32, (8, u.shape[1]), 0)
        acc = u * w[MB_CONV - 1:MB_CONV, :] + bias_ref[...]
        for s in range(1, MB_CONV):
            us = pltpu.roll(u, s, axis=0)
            top = jnp.where(row < s, pltpu.roll(prev, s, axis=0), us[0:8])
            us = jnp.concatenate([top, us[8:]], axis=0)
            acc = acc + us * w[MB_CONV - 1 - s:MB_CONV - s, :]
        prev_ref[...] = u[n - 8:n, :]
        return _silu(acc)

    xs = conv(x_ref, px_ref, cwx, cbx)
    bm = conv(b_ref, pb_ref, cwb, cbb)
    cm = conv(c_ref, pc_ref, cwc, cbc)

    dt = _softplus(dt_ref[...] + dtb_ref[...])
    a = dt * (-jnp.exp(alog_ref[...]))
    xdt = xs * dt
    tril = tril_ref[...]
    cs = _mm(tril, a, hi=True)
    cs_end = cs[n - 1:n, :]

    ti = lax.broadcasted_iota(jnp.int32, (n, n), 0)
    si = lax.broadcasted_iota(jnp.int32, (n, n), 1)
    incl = si <= ti
    strict_f = (si < ti).astype(F32)
    lane = lax.broadcasted_iota(jnp.int32, (n, 4 * MB_P), 1)

    gmat = _mm(cm, bm, NT)
    h_all = h_ref[...]
    y = jnp.exp(cs) * _mm(cm, h_all, NT)
    for e in range(4):
        a_e = a[:, e * MB_P:(e + 1) * MB_P]
        dmat = _mm(tril, a_e * strict_f, hi=True)
        lmat = jnp.where(incl, jnp.exp(dmat), 0.0)
        me = (lane >= e * MB_P) & (lane < (e + 1) * MB_P)
        y = y + _mm(gmat * lmat, jnp.where(me, xdt, 0.0))

    rowscale = jnp.exp(_mm(a, jnp.ones((n, MB_N), F32), TN, hi=True))
    h_new = h_all * rowscale + _mm(xdt * jnp.exp(cs_end - cs), bm, TN)
    h_ref[...] = h_new

    o_ref[...] = _mb_post(y, xs, z_ref[...], d_ref[...], nw_ref[...]).astype(o_ref.dtype)

    @pl.when(c == pl.num_programs(2) - 1)
    def _():
        for e in range(4):
            s_ref[0, e] = h_new[e * MB_P:(e + 1) * MB_P, :]


def _ssd_prompt(proj, p, bsz, t):
    n = MB_CHUNK
    n_t = t // n
    tril = jnp.asarray(np.tril(np.ones((n, n))), F32)
    rows = lambda blk: (lambda b, g, c: (b * n_t + c, blk))
    x256 = lambda base: pl.BlockSpec((n, 256), lambda b, g, c: (b * n_t + c, base * 4 + g))
    v256 = pl.BlockSpec((1, 256), lambda b, g, c: (0, g))
    return pl.pallas_call(
        _ssd_prompt_kernel,
        grid=(bsz, MB_GROUPS, n_t),
        in_specs=[
            x256(C_X),
            pl.BlockSpec((n, 128), lambda b, g, c: (b * n_t + c, C_BC * 8 + g)),
            pl.BlockSpec((n, 128), lambda b, g, c: (b * n_t + c, C_BC * 8 + 4 + g)),
            x256(C_Z), x256(C_DT),
            pl.BlockSpec((MB_CONV, 256), lambda b, g, c: (0, g)),
            pl.BlockSpec((MB_CONV, 128), lambda b, g, c: (0, 8 + g)),
            pl.BlockSpec((MB_CONV, 128), lambda b, g, c: (0, 12 + g)),
            v256,
            pl.BlockSpec((1, 128), lambda b, g, c: (0, 8 + g)),
            pl.BlockSpec((1, 128), lambda b, g, c: (0, 12 + g)),
            v256, v256, v256, v256,
            pl.BlockSpec((n, n), lambda b, g, c: (0, 0)),
        ],
        out_specs=[
            pl.BlockSpec((n, 256), lambda b, g, c: (b * n_t + c, g)),
            pl.BlockSpec((1, 4, MB_P, MB_N), lambda b, g, c: (b, g, 0, 0)),
        ],
        out_shape=[
            jax.ShapeDtypeStruct((bsz * t, D_MODEL), BF16),
            jax.ShapeDtypeStruct((bsz, MB_HEADS, MB_P, MB_N), F32),
        ],
        scratch_shapes=[pltpu.VMEM((4 * MB_P, MB_N), F32), pltpu.VMEM((8, 256), F32),
                        pltpu.VMEM((8, 128), F32), pltpu.VMEM((8, 128), F32)],
        compiler_params=_params(("parallel", "parallel", "arbitrary")),
        name="ssd_prompt",
    )(proj, proj, proj, proj, proj, p["cw"], p["cw"], p["cw"], p["cb"], p["cb"], p["cb"], p["dtb"], p["alog"],
      p["d"], p["nw"], tril)


def _ssd_sample_kernel(xbc_ref, z_ref, dt_ref, cs_ref, cw_ref, cb_ref, dtb_ref, alog_ref, d_ref, nw_ref, s_ref,
                       o_ref, so_ref, xs_ref, xd_ref, bs_ref, cms_ref, dec_ref, ob_ref):
    tb = SAMPLE_TB
    w = cw_ref[...]
    acc = xbc_ref[...] * w[MB_CONV - 1:MB_CONV, :] + cb_ref[...]
    for i in range(MB_CONV - 1):
        acc = acc + cs_ref[i] * w[i:i + 1, :]
    xbc = _silu(acc)
    xs = xbc[:, :D_MODEL]
    dt = _softplus(dt_ref[...] + dtb_ref[...])
    xs_ref[...] = xs
    xd_ref[...] = xs * dt
    bs_ref[...] = xbc[:, D_MODEL:D_MODEL + MB_GROUPS * MB_N]
    cms_ref[...] = xbc[:, D_MODEL + MB_GROUPS * MB_N:]
    dec = jnp.exp(dt * (-jnp.exp(alog_ref[...])))
    for i in range(tb):
        dec_ref[i] = dec[i:i + 1, :]
    ob_ref[...] = jnp.zeros_like(ob_ref)
    rows = lax.broadcasted_iota(jnp.int32, (tb, 1), 0)

    def per_seq(b, carry):
        sel = rows == b
        for h in range(MB_HEADS):
            g = h // (MB_HEADS // MB_GROUPS)
            sl = slice(h * MB_P, (h + 1) * MB_P)
            gl = slice(g * MB_N, (g + 1) * MB_N)
            pair = slice((h // 2) * 128, (h // 2 + 1) * 128)
            dec = dec_ref[b, :, pair][:, (h % 2) * MB_P:(h % 2) * MB_P + 1]
            upd = _mm(jnp.where(sel, xd_ref[:, sl], 0.0), bs_ref[:, gl], TN)
            h_new = dec * s_ref[b, h] + upd
            so_ref[b, h] = h_new
            ob_ref[:, sl] += _mm(jnp.where(sel, cms_ref[:, gl], 0.0), h_new, NT)
        return carry

    lax.fori_loop(0, tb, per_seq, 0)
    for g in range(MB_GROUPS):
        sl = slice(g * 256, (g + 1) * 256)
        o_ref[:, sl] = _mb_post(ob_ref[:, sl], xs_ref[:, sl], z_ref[:, sl], d_ref[:, sl], nw_ref[:, sl])


def _ssd_sample(proj, p, conv_state, state):
    bsz = state.shape[0]
    tb = SAMPLE_TB
    vec = lambda w: pl.BlockSpec((1, w), lambda i: (0, 0))
    tile = pltpu.VMEM((tb, D_MODEL), F32)
    half = pltpu.VMEM((tb, MB_GROUPS * MB_N), F32)
    sspec = pl.BlockSpec((tb, MB_HEADS, MB_P, MB_N), lambda i: (i, 0, 0, 0))
    return pl.pallas_call(
        _ssd_sample_kernel,
        grid=(bsz // tb,),
        in_specs=[
            pl.BlockSpec((tb, MB_XBC), lambda i: (i, C_X // 2)),
            pl.BlockSpec((tb, D_MODEL), lambda i: (i, C_Z)),
            pl.BlockSpec((tb, D_MODEL), lambda i: (i, C_DT)),
            pl.BlockSpec((MB_CONV - 1, tb, MB_XBC), lambda i: (0, i, 0)),
            pl.BlockSpec((MB_CONV, MB_XBC), lambda i: (0, 0)),
            vec(MB_XBC), vec(D_MODEL), vec(D_MODEL), vec(D_MODEL), vec(D_MODEL),
            sspec,
        ],
        out_specs=[pl.BlockSpec((tb, D_MODEL), lambda i: (i, 0)), sspec],
        out_shape=[
            jax.ShapeDtypeStruct((bsz, D_MODEL), F32),
            jax.ShapeDtypeStruct(state.shape, F32),
        ],
        scratch_shapes=[tile, tile, half, half, pltpu.VMEM((tb, 1, D_MODEL), F32), tile],
        compiler_params=_params(("parallel",)),
        name="ssd_sample",
    )(proj, proj, proj, conv_state, p["cw"], p["cb"], p["dtb"], p["alog"], p["d"], p["nw"], state)


def _prep_w_in(w_in):
    o = np.cumsum([0, 1024, 1024, 1024, 1024, RW_SHIFT, 1024, 1024, MB_XBC, MB_HEADS, 3 * D_MODEL])
    zrw, grw, zmb, xbc, dtc, gate = o[4], o[5], o[6], o[7], o[8], o[9]
    parts = [
        w_in[..., :zrw], w_in[..., zrw:zrw + 3 * D_MODEL], w_in[..., grw:zmb], w_in[..., xbc:dtc],
        w_in[..., zmb:xbc], w_in[..., gate:],
        jnp.repeat(w_in[..., dtc:gate], MB_P, axis=-1),
        w_in[..., zrw + 3 * D_MODEL:grw],
    ]
    return jnp.concatenate(parts, axis=-1).astype(BF16)


def _rep(v):
    return jnp.repeat(v, MB_P, axis=-1)[:, None, :]


def kernel(x_prompt, x_sample, state_hgrn, state_rwkv, state_rwkv_shift, state_ssm, state_conv, norm_w, w_in, hg_lb, hg_norm_w, rw_mu, rw_w0, rw_w2, rw_a0, rw_a2, rw_k_k, rw_k_a, rw_r_k, rw_ln_w, rw_ln_b, mb_conv_w, mb_conv_b, mb_dt_bias, mb_A_log, mb_D, mb_norm_w, w_o_hg, w_o_rw, w_o_mb, w_out, final_norm_w):
    bp, t, _ = x_prompt.shape
    bs = x_sample.shape[0]
    sm = jax.nn.softmax(hg_lb.astype(F32), axis=0)
    lbs = jnp.cumsum(sm, axis=0) - sm[0:1]
    w_in_r = _prep_w_in(w_in)
    row = lambda a: a[:, None, :]
    dtb, alog, drep = _rep(mb_dt_bias), _rep(mb_A_log), _rep(mb_D)
    w2a2 = jnp.concatenate([rw_w2, rw_a2], axis=1)
    whg, wrw, wmb, wout = (w.astype(BF16) for w in (w_o_hg, w_o_rw, w_o_mb, w_out))
    fw = final_norm_w[None, :]

    hp = x_prompt.reshape(bp * t, D_MODEL)
    hs = x_sample.reshape(bs, D_MODEL)
    p_states, s_states = [], []
    yp = ys = None
    for l in range(DEPTH):
        rw_p = dict(mu=row(rw_mu)[l], w0=row(rw_w0)[l], a0=row(rw_a0)[l], kk=row(rw_k_k)[l], ka=row(rw_k_a)[l],
                    rk=rw_r_k[l].reshape(1, D_MODEL), lnw=row(rw_ln_w)[l], lnb=row(rw_ln_b)[l], w2a2=w2a2[l])
        mb_p = dict(cw=mb_conv_w[l], cb=row(mb_conv_b)[l], dtb=dtb[l], alog=alog[l], d=drep[l],
                    nw=row(mb_norm_w)[l])
        lb = row(lbs)[l]
        nw_hg = hg_norm_w[l][None, :]
        final = l == DEPTH - 1

        proj = _inproj(hp, row(norm_w)[l], w_in_r[l])
        o_hg, p_hg = _hgrn_prompt(proj, lb, nw_hg, bp, t)
        o_rw, p_rw = _rwkv_prompt(proj, rw_p, bp, t)
        o_mb, p_ssm = _ssd_prompt(proj, mb_p, bp, t)
        proj3 = proj.reshape(bp, t, NCOL)
        p_shift = jnp.concatenate([proj3[:, -1, C_R * 1024:C_GR * 1024], proj3[:, -1, C_WDAD * 128:]], axis=-1)
        p_conv = proj3[:, t - (MB_CONV - 1):, C_X * 1024:C_Z * 1024]
        res = _merge(o_hg, o_rw, o_mb, proj, hp, whg[l], wrw[l], wmb[l], wout[l], fw, final)
        hp = res[0]
        if final:
            yp = res[1]
        p_states.append((p_hg, p_rw, p_shift, p_ssm, p_conv))

        proj = _inproj(hs, row(norm_w)[l], w_in_r[l])
        o_hg, s_hg = _hgrn_sample(proj, lb, nw_hg, state_hgrn[l])
        o_rw, s_rw = _rwkv_sample(proj, rw_p, state_rwkv_shift[l], state_rwkv[l])
        o_mb, s_ssm = _ssd_sample(proj, mb_p, jnp.swapaxes(state_conv[l], 0, 1), state_ssm[l])
        s_shift = jnp.concatenate([proj[:, C_R * 1024:C_GR * 1024], proj[:, C_WDAD * 128:]], axis=-1)
        s_conv = jnp.concatenate([state_conv[l][:, 1:], proj[:, None, C_X * 1024:C_Z * 1024]], axis=1)
        res = _merge(o_hg, o_rw, o_mb, proj, hs, whg[l], wrw[l], wmb[l], wout[l], fw, final)
        hs = res[0]
        if final:
            ys = res[1]
        s_states.append((s_hg, s_rw, s_shift, s_ssm, s_conv))

    stack = lambda states, i: jnp.stack([s[i] for s in states])
    return (yp.reshape(bp, t, D_MODEL), ys.reshape(bs, 1, D_MODEL),
            *[stack(p_states, i) for i in range(5)], *[stack(s_states, i) for i in range(5)])
```

```python
import functools

import numpy as np
import jax
import jax.numpy as jnp
from jax import lax
from jax.experimental import pallas as pl
from jax.experimental.pallas import tpu as pltpu

F32 = jnp.float32
BF16 = jnp.bfloat16
HI = lax.Precision.HIGHEST

D_MODEL = 1024
DEPTH = 2
HG_HEADS, HG_DK, HG_DV = 8, 128, 128
HG_F_MIN = 1e-20
RW_HEADS, RW_N = 16, 64
RW_LORA = 64
RW_SHIFT = 3 * D_MODEL + 2 * RW_LORA
RW_LN_EPS = 64e-5
MB_HEADS, MB_P, MB_GROUPS, MB_N, MB_CONV = 16, 64, 4, 128, 4
MB_XBC = D_MODEL + 2 * MB_GROUPS * MB_N
RMS_EPS = 1e-6

C_Q, C_F, C_I, C_GH, C_R, C_K, C_V, C_GR, C_X, C_BC, C_Z, C_G0, C_G1, C_G2, C_DT = range(15)
C_WDAD = 15 * 8
NCOL = 15 * 1024 + 128

HG_BLK = 16
RW_CHUNK = 64
MB_CHUNK = 64
SAMPLE_TB = 8

VMEM_LIMIT = 48 * 1024 * 1024

NN = (((1,), (0,)), ((), ()))
NT = (((1,), (1,)), ((), ()))
TN = (((0,), (0,)), ((), ()))


def _mm(a, b, dims=NN, hi=False):
    if hi:
        return lax.dot_general(a.astype(F32), b.astype(F32), dims, precision=HI, preferred_element_type=F32)
    return lax.dot_general(a.astype(BF16), b.astype(BF16), dims, preferred_element_type=F32)


def _mm_exact(e, x, dims=NN, x_is_lhs=False, passes=3):
    e = e.astype(BF16)
    acc = None
    for _ in range(passes):
        p = x.astype(BF16)
        x = x - p.astype(F32)
        t = (lax.dot_general(p, e, dims, preferred_element_type=F32) if x_is_lhs
             else lax.dot_general(e, p, dims, preferred_element_type=F32))
        acc = t if acc is None else acc + t
    return acc


def _sigmoid(x):
    return jax.nn.sigmoid(x)


def _silu(x):
    return x * jax.nn.sigmoid(x)


def _softplus(x):
    return jnp.maximum(x, 0.0) + jnp.log1p(jnp.exp(-jnp.abs(x)))


def _col_bcast(rows, width):
    ones = jnp.ones((rows.shape[0], width), BF16)
    p1 = rows.astype(BF16)
    r1 = rows - p1.astype(F32)
    p2 = r1.astype(BF16)
    p3 = (r1 - p2.astype(F32)).astype(BF16)
    out = lax.dot_general(p1, ones, TN, preferred_element_type=F32)
    out = out + lax.dot_general(p2, ones, TN, preferred_element_type=F32)
    return out + lax.dot_general(p3, ones, TN, preferred_element_type=F32)


def _params(sem):
    return pltpu.CompilerParams(dimension_semantics=sem, vmem_limit_bytes=VMEM_LIMIT)


def _inproj_kernel(x_ref, nw_ref, w_ref, o_ref, xn_ref):
    @pl.when(pl.program_id(1) == 0)
    def _():
        x = x_ref[...]
        ms = jnp.mean(x * x, axis=-1, keepdims=True)
        xn_ref[...] = (x * lax.rsqrt(ms + RMS_EPS) * nw_ref[...]).astype(BF16)

    o_ref[...] = jnp.dot(xn_ref[...], w_ref[...], preferred_element_type=F32)


def _inproj(x2d, nw, w):
    m = x2d.shape[0]
    tm = min(1024, m)
    tn = 1408
    return pl.pallas_call(
        _inproj_kernel,
        grid=(m // tm, NCOL // tn),
        in_specs=[
            pl.BlockSpec((tm, D_MODEL), lambda i, j: (i, 0)),
            pl.BlockSpec((1, D_MODEL), lambda i, j: (0, 0)),
            pl.BlockSpec((D_MODEL, tn), lambda i, j: (0, j)),
        ],
        out_specs=pl.BlockSpec((tm, tn), lambda i, j: (i, j)),
        out_shape=jax.ShapeDtypeStruct((m, NCOL), F32),
        scratch_shapes=[pltpu.VMEM((tm, D_MODEL), BF16)],
        compiler_params=_params(("parallel", "arbitrary")),
        name="inproj",
    )(x2d, nw, w)


def _merge_kernel(ohg, orw, omb, g0, g1, g2, h_ref, whg, wrw, wmb, wout, fw_ref, hn_ref, *y_ref):
    u = _sigmoid(g0[...]) * _mm(ohg[...], whg[...])
    u = u + _sigmoid(g1[...]) * _mm(orw[...], wrw[...])
    u = u + _sigmoid(g2[...]) * _mm(omb[...], wmb[...])
    hn = h_ref[...] + _mm(u, wout[...])
    hn_ref[...] = hn
    if y_ref:
        ms = jnp.mean(hn * hn, axis=-1, keepdims=True)
        y_ref[0][...] = hn * lax.rsqrt(ms + RMS_EPS) * fw_ref[...]


def _merge(ohg, orw, omb, proj, h2d, whg, wrw, wmb, wout, fw, final):
    m = h2d.shape[0]
    tm = min(256, m)
    row = lambda i: (i, 0)
    const = lambda i: (0, 0)
    wspec = pl.BlockSpec((D_MODEL, D_MODEL), const)
    out_shape = [jax.ShapeDtypeStruct((m, D_MODEL), F32)]
    out_specs = [pl.BlockSpec((tm, D_MODEL), row)]
    if final:
        out_shape.append(jax.ShapeDtypeStruct((m, D_MODEL), F32))
        out_specs.append(pl.BlockSpec((tm, D_MODEL), row))
    return pl.pallas_call(
        _merge_kernel,
        grid=(m // tm,),
        in_specs=[
            pl.BlockSpec((tm, D_MODEL), row), pl.BlockSpec((tm, D_MODEL), row), pl.BlockSpec((tm, D_MODEL), row),
            pl.BlockSpec((tm, D_MODEL), lambda i: (i, C_G0)),
            pl.BlockSpec((tm, D_MODEL), lambda i: (i, C_G1)),
            pl.BlockSpec((tm, D_MODEL), lambda i: (i, C_G2)),
            pl.BlockSpec((tm, D_MODEL), row),
            wspec, wspec, wspec, wspec,
            pl.BlockSpec((1, D_MODEL), const),
        ],
        out_specs=out_specs,
        out_shape=out_shape,
        compiler_params=_params(("parallel",)),
        name="merge",
    )(ohg, orw, omb, proj, proj, proj, h2d, whg, wrw, wmb, wout, fw)


def _hgrn_prep(q, f, lb):
    fg = lb + (1.0 - lb) * _sigmoid(f)
    logf = jnp.log(jnp.maximum(fg, HG_F_MIN))
    kx = (1.0 - lb) * _sigmoid(-f)
    qx = _silu(q) * (HG_DK ** -0.5)
    return qx, kx, logf


def _hgrn_post(o, g, nw):
    ms = jnp.mean(o * o, axis=-1, keepdims=True)
    return o * lax.rsqrt(ms + RMS_EPS) * nw * _silu(g)


def _hgrn_prompt_kernel(q_ref, f_ref, i_ref, g_ref, lb_ref, nw_ref, tril_ref, o_ref, s_ref,
                        st_ref, qs_ref, ks_ref, bs_ref, ob_ref, *, nblk):
    c = pl.program_id(1)

    @pl.when(c == 0)
    def _():
        st_ref[...] = jnp.zeros_like(st_ref)

    tril = tril_ref[...]
    heads = [slice(h * 128, (h + 1) * 128) for h in range(HG_HEADS)]
    for sl in heads:
        qx, kx, logf = _hgrn_prep(q_ref[:, sl], f_ref[:, sl], lb_ref[:, sl])
        qs_ref[:, sl] = qx
        ks_ref[:, sl] = kx
        bs_ref[:, sl] = _mm_exact(tril, logf)
    row = lax.broadcasted_iota(jnp.int32, (HG_BLK, 1), 0)

    def blk(j, carry):
        r0 = pl.multiple_of(j * HG_BLK, HG_BLK)
        rows = pl.ds(r0, HG_BLK)
        for h, sl in enumerate(heads):
            qb, kb, bb, vb = qs_ref[rows, sl], ks_ref[rows, sl], bs_ref[rows, sl], i_ref[rows, sl]
            acc = jnp.zeros((HG_BLK, HG_DV), F32)
            for jj in range(HG_BLK):
                t = qb * jnp.exp(jnp.minimum(bb - bb[jj:jj + 1, :], 0.0)) * kb[jj:jj + 1, :]
                a = jnp.sum(t, axis=-1, keepdims=True)
                acc = acc + jnp.where(row >= jj, a, 0.0) * vb[jj:jj + 1, :]
            st = st_ref[h]
            ob_ref[rows, sl] = acc + _mm(qb * jnp.exp(bb), st, NT)
            b_end = bb[HG_BLK - 1:HG_BLK, :]
            st_ref[h] = st * jnp.exp(b_end) + _mm(vb, kb * jnp.exp(b_end - bb), TN)
        return carry

    lax.fori_loop(0, nblk, blk, 0)
    nw = nw_ref[...]
    for sl in heads:
        o_ref[:, sl] = _hgrn_post(ob_ref[:, sl], g_ref[:, sl], nw).astype(o_ref.dtype)

    @pl.when(c == pl.num_programs(1) - 1)
    def _():
        for h in range(HG_HEADS):
            s_ref[0, h] = st_ref[h].T


def _hgrn_prompt(proj, lb, nw, bsz, t):
    tt = min(256, t)
    n_t = t // tt
    tril = jnp.asarray(np.kron(np.eye(tt // HG_BLK), np.tril(np.ones((HG_BLK, HG_BLK)))), BF16)
    col = lambda blk: (lambda b, c: (b * n_t + c, blk))
    const = lambda shape: pl.BlockSpec(shape, lambda b, c: (0, 0))
    tile = pltpu.VMEM((tt, D_MODEL), F32)
    return pl.pallas_call(
        functools.partial(_hgrn_prompt_kernel, nblk=tt // HG_BLK),
        grid=(bsz, n_t),
        in_specs=[
            pl.BlockSpec((tt, D_MODEL), col(C_Q)), pl.BlockSpec((tt, D_MODEL), col(C_F)),
            pl.BlockSpec((tt, D_MODEL), col(C_I)), pl.BlockSpec((tt, D_MODEL), col(C_GH)),
            const((1, D_MODEL)), const((1, 128)), const((tt, tt)),
        ],
        out_specs=[
            pl.BlockSpec((tt, D_MODEL), lambda b, c: (b * n_t + c, 0)),
            pl.BlockSpec((1, HG_HEADS, HG_DK, HG_DV), lambda b, c: (b, 0, 0, 0)),
        ],
        out_shape=[
            jax.ShapeDtypeStruct((bsz * t, D_MODEL), BF16),
            jax.ShapeDtypeStruct((bsz, HG_HEADS, HG_DK, HG_DV), F32),
        ],
        scratch_shapes=[pltpu.VMEM((HG_HEADS, HG_DV, HG_DK), F32), tile, tile, tile, tile],
        compiler_params=_params(("parallel", "arbitrary")),
        name="hgrn_prompt",
    )(proj, proj, proj, proj, lb, nw, tril)


def _hgrn_sample_kernel(q_ref, f_ref, i_ref, g_ref, lb_ref, nw_ref, s_ref, o_ref, so_ref,
                        qs_ref, ks_ref, fs_ref, ob_ref):
    qx, kx, logf = _hgrn_prep(q_ref[...], f_ref[...], lb_ref[...])
    qs_ref[...] = qx
    ks_ref[...] = kx
    fs_ref[...] = jnp.exp(logf)
    ob_ref[...] = jnp.zeros_like(ob_ref)
    rows = lax.broadcasted_iota(jnp.int32, (SAMPLE_TB, 1), 0)

    def per_seq(b, carry):
        sel = rows == b
        for h in range(HG_HEADS):
            sl = slice(h * 128, (h + 1) * 128)
            fcol = _col_bcast(jnp.where(sel, fs_ref[:, sl], 0.0), HG_DV)
            kv = _mm(jnp.where(sel, ks_ref[:, sl], 0.0), i_ref[:, sl], TN)
            s_new = fcol * s_ref[b, h] + kv
            so_ref[b, h] = s_new
            ob_ref[:, sl] += _mm(jnp.where(sel, qs_ref[:, sl], 0.0), s_new)
        return carry

    lax.fori_loop(0, SAMPLE_TB, per_seq, 0)
    nw = nw_ref[...]
    for h in range(HG_HEADS):
        sl = slice(h * 128, (h + 1) * 128)
        o_ref[:, sl] = _hgrn_post(ob_ref[:, sl], g_ref[:, sl], nw)


def _hgrn_sample(proj, lb, nw, state):
    bsz = state.shape[0]
    tb = SAMPLE_TB
    col = lambda blk: (lambda i: (i, blk))
    tile = pltpu.VMEM((tb, D_MODEL), F32)
    sspec = pl.BlockSpec((tb, HG_HEADS, HG_DK, HG_DV), lambda i: (i, 0, 0, 0))
    return pl.pallas_call(
        _hgrn_sample_kernel,
        grid=(bsz // tb,),
        in_specs=[
            pl.BlockSpec((tb, D_MODEL), col(C_Q)), pl.BlockSpec((tb, D_MODEL), col(C_F)),
            pl.BlockSpec((tb, D_MODEL), col(C_I)), pl.BlockSpec((tb, D_MODEL), col(C_GH)),
            pl.BlockSpec((1, D_MODEL), lambda i: (0, 0)),
            pl.BlockSpec((1, 128), lambda i: (0, 0)),
            sspec,
        ],
        out_specs=[pl.BlockSpec((tb, D_MODEL), lambda i: (i, 0)), sspec],
        out_shape=[
            jax.ShapeDtypeStruct((bsz, D_MODEL), F32),
            jax.ShapeDtypeStruct(state.shape, F32),
        ],
        scratch_shapes=[tile, tile, tile, tile],
        compiler_params=_params(("parallel",)),
        name="hgrn_sample",
    )(proj, proj, proj, proj, lb, nw, state)


def _rw_lora(wdad, w2a2):
    lo = lax.broadcasted_iota(jnp.int32, wdad.shape, 1) < RW_LORA
    lora_w = _mm(jnp.where(lo, jnp.tanh(wdad), 0.0), w2a2, hi=True)
    lora_a = _mm(jnp.where(lo, 0.0, wdad), w2a2, hi=True)
    return lora_w, lora_a


def _rw_prep(r, k0, v, lora_w, lora_a, w0, a0, kk_w, ka_w, ones_bd):
    w_log = -_softplus(-(w0 + lora_w)) - 0.5
    lw = -jnp.exp(w_log)
    a_sig = _sigmoid(a0 + lora_a)
    kk = k0 * kk_w
    nrm = jnp.sqrt(_mm_exact(ones_bd, kk * kk, x_is_lhs=True))
    kk = kk / jnp.maximum(nrm, 1e-12)
    k = k0 * (1.0 + (a_sig - 1.0) * ka_w)
    return r, k, v, lw, -kk, kk * a_sig


def _rw_posts(os_, rs, ks, vs, gs, rk_ws, lnws, lnbs, ones_bd):
    idx = range(len(os_))
    seg = lambda x: _mm_exact(ones_bd, x, x_is_lhs=True)
    ds = [os_[i] - seg(os_[i]) * (1.0 / RW_N) for i in idx]
    var = [seg(d * d) * (1.0 / RW_N) for d in ds]
    bonus = [seg(rs[i] * ks[i] * rk_ws[i]) for i in idx]
    return [(ds[i] * lax.rsqrt(var[i] + RW_LN_EPS) * lnws[i] + lnbs[i] + bonus[i] * vs[i]) * _silu(gs[i])
            for i in idx]


def _rw_post(o, r, k, v, g, rk_w, lnw, lnb, ones_bd):
    return _rw_posts([o], [r], [k], [v], [g], [rk_w], [lnw], [lnb], ones_bd)[0]


def _rw_chunks(rs, ks, vs, lws, avs, bvs, sbds, tril):
    n = RW_CHUNK
    pairs = range(len(rs))
    lane = lax.broadcasted_iota(jnp.int32, (n, 128), 1)
    masks = (lane < RW_N, lane >= RW_N)
    ti = lax.broadcasted_iota(jnp.int32, (n, n), 0)
    si = lax.broadcasted_iota(jnp.int32, (n, n), 1)
    strict = si < ti
    incl = si <= ti
    eye = (si == ti).astype(F32)

    cls = [_mm_exact(tril, lws[q]) for q in pairs]
    a_h = [avs[q] * jnp.exp(cls[q] - lws[q]) for q in pairs]
    r_h = [rs[q] * jnp.exp(cls[q]) for q in pairs]
    b_c = [bvs[q] * jnp.exp(-cls[q]) for q in pairs]
    k_c = [ks[q] * jnp.exp(-cls[q]) for q in pairs]
    w_all = [_mm(a_h[q], sbds[q], NT) for q in pairs]
    o_all = [_mm(r_h[q], sbds[q], NT) for q in pairs]

    heads = [(q, e) for q in pairs for e in range(2)]
    lhs = [jnp.concatenate([jnp.where(masks[e], a_h[q], 0.0), jnp.where(masks[e], r_h[q], 0.0)], axis=0)
           for q, e in heads]
    m_b = [_mm(lhs[i], b_c[q], NT) for i, (q, e) in enumerate(heads)]
    m_k = [_mm(lhs[i], k_c[q], NT) for i, (q, e) in enumerate(heads)]
    m_ab = [jnp.where(strict, m[:n], 0.0) for m in m_b]
    m_ak = [jnp.where(strict, m[:n], 0.0) for m in m_k]
    m_rb = [jnp.where(incl, m[n:], 0.0) for m in m_b]
    m_rk = [jnp.where(incl, m[n:], 0.0) for m in m_k]
    xs = [eye + m for m in m_ab]
    mps = m_ab
    for _ in range(5):
        mps = [_mm(m, m) for m in mps]
        xs = [x + _mm(x, m) for x, m in zip(xs, mps)]
    v_e = [jnp.where(masks[e], vs[q], 0.0) for q, e in heads]
    for i, (q, e) in enumerate(heads):
        w_all[q] = w_all[q] + _mm(m_ak[i], v_e[i])
    u_parts = [_mm(xs[i], jnp.where(masks[e], w_all[q], 0.0)) for i, (q, e) in enumerate(heads)]
    u_all = [u_parts[2 * q] + u_parts[2 * q + 1] for q in pairs]
    for i, (q, e) in enumerate(heads):
        o_all[q] = o_all[q] + _mm(m_rb[i], jnp.where(masks[e], u_all[q], 0.0)) + _mm(m_rk[i], v_e[i])

    vi = lax.broadcasted_iota(jnp.int32, (128, 128), 0)
    ki = lax.broadcasted_iota(jnp.int32, (128, 128), 1)
    same_head = (vi < RW_N) == (ki < RW_N)
    new = []
    for q in pairs:
        cl_end = cls[q][n - 1:n, :]
        e_end = jnp.exp(cl_end - cls[q])
        upd = _mm(u_all[q], bvs[q] * e_end, TN) + _mm(vs[q], ks[q] * e_end, TN)
        new.append(sbds[q] * jnp.exp(cl_end) + jnp.where(same_head, upd, 0.0))
    return o_all, new


def _rwkv_prompt_kernel(r_ref, k_ref, v_ref, g_ref, wdad_ref, mu_ref, w0_ref, a0_ref, kkw_ref, kaw_ref, rkw_ref,
                        lnw_ref, lnb_ref, w2a2_ref, tril_ref, ones_ref, o_ref, s_ref,
                        sbd_ref, pr_ref, pk_ref, pv_ref, pwd_ref, lw_ref, la_ref):
    c = pl.program_id(1)
    n = RW_CHUNK

    @pl.when(c == 0)
    def _():
        sbd_ref[...] = jnp.zeros_like(sbd_ref)
        pr_ref[...] = jnp.zeros_like(pr_ref)
        pk_ref[...] = jnp.zeros_like(pk_ref)
        pv_ref[...] = jnp.zeros_like(pv_ref)
        pwd_ref[...] = jnp.zeros_like(pwd_ref)

    row = lax.broadcasted_iota(jnp.int32, (n, 128), 0)

    def shift(z_ref, prev_ref, sl, mu_lo):
        z = z_ref[:, sl]
        zp = jnp.where(row == 0, prev_ref[:, sl], pltpu.roll(z, 1, axis=0))
        prev_ref[:, sl] = z[n - 1:n, :]
        return z + (zp - z) * mu_ref[:, mu_lo:mu_lo + 128]

    wdad = shift(wdad_ref, pwd_ref, slice(0, 128), 3 * D_MODEL)
    lora_w, lora_a = _rw_lora(wdad, w2a2_ref[...])
    lw_ref[...] = lora_w
    la_ref[...] = lora_a
    ones_bd = ones_ref[...]
    pairs = [slice(q * 128, (q + 1) * 128) for q in range(RW_HEADS // 2)]
    prep = [_rw_prep(
        shift(r_ref, pr_ref, sl, q * 128), shift(k_ref, pk_ref, sl, D_MODEL + q * 128),
        shift(v_ref, pv_ref, sl, 2 * D_MODEL + q * 128), lw_ref[:, sl], la_ref[:, sl],
        w0_ref[:, sl], a0_ref[:, sl], kkw_ref[:, sl], kaw_ref[:, sl], ones_bd) for q, sl in enumerate(pairs)]
    rs, ks, vs, lws, avs, bvs = (list(x) for x in zip(*prep))
    o_all, sbd_new = _rw_chunks(rs, ks, vs, lws, avs, bvs, [sbd_ref[q] for q in range(len(pairs))], tril_ref[...])
    for q in range(len(pairs)):
        sbd_ref[q] = sbd_new[q]
    outs = _rw_posts(o_all, rs, ks, vs, [g_ref[:, sl] for sl in pairs], [rkw_ref[:, sl] for sl in pairs],
                     [lnw_ref[:, sl] for sl in pairs], [lnb_ref[:, sl] for sl in pairs], ones_bd)
    for sl, out in zip(pairs, outs):
        o_ref[:, sl] = out.astype(o_ref.dtype)

    @pl.when(c == pl.num_programs(1) - 1)
    def _():
        s_ref[0] = sbd_ref[...]


def _rw_consts():
    tril = jnp.asarray(np.tril(np.ones((RW_CHUNK, RW_CHUNK))), BF16)
    ones_bd = jnp.asarray(np.kron(np.eye(2), np.ones((RW_N, RW_N))), BF16)
    return tril, ones_bd


def _rwkv_prompt(proj, p, bsz, t):
    n = RW_CHUNK
    n_t = t // n
    tril, ones_bd = _rw_consts()
    col = lambda blk: (lambda b, c: (b * n_t + c, blk))
    const = lambda shape: pl.BlockSpec(shape, lambda b, c: (0, 0))
    vec = const((1, D_MODEL))
    prev = pltpu.VMEM((1, D_MODEL), F32)
    tile = pltpu.VMEM((n, D_MODEL), F32)
    o, sbd = pl.pallas_call(
        _rwkv_prompt_kernel,
        grid=(bsz, n_t),
        in_specs=[
            pl.BlockSpec((n, D_MODEL), col(C_R)), pl.BlockSpec((n, D_MODEL), col(C_K)),
            pl.BlockSpec((n, D_MODEL), col(C_V)), pl.BlockSpec((n, D_MODEL), col(C_GR)),
            pl.BlockSpec((n, 128), lambda b, c: (b * n_t + c, C_WDAD)),
            const((1, RW_SHIFT)),
            vec, vec, vec, vec, vec, vec, vec,
            const((128, D_MODEL)), const((n, n)), const((128, 128)),
        ],
        out_specs=[
            pl.BlockSpec((n, D_MODEL), lambda b, c: (b * n_t + c, 0)),
            pl.BlockSpec((1, RW_HEADS // 2, 128, 128), lambda b, c: (b, 0, 0, 0)),
        ],
        out_shape=[
            jax.ShapeDtypeStruct((bsz * t, D_MODEL), BF16),
            jax.ShapeDtypeStruct((bsz, RW_HEADS // 2, 128, 128), F32),
        ],
        scratch_shapes=[pltpu.VMEM((RW_HEADS // 2, 128, 128), F32), prev, prev, prev, pltpu.VMEM((1, 128), F32),
                        tile, tile],
        compiler_params=_params(("parallel", "arbitrary")),
        name="rwkv_prompt",
    )(proj, proj, proj, proj, proj, p["mu"], p["w0"], p["a0"], p["kk"], p["ka"], p["rk"], p["lnw"], p["lnb"],
      p["w2a2"], tril, ones_bd)
    s6 = sbd.reshape(bsz, RW_HEADS // 2, 2, RW_N, 2, RW_N)
    state = jnp.stack([s6[:, :, 0, :, 0, :], s6[:, :, 1, :, 1, :]], axis=2)
    return o, state.reshape(bsz, RW_HEADS, RW_N, RW_N)


def _rwkv_sample_kernel(r_ref, k_ref, v_ref, g_ref, wdad_ref, sh_ref, mu_ref, w0_ref, a0_ref, kkw_ref, kaw_ref,
                        rkw_ref, lnw_ref, lnb_ref, w2a2_ref, ones_ref, s_ref, o_ref, so_ref,
                        rs_ref, ks_ref, vs_ref, ob_ref, kr_ref, wr_ref, ar_ref, br_ref):
    ones_bd = ones_ref[...]
    tb = SAMPLE_TB

    def store_rows(ref, sl, x):
        for i in range(tb):
            ref[i, :, sl] = x[i:i + 1, :]

    def shift(z, lo, width):
        return z + (sh_ref[:, lo:lo + width] - z) * mu_ref[:, lo:lo + width]

    wdad = shift(wdad_ref[...], 3 * D_MODEL, 128)
    for q in range(RW_HEADS // 2):
        sl = slice(q * 128, (q + 1) * 128)
        lora_w, lora_a = _rw_lora(wdad, w2a2_ref[:, sl])
        r, k, v, lw, av, bv = _rw_prep(
            shift(r_ref[:, sl], q * 128, 128), shift(k_ref[:, sl], D_MODEL + q * 128, 128),
            shift(v_ref[:, sl], 2 * D_MODEL + q * 128, 128), lora_w, lora_a, w0_ref[:, sl], a0_ref[:, sl],
            kkw_ref[:, sl], kaw_ref[:, sl], ones_bd)
        rs_ref[:, sl] = r
        ks_ref[:, sl] = k
        vs_ref[:, sl] = v
        store_rows(kr_ref, sl, k)
        store_rows(wr_ref, sl, jnp.exp(lw))
        store_rows(ar_ref, sl, av)
        store_rows(br_ref, sl, bv)
    ob_ref[...] = jnp.zeros_like(ob_ref)
    rows = lax.broadcasted_iota(jnp.int32, (tb, 1), 0)

    def per_seq(b, carry):
        sel = rows == b
        for q in range(RW_HEADS // 2):
            pair = slice(q * 128, (q + 1) * 128)
            a_row, w_row = ar_ref[b, :, pair], wr_ref[b, :, pair]
            b_row, k_row = br_ref[b, :, pair], kr_ref[b, :, pair]
            for e in range(2):
                h = 2 * q + e
                sl = slice(h * RW_N, (h + 1) * RW_N)
                half = slice(e * RW_N, (e + 1) * RW_N)
                s = s_ref[b, h]
                sa = jnp.sum(s * a_row[:, half], axis=-1, keepdims=True)
                vcol = _col_bcast(jnp.where(sel, vs_ref[:, sl], 0.0), RW_N)
                s_new = s * w_row[:, half] + sa * b_row[:, half] + vcol * k_row[:, half]
                so_ref[b, h] = s_new
                ob_ref[:, sl] += _mm(jnp.where(sel, rs_ref[:, sl], 0.0), s_new, NT)
        return carry

    lax.fori_loop(0, tb, per_seq, 0)
    for q in range(RW_HEADS // 2):
        sl = slice(q * 128, (q + 1) * 128)
        o_ref[:, sl] = _rw_post(ob_ref[:, sl], rs_ref[:, sl], ks_ref[:, sl], vs_ref[:, sl], g_ref[:, sl],
                                rkw_ref[:, sl], lnw_ref[:, sl], lnb_ref[:, sl], ones_bd)


def _rwkv_sample(proj, p, shift_state, state):
    bsz = state.shape[0]
    tb = SAMPLE_TB
    _, ones_bd = _rw_consts()
    col = lambda blk: (lambda i: (i, blk))
    vec = pl.BlockSpec((1, D_MODEL), lambda i: (0, 0))
    tile = pltpu.VMEM((tb, D_MODEL), F32)
    sspec = pl.BlockSpec((tb, RW_HEADS, RW_N, RW_N), lambda i: (i, 0, 0, 0))
    return pl.pallas_call(
        _rwkv_sample_kernel,
        grid=(bsz // tb,),
        in_specs=[
            pl.BlockSpec((tb, D_MODEL), col(C_R)), pl.BlockSpec((tb, D_MODEL), col(C_K)),
            pl.BlockSpec((tb, D_MODEL), col(C_V)), pl.BlockSpec((tb, D_MODEL), col(C_GR)),
            pl.BlockSpec((tb, 128), lambda i: (i, C_WDAD)),
            pl.BlockSpec((tb, RW_SHIFT), lambda i: (i, 0)),
            pl.BlockSpec((1, RW_SHIFT), lambda i: (0, 0)),
            vec, vec, vec, vec, vec, vec, vec,
            pl.BlockSpec((128, D_MODEL), lambda i: (0, 0)),
            pl.BlockSpec((128, 128), lambda i: (0, 0)),
            sspec,
        ],
        out_specs=[pl.BlockSpec((tb, D_MODEL), lambda i: (i, 0)), sspec],
        out_shape=[
            jax.ShapeDtypeStruct((bsz, D_MODEL), F32),
            jax.ShapeDtypeStruct(state.shape, F32),
        ],
        scratch_shapes=[tile] * 4 + [pltpu.VMEM((tb, 1, D_MODEL), F32)] * 4,
        compiler_params=_params(("parallel",)),
        name="rwkv_sample",
    )(proj, proj, proj, proj, proj, shift_state, p["mu"], p["w0"], p["a0"], p["kk"], p["ka"], p["rk"],
      p["lnw"], p["lnb"], p["w2a2"], ones_bd, state)


def _mb_post(y, xs, z, d_rep, nw):
    y = (y + d_rep * xs) * _silu(z)
    ms = jnp.mean(y * y, axis=-1, keepdims=True)
    return y * lax.rsqrt(ms + RMS_EPS) * nw


def _ssd_prompt_kernel(x_ref, bc_ref, z_ref, dt_ref, cw_ref, cb_ref, dtb_ref, alog_ref, d_ref, nw_ref, tril_ref,
                       o_ref, s_ref, h_ref, px_ref, pbc_ref):
    c = pl.program_id(1)
    n = MB_CHUNK

    @pl.when(c == 0)
    def _():
        h_ref[...] = jnp.zeros_like(h_ref)
        px_ref[...] = jnp.zeros_like(px_ref)
        pbc_ref[...] = jnp.zeros_like(pbc_ref)

    def conv(u_ref, prev_ref, sl, wsl):
        u = u_ref[:, sl]
        prev = prev_ref[:, sl]
        w = cw_ref[:, wsl]
        row = lax.broadcasted_iota(jnp.int32, (8, u.shape[1]), 0)
        acc = u * w[MB_CONV - 1:MB_CONV, :] + cb_ref[:, wsl]
        for s in range(1, MB_CONV):
            us = pltpu.roll(u, s, axis=0)
            top = jnp.where(row < s, pltpu.roll(prev, s, axis=0), us[0:8])
            us = jnp.concatenate([top, us[8:]], axis=0)
            acc = acc + us * w[MB_CONV - 1 - s:MB_CONV - s, :]
        prev_ref[:, sl] = u[n - 8:n, :]
        return _silu(acc)

    tril = tril_ref[...]
    ti = lax.broadcasted_iota(jnp.int32, (n, n), 0)
    si = lax.broadcasted_iota(jnp.int32, (n, n), 1)
    incl = si <= ti
    strict_f = (si < ti).astype(F32)
    lane = lax.broadcasted_iota(jnp.int32, (n, 4 * MB_P), 1)
    ones_n = jnp.ones((n, MB_N), BF16)
    nbc = MB_GROUPS * MB_N

    for g in range(MB_GROUPS):
        gx = slice(g * 256, (g + 1) * 256)
        gb = slice(g * MB_N, (g + 1) * MB_N)
        gc = slice(nbc + g * MB_N, nbc + (g + 1) * MB_N)
        xs = conv(x_ref, px_ref, gx, gx)
        bm = conv(bc_ref, pbc_ref, gb, slice(D_MODEL + gb.start, D_MODEL + gb.stop))
        cm = conv(bc_ref, pbc_ref, gc, slice(D_MODEL + gc.start, D_MODEL + gc.stop))

        dt = _softplus(dt_ref[:, gx] + dtb_ref[:, gx])
        a = dt * (-jnp.exp(alog_ref[:, gx]))
        xdt = xs * dt
        cs = _mm_exact(tril, a)
        cs_end = cs[n - 1:n, :]

        gmat = _mm(cm, bm, NT)
        h_all = h_ref[g]
        y = jnp.exp(cs) * _mm(cm, h_all, NT)
        for e in range(4):
            a_e = a[:, e * MB_P:(e + 1) * MB_P]
            dmat = _mm_exact(tril, a_e * strict_f)
            lmat = jnp.where(incl, jnp.exp(dmat), 0.0)
            me = (lane >= e * MB_P) & (lane < (e + 1) * MB_P)
            y = y + _mm(gmat * lmat, jnp.where(me, xdt, 0.0))

        rowscale = jnp.exp(_mm_exact(ones_n, a, TN, x_is_lhs=True))
        h_new = h_all * rowscale + _mm(xdt * jnp.exp(cs_end - cs), bm, TN)
        h_ref[g] = h_new
        o_ref[:, gx] = _mb_post(y, xs, z_ref[:, gx], d_ref[:, gx], nw_ref[:, gx]).astype(o_ref.dtype)

    @pl.when(c == pl.num_programs(1) - 1)
    def _():
        for g in range(MB_GROUPS):
            for e in range(4):
                s_ref[0, 4 * g + e] = h_ref[g, e * MB_P:(e + 1) * MB_P, :]


def _ssd_prompt(proj, p, bsz, t):
    n = MB_CHUNK
    n_t = t // n
    tril = jnp.asarray(np.tril(np.ones((n, n))), BF16)
    col = lambda blk: (lambda b, c: (b * n_t + c, blk))
    const = lambda shape: pl.BlockSpec(shape, lambda b, c: (0, 0))
    vec = const((1, D_MODEL))
    return pl.pallas_call(
        _ssd_prompt_kernel,
        grid=(bsz, n_t),
        in_specs=[
            pl.BlockSpec((n, D_MODEL), col(C_X)), pl.BlockSpec((n, D_MODEL), col(C_BC)),
            pl.BlockSpec((n, D_MODEL), col(C_Z)), pl.BlockSpec((n, D_MODEL), col(C_DT)),
            const((MB_CONV, MB_XBC)), const((1, MB_XBC)),
            vec, vec, vec, vec,
            const((n, n)),
        ],
        out_specs=[
            pl.BlockSpec((n, D_MODEL), lambda b, c: (b * n_t + c, 0)),
            pl.BlockSpec((1, MB_HEADS, MB_P, MB_N), lambda b, c: (b, 0, 0, 0)),
        ],
        out_shape=[
            jax.ShapeDtypeStruct((bsz * t, D_MODEL), BF16),
            jax.ShapeDtypeStruct((bsz, MB_HEADS, MB_P, MB_N), F32),
        ],
        scratch_shapes=[pltpu.VMEM((MB_GROUPS, 4 * MB_P, MB_N), F32), pltpu.VMEM((8, D_MODEL), F32),
                        pltpu.VMEM((8, D_MODEL), F32)],
        compiler_params=_params(("parallel", "arbitrary")),
        name="ssd_prompt",
    )(proj, proj, proj, proj, p["cw"], p["cb"], p["dtb"], p["alog"], p["d"], p["nw"], tril)


def _ssd_sample_kernel(xbc_ref, z_ref, dt_ref, cs_ref, cw_ref, cb_ref, dtb_ref, alog_ref, d_ref, nw_ref, s_ref,
                       o_ref, so_ref, xs_ref, xd_ref, bs_ref, cms_ref, dec_ref, ob_ref):
    tb = SAMPLE_TB
    w = cw_ref[...]
    acc = xbc_ref[...] * w[MB_CONV - 1:MB_CONV, :] + cb_ref[...]
    for i in range(MB_CONV - 1):
        acc = acc + cs_ref[i] * w[i:i + 1, :]
    xbc = _silu(acc)
    xs = xbc[:, :D_MODEL]
    dt = _softplus(dt_ref[...] + dtb_ref[...])
    xs_ref[...] = xs
    xd_ref[...] = xs * dt
    bs_ref[...] = xbc[:, D_MODEL:D_MODEL + MB_GROUPS * MB_N]
    cms_ref[...] = xbc[:, D_MODEL + MB_GROUPS * MB_N:]
    dec = jnp.exp(dt * (-jnp.exp(alog_ref[...])))
    for i in range(tb):
        dec_ref[i] = dec[i:i + 1, :]
    ob_ref[...] = jnp.zeros_like(ob_ref)
    rows = lax.broadcasted_iota(jnp.int32, (tb, 1), 0)

    def per_seq(b, carry):
        sel = rows == b
        for h in range(MB_HEADS):
            g = h // (MB_HEADS // MB_GROUPS)
            sl = slice(h * MB_P, (h + 1) * MB_P)
            gl = slice(g * MB_N, (g + 1) * MB_N)
            pair = slice((h // 2) * 128, (h // 2 + 1) * 128)
            dec = dec_ref[b, :, pair][:, (h % 2) * MB_P:(h % 2) * MB_P + 1]
            upd = _mm(jnp.where(sel, xd_ref[:, sl], 0.0), bs_ref[:, gl], TN)
            h_new = dec * s_ref[b, h] + upd
            so_ref[b, h] = h_new
            ob_ref[:, sl] += _mm(jnp.where(sel, cms_ref[:, gl], 0.0), h_new, NT)
        return carry

    lax.fori_loop(0, tb, per_seq, 0)
    for g in range(MB_GROUPS):
        sl = slice(g * 256, (g + 1) * 256)
        o_ref[:, sl] = _mb_post(ob_ref[:, sl], xs_ref[:, sl], z_ref[:, sl], d_ref[:, sl], nw_ref[:, sl])


def _ssd_sample(proj, p, conv_state, state):
    bsz = state.shape[0]
    tb = SAMPLE_TB
    vec = lambda w: pl.BlockSpec((1, w), lambda i: (0, 0))
    tile = pltpu.VMEM((tb, D_MODEL), F32)
    half = pltpu.VMEM((tb, MB_GROUPS * MB_N), F32)
    sspec = pl.BlockSpec((tb, MB_HEADS, MB_P, MB_N), lambda i: (i, 0, 0, 0))
    return pl.pallas_call(
        _ssd_sample_kernel,
        grid=(bsz // tb,),
        in_specs=[
            pl.BlockSpec((tb, MB_XBC), lambda i: (i, C_X // 2)),
            pl.BlockSpec((tb, D_MODEL), lambda i: (i, C_Z)),
            pl.BlockSpec((tb, D_MODEL), lambda i: (i, C_DT)),
            pl.BlockSpec((MB_CONV - 1, tb, MB_XBC), lambda i: (0, i, 0)),
            pl.BlockSpec((MB_CONV, MB_XBC), lambda i: (0, 0)),
            vec(MB_XBC), vec(D_MODEL), vec(D_MODEL), vec(D_MODEL), vec(D_MODEL),
            sspec,
        ],
        out_specs=[pl.BlockSpec((tb, D_MODEL), lambda i: (i, 0)), sspec],
        out_shape=[
            jax.ShapeDtypeStruct((bsz, D_MODEL), F32),
            jax.ShapeDtypeStruct(state.shape, F32),
        ],
        scratch_shapes=[tile, tile, half, half, pltpu.VMEM((tb, 1, D_MODEL), F32), tile],
        compiler_params=_params(("parallel",)),
        name="ssd_sample",
    )(proj, proj, proj, conv_state, p["cw"], p["cb"], p["dtb"], p["alog"], p["d"], p["nw"], state)


def _prep_w_in(w_in):
    o = np.cumsum([0, 1024, 1024, 1024, 1024, RW_SHIFT, 1024, 1024, MB_XBC, MB_HEADS, 3 * D_MODEL])
    zrw, grw, zmb, xbc, dtc, gate = o[4], o[5], o[6], o[7], o[8], o[9]
    parts = [
        w_in[..., :zrw], w_in[..., zrw:zrw + 3 * D_MODEL], w_in[..., grw:zmb], w_in[..., xbc:dtc],
        w_in[..., zmb:xbc], w_in[..., gate:],
        jnp.repeat(w_in[..., dtc:gate], MB_P, axis=-1),
        w_in[..., zrw + 3 * D_MODEL:grw],
    ]
    return jnp.concatenate(parts, axis=-1).astype(BF16)


def _rep(v):
    return jnp.repeat(v, MB_P, axis=-1)[:, None, :]


def kernel(x_prompt, x_sample, state_hgrn, state_rwkv, state_rwkv_shift, state_ssm, state_conv, norm_w, w_in, hg_lb, hg_norm_w, rw_mu, rw_w0, rw_w2, rw_a0, rw_a2, rw_k_k, rw_k_a, rw_r_k, rw_ln_w, rw_ln_b, mb_conv_w, mb_conv_b, mb_dt_bias, mb_A_log, mb_D, mb_norm_w, w_o_hg, w_o_rw, w_o_mb, w_out, final_norm_w):
    bp, t, _ = x_prompt.shape
    bs = x_sample.shape[0]
    sm = jax.nn.softmax(hg_lb.astype(F32), axis=0)
    lbs = jnp.cumsum(sm, axis=0) - sm[0:1]
    w_in_r = _prep_w_in(w_in)
    row = lambda a: a[:, None, :]
    dtb, alog, drep = _rep(mb_dt_bias), _rep(mb_A_log), _rep(mb_D)
    w2a2 = jnp.concatenate([rw_w2, rw_a2], axis=1)
    whg, wrw, wmb, wout = (w.astype(BF16) for w in (w_o_hg, w_o_rw, w_o_mb, w_out))
    fw = final_norm_w[None, :]

    hp = x_prompt.reshape(bp * t, D_MODEL)
    hs = x_sample.reshape(bs, D_MODEL)
    p_states, s_states = [], []
    yp = ys = None
    for l in range(DEPTH):
        rw_p = dict(mu=row(rw_mu)[l], w0=row(rw_w0)[l], a0=row(rw_a0)[l], kk=row(rw_k_k)[l], ka=row(rw_k_a)[l],
                    rk=rw_r_k[l].reshape(1, D_MODEL), lnw=row(rw_ln_w)[l], lnb=row(rw_ln_b)[l], w2a2=w2a2[l])
        mb_p = dict(cw=mb_conv_w[l], cb=row(mb_conv_b)[l], dtb=dtb[l], alog=alog[l], d=drep[l],
                    nw=row(mb_norm_w)[l])
        lb = row(lbs)[l]
        nw_hg = hg_norm_w[l][None, :]
        final = l == DEPTH - 1

        proj = _inproj(hp, row(norm_w)[l], w_in_r[l])
        o_hg, p_hg = _hgrn_prompt(proj, lb, nw_hg, bp, t)
        o_rw, p_rw = _rwkv_prompt(proj, rw_p, bp, t)
        o_mb, p_ssm = _ssd_prompt(proj, mb_p, bp, t)
        proj3 = proj.reshape(bp, t, NCOL)
        p_shift = jnp.concatenate([proj3[:, -1, C_R * 1024:C_GR * 1024], proj3[:, -1, C_WDAD * 128:]], axis=-1)
        p_conv = proj3[:, t - (MB_CONV - 1):, C_X * 1024:C_Z * 1024]
        res = _merge(o_hg, o_rw, o_mb, proj, hp, whg[l], wrw[l], wmb[l], wout[l], fw, final)
        hp = res[0]
        if final:
            yp = res[1]
        p_states.append((p_hg, p_rw, p_shift, p_ssm, p_conv))

        proj = _inproj(hs, row(norm_w)[l], w_in_r[l])
        o_hg, s_hg = _hgrn_sample(proj, lb, nw_hg, state_hgrn[l])
        o_rw, s_rw = _rwkv_sample(proj, rw_p, state_rwkv_shift[l], state_rwkv[l])
        o_mb, s_ssm = _ssd_sample(proj, mb_p, jnp.swapaxes(state_conv[l], 0, 1), state_ssm[l])
        s_shift = jnp.concatenate([proj[:, C_R * 1024:C_GR * 1024], proj[:, C_WDAD * 128:]], axis=-1)
        s_conv = jnp.concatenate([state_conv[l][:, 1:], proj[:, None, C_X * 1024:C_Z * 1024]], axis=1)
        res = _merge(o_hg, o_rw, o_mb, proj, hs, whg[l], wrw[l], wmb[l], wout[l], fw, final)
        hs = res[0]
        if final:
            ys = res[1]
        s_states.append((s_hg, s_rw, s_shift, s_ssm, s_conv))

    stack = lambda states, i: jnp.stack([s[i] for s in states])
    return (yp.reshape(bp, t, D_MODEL), ys.reshape(bs, 1, D_MODEL),
            *[stack(p_states, i) for i in range(5)], *[stack(s_states, i) for i in range(5)])
```

```python
import functools

import numpy as np
import jax
import jax.numpy as jnp
from jax import lax
from jax.experimental import pallas as pl
from jax.experimental.pallas import tpu as pltpu

F32 = jnp.float32
BF16 = jnp.bfloat16
HI = lax.Precision.HIGHEST

D_MODEL = 1024
DEPTH = 2
HG_HEADS, HG_DK, HG_DV = 8, 128, 128
HG_F_MIN = 1e-20
RW_HEADS, RW_N = 16, 64
RW_LORA = 64
RW_SHIFT = 3 * D_MODEL + 2 * RW_LORA
RW_LN_EPS = 64e-5
MB_HEADS, MB_P, MB_GROUPS, MB_N, MB_CONV = 16, 64, 4, 128, 4
MB_XBC = D_MODEL + 2 * MB_GROUPS * MB_N
RMS_EPS = 1e-6

C_Q, C_F, C_I, C_GH, C_R, C_K, C_V, C_GR, C_X, C_BC, C_Z, C_G0, C_G1, C_G2, C_DT = range(15)
C_WDAD = 15 * 8
NCOL = 15 * 1024 + 128

HG_BLK = 16
RW_CHUNK = 64
MB_CHUNK = 64
SAMPLE_TB = 8

VMEM_LIMIT = 48 * 1024 * 1024

NN = (((1,), (0,)), ((), ()))
NT = (((1,), (1,)), ((), ()))
TN = (((0,), (0,)), ((), ()))


def _mm(a, b, dims=NN, hi=False):
    if hi:
        return lax.dot_general(a.astype(F32), b.astype(F32), dims, precision=HI, preferred_element_type=F32)
    return lax.dot_general(a.astype(BF16), b.astype(BF16), dims, preferred_element_type=F32)


def _mm_exact(e, x, dims=NN, x_is_lhs=False, passes=3):
    e = e.astype(BF16)
    acc = None
    for _ in range(passes):
        p = x.astype(BF16)
        x = x - p.astype(F32)
        t = (lax.dot_general(p, e, dims, preferred_element_type=F32) if x_is_lhs
             else lax.dot_general(e, p, dims, preferred_element_type=F32))
        acc = t if acc is None else acc + t
    return acc


def _mm3(a, b, dims=NN):
    a_hi = a.astype(BF16)
    a_lo = (a - a_hi.astype(F32)).astype(BF16)
    b_hi = b.astype(BF16)
    b_lo = (b - b_hi.astype(F32)).astype(BF16)
    dot = lambda x, y: lax.dot_general(x, y, dims, preferred_element_type=F32)
    return dot(a_hi, b_hi) + (dot(a_hi, b_lo) + dot(a_lo, b_hi))


def _seg_many(ones_bd, xs):
    m = xs[0].shape[0]
    out = _mm_exact(ones_bd, jnp.concatenate(xs, axis=0) if len(xs) > 1 else xs[0], x_is_lhs=True)
    return [out[i * m:(i + 1) * m] for i in range(len(xs))]


def _sigmoid(x):
    return jax.nn.sigmoid(x)


def _silu(x):
    return x * jax.nn.sigmoid(x)


def _softplus(x):
    return jnp.maximum(x, 0.0) + jnp.log1p(jnp.exp(-jnp.abs(x)))


def _col_bcast(rows, width):
    ones = jnp.ones((rows.shape[0], width), BF16)
    p1 = rows.astype(BF16)
    r1 = rows - p1.astype(F32)
    p2 = r1.astype(BF16)
    p3 = (r1 - p2.astype(F32)).astype(BF16)
    out = lax.dot_general(p1, ones, TN, preferred_element_type=F32)
    out = out + lax.dot_general(p2, ones, TN, preferred_element_type=F32)
    return out + lax.dot_general(p3, ones, TN, preferred_element_type=F32)


def _params(sem):
    return pltpu.CompilerParams(dimension_semantics=sem, vmem_limit_bytes=VMEM_LIMIT)


def _inproj_kernel(x_ref, nw_ref, w_ref, o_ref, xn_ref):
    @pl.when(pl.program_id(1) == 0)
    def _():
        x = x_ref[...]
        ms = jnp.mean(x * x, axis=-1, keepdims=True)
        xn_ref[...] = (x * lax.rsqrt(ms + RMS_EPS) * nw_ref[...]).astype(BF16)

    o_ref[...] = jnp.dot(xn_ref[...], w_ref[...], preferred_element_type=F32)


def _inproj(x2d, nw, w, l):
    m = x2d.shape[0]
    tm = min(1024, m)
    tn = 1408
    return pl.pallas_call(
        _inproj_kernel,
        grid=(m // tm, NCOL // tn),
        in_specs=[
            pl.BlockSpec((tm, D_MODEL), lambda i, j: (i, 0)),
            pl.BlockSpec((None, 1, D_MODEL), lambda i, j: (l, 0, 0)),
            pl.BlockSpec((None, D_MODEL, tn), lambda i, j: (l, 0, j)),
        ],
        out_specs=pl.BlockSpec((tm, tn), lambda i, j: (i, j)),
        out_shape=jax.ShapeDtypeStruct((m, NCOL), F32),
        scratch_shapes=[pltpu.VMEM((tm, D_MODEL), BF16)],
        compiler_params=_params(("parallel", "arbitrary")),
        name="inproj",
    )(x2d, nw, w)


def _merge_kernel(ohg, orw, omb, g0, g1, g2, h_ref, whg, wrw, wmb, wout, fw_ref, hn_ref, *y_ref):
    u = _sigmoid(g0[...]) * _mm(ohg[...], whg[...])
    u = u + _sigmoid(g1[...]) * _mm(orw[...], wrw[...])
    u = u + _sigmoid(g2[...]) * _mm(omb[...], wmb[...])
    hn = h_ref[...] + _mm(u, wout[...])
    hn_ref[...] = hn
    if y_ref:
        ms = jnp.mean(hn * hn, axis=-1, keepdims=True)
        y_ref[0][...] = hn * lax.rsqrt(ms + RMS_EPS) * fw_ref[...]


def _merge(ohg, orw, omb, proj, h2d, whg, wrw, wmb, wout, fw, final, l):
    m = h2d.shape[0]
    tm = min(256, m)
    row = lambda i: (i, 0)
    const = lambda i: (0, 0)
    wspec = pl.BlockSpec((None, D_MODEL, D_MODEL), lambda i: (l, 0, 0))
    out_shape = [jax.ShapeDtypeStruct((m, D_MODEL), F32)]
    out_specs = [pl.BlockSpec((tm, D_MODEL), row)]
    if final:
        out_shape.append(jax.ShapeDtypeStruct((m, D_MODEL), F32))
        out_specs.append(pl.BlockSpec((tm, D_MODEL), row))
    return pl.pallas_call(
        _merge_kernel,
        grid=(m // tm,),
        in_specs=[
            pl.BlockSpec((tm, D_MODEL), row), pl.BlockSpec((tm, D_MODEL), row), pl.BlockSpec((tm, D_MODEL), row),
            pl.BlockSpec((tm, D_MODEL), lambda i: (i, C_G0)),
            pl.BlockSpec((tm, D_MODEL), lambda i: (i, C_G1)),
            pl.BlockSpec((tm, D_MODEL), lambda i: (i, C_G2)),
            pl.BlockSpec((tm, D_MODEL), row),
            wspec, wspec, wspec, wspec,
            pl.BlockSpec((1, D_MODEL), const),
        ],
        out_specs=out_specs,
        out_shape=out_shape,
        compiler_params=_params(("parallel",)),
        name="merge",
    )(ohg, orw, omb, proj, proj, proj, h2d, whg, wrw, wmb, wout, fw)


def _sample_call(body, name, bsz, in_specs, args, state_spec, state_shape, scratch, new_state):
    tb = SAMPLE_TB
    aliases = {}
    if new_state is not None:
        in_specs = in_specs + [pl.BlockSpec(memory_space=pl.ANY)]
        args = args + (new_state,)
        aliases = {len(args) - 1: 1}
        inner = body
        body = lambda *refs: inner(*refs[:len(args) - 1], *refs[len(args):])
    return pl.pallas_call(
        body,
        grid=(bsz // tb,),
        in_specs=in_specs,
        out_specs=[pl.BlockSpec((tb, D_MODEL), lambda i: (i, 0)), state_spec],
        out_shape=[jax.ShapeDtypeStruct((bsz, D_MODEL), F32), jax.ShapeDtypeStruct(state_shape, F32)],
        scratch_shapes=scratch,
        input_output_aliases=aliases,
        compiler_params=_params(("parallel",)),
        name=name,
    )(*args)


def _hgrn_prep(q, f, lb):
    fg = lb + (1.0 - lb) * _sigmoid(f)
    logf = jnp.log(jnp.maximum(fg, HG_F_MIN))
    kx = (1.0 - lb) * _sigmoid(-f)
    qx = _silu(q) * (HG_DK ** -0.5)
    return qx, kx, logf


def _hgrn_post(o, g, nw):
    ms = jnp.mean(o * o, axis=-1, keepdims=True)
    return o * lax.rsqrt(ms + RMS_EPS) * nw * _silu(g)


def _hgrn_prompt_kernel(q_ref, f_ref, i_ref, g_ref, lb_ref, nw_ref, tril_ref, o_ref, s_ref,
                        st_ref, qs_ref, ks_ref, bs_ref, ob_ref, *, nblk):
    c = pl.program_id(1)

    @pl.when(c == 0)
    def _():
        st_ref[...] = jnp.zeros_like(st_ref)

    tril = tril_ref[...]
    heads = [slice(h * 128, (h + 1) * 128) for h in range(HG_HEADS)]
    for sl in heads:
        qx, kx, logf = _hgrn_prep(q_ref[:, sl], f_ref[:, sl], lb_ref[:, sl])
        qs_ref[:, sl] = qx
        ks_ref[:, sl] = kx
        bs_ref[:, sl] = _mm_exact(tril, logf)
    half = HG_BLK // 2
    lane8 = lax.broadcasted_iota(jnp.int32, (half, HG_DK), 1)
    lower = (lax.broadcasted_iota(jnp.int32, (HG_BLK, HG_DK), 1)
             <= lax.broadcasted_iota(jnp.int32, (HG_BLK, HG_DK), 0))

    hs = range(HG_HEADS)

    def block_rows(j):
        return pl.ds(pl.multiple_of(j * HG_BLK, HG_BLK), HG_BLK)

    def intra_weights(j):
        rows = block_rows(j)
        amat = []
        for sl in heads:
            qb, kb, bb = qs_ref[rows, sl], ks_ref[rows, sl], bs_ref[rows, sl]
            a_top = jnp.zeros((half, HG_DK), F32)
            a_bot = jnp.zeros((half, HG_DK), F32)
            for jj in range(HG_BLK):
                bj, kj = bb[jj:jj + 1, :], kb[jj:jj + 1, :]
                if jj < half:
                    s = jnp.sum(qb[:half] * jnp.exp(bb[:half] - bj) * kj, axis=-1, keepdims=True)
                    a_top = jnp.where(lane8 == jj, s, a_top)
                s = jnp.sum(qb[half:] * jnp.exp(bb[half:] - bj) * kj, axis=-1, keepdims=True)
                a_bot = jnp.where(lane8 == jj, s, a_bot)
            amat.append(jnp.where(lower, jnp.concatenate([a_top, a_bot], axis=0), 0.0))
        return tuple(amat)

    def apply_block(j, amat):
        rows = block_rows(j)
        qb = [qs_ref[rows, sl] for sl in heads]
        kb = [ks_ref[rows, sl] for sl in heads]
        bb = [bs_ref[rows, sl] for sl in heads]
        vb = [i_ref[rows, sl] for sl in heads]
        st = [st_ref[h] for h in hs]
        intra = [_mm(amat[h][:, :HG_BLK], vb[h]) for h in hs]
        inter = [_mm(qb[h] * jnp.exp(bb[h]), st[h], NT) for h in hs]
        for h in hs:
            b_end = bb[h][HG_BLK - 1:HG_BLK, :]
            st_ref[h] = st[h] * jnp.exp(b_end) + _mm(vb[h], kb[h] * jnp.exp(b_end - bb[h]), TN)
        for h, sl in enumerate(heads):
            ob_ref[rows, sl] = intra[h] + inter[h]

    def step(j, amat_prev):
        apply_block(j - 1, amat_prev)
        return intra_weights(j)

    amat_last = lax.fori_loop(1, nblk, step, intra_weights(0))
    apply_block(nblk - 1, amat_last)
    nw = nw_ref[...]
    for sl in heads:
        o_ref[:, sl] = _hgrn_post(ob_ref[:, sl], g_ref[:, sl], nw).astype(o_ref.dtype)

    @pl.when(c == pl.num_programs(1) - 1)
    def _():
        for h in range(HG_HEADS):
            s_ref[0, h] = st_ref[h].T


def _hgrn_prompt(proj, lb, nw, bsz, t, l):
    tt = min(256, t)
    n_t = t // tt
    tril = jnp.asarray(np.kron(np.eye(tt // HG_BLK), np.tril(np.ones((HG_BLK, HG_BLK)))), BF16)
    col = lambda blk: (lambda b, c: (b * n_t + c, blk))
    const = lambda shape: pl.BlockSpec(shape, lambda b, c: (0, 0))
    layer = lambda shape: pl.BlockSpec((None,) + shape, lambda b, c: (l, 0, 0))
    tile = pltpu.VMEM((tt, D_MODEL), F32)
    return pl.pallas_call(
        functools.partial(_hgrn_prompt_kernel, nblk=tt // HG_BLK),
        grid=(bsz, n_t),
        in_specs=[
            pl.BlockSpec((tt, D_MODEL), col(C_Q)), pl.BlockSpec((tt, D_MODEL), col(C_F)),
            pl.BlockSpec((tt, D_MODEL), col(C_I)), pl.BlockSpec((tt, D_MODEL), col(C_GH)),
            layer((1, D_MODEL)), layer((1, 128)), const((tt, tt)),
        ],
        out_specs=[
            pl.BlockSpec((tt, D_MODEL), lambda b, c: (b * n_t + c, 0)),
            pl.BlockSpec((1, HG_HEADS, HG_DK, HG_DV), lambda b, c: (b, 0, 0, 0)),
        ],
        out_shape=[
            jax.ShapeDtypeStruct((bsz * t, D_MODEL), BF16),
            jax.ShapeDtypeStruct((bsz, HG_HEADS, HG_DK, HG_DV), F32),
        ],
        scratch_shapes=[pltpu.VMEM((HG_HEADS, HG_DV, HG_DK), F32), tile, tile, tile, tile],
        compiler_params=_params(("parallel", "arbitrary")),
        name="hgrn_prompt",
    )(proj, proj, proj, proj, lb, nw, tril)


def _hgrn_sample_kernel(q_ref, f_ref, i_ref, g_ref, lb_ref, nw_ref, s_ref, o_ref, so_ref,
                        qs_ref, ks_ref, fs_ref, ob_ref):
    qx, kx, logf = _hgrn_prep(q_ref[...], f_ref[...], lb_ref[...])
    qs_ref[...] = qx
    ks_ref[...] = kx
    fs_ref[...] = jnp.exp(logf)
    ob_ref[...] = jnp.zeros_like(ob_ref)
    rows = lax.broadcasted_iota(jnp.int32, (SAMPLE_TB, 1), 0)

    heads = [slice(h * 128, (h + 1) * 128) for h in range(HG_HEADS)]

    def per_seq(b, carry):
        sel = rows == b
        fcol = [_col_bcast(jnp.where(sel, fs_ref[:, sl], 0.0), HG_DV) for sl in heads]
        kv = [_mm(jnp.where(sel, ks_ref[:, sl], 0.0), i_ref[:, sl], TN) for sl in heads]
        s_new = [fcol[h] * s_ref[b, h] + kv[h] for h in range(HG_HEADS)]
        for h in range(HG_HEADS):
            so_ref[b, h] = s_new[h]
        outs = [_mm(jnp.where(sel, qs_ref[:, sl], 0.0), s_new[h]) for h, sl in enumerate(heads)]
        for sl, o in zip(heads, outs):
            ob_ref[:, sl] += o
        return carry

    lax.fori_loop(0, SAMPLE_TB, per_seq, 0)
    nw = nw_ref[...]
    for h in range(HG_HEADS):
        sl = slice(h * 128, (h + 1) * 128)
        o_ref[:, sl] = _hgrn_post(ob_ref[:, sl], g_ref[:, sl], nw)


def _hgrn_sample(proj, lb, nw, state, l, new_state):
    bsz = state.shape[1]
    tb = SAMPLE_TB
    col = lambda blk: (lambda i: (i, blk))
    tile = pltpu.VMEM((tb, D_MODEL), F32)
    sspec = pl.BlockSpec((None, tb, HG_HEADS, HG_DK, HG_DV), lambda i: (l, i, 0, 0, 0))
    return _sample_call(
        _hgrn_sample_kernel, "hgrn_sample", bsz,
        [
            pl.BlockSpec((tb, D_MODEL), col(C_Q)), pl.BlockSpec((tb, D_MODEL), col(C_F)),
            pl.BlockSpec((tb, D_MODEL), col(C_I)), pl.BlockSpec((tb, D_MODEL), col(C_GH)),
            pl.BlockSpec((None, 1, D_MODEL), lambda i: (l, 0, 0)),
            pl.BlockSpec((None, 1, 128), lambda i: (l, 0, 0)),
            sspec,
        ],
        (proj, proj, proj, proj, lb, nw, state), sspec, state.shape, [tile, tile, tile, tile], new_state)


def _rw_lora(wdad, w2a2):
    lo = lax.broadcasted_iota(jnp.int32, wdad.shape, 1) < RW_LORA
    lora_w = _mm3(jnp.where(lo, jnp.tanh(wdad), 0.0), w2a2)
    lora_a = _mm3(jnp.where(lo, 0.0, wdad), w2a2)
    return lora_w, lora_a


def _rw_prep_all(items, ones_bd):
    rs, ks, vs, lws, a_sigs, kks = [], [], [], [], [], []
    for r, k0, v, lora_w, lora_a, w0, a0, kk_w, ka_w in items:
        w_log = -_softplus(-(w0 + lora_w)) - 0.5
        lws.append(-jnp.exp(w_log))
        a_sig = _sigmoid(a0 + lora_a)
        a_sigs.append(a_sig)
        kks.append(k0 * kk_w)
        rs.append(r)
        vs.append(v)
        ks.append(k0 * (1.0 + (a_sig - 1.0) * ka_w))
    sq = _seg_many(ones_bd, [kk * kk for kk in kks])
    kks = [kk / jnp.maximum(jnp.sqrt(s), 1e-12) for kk, s in zip(kks, sq)]
    return rs, ks, vs, lws, [-kk for kk in kks], [kk * a for kk, a in zip(kks, a_sigs)]


def _rw_posts(os_, rs, ks, vs, gs, rk_ws, lnws, lnbs, ones_bd):
    idx = range(len(os_))
    sums = _seg_many(ones_bd, list(os_) + [rs[i] * ks[i] * rk_ws[i] for i in idx])
    ds = [os_[i] - sums[i] * (1.0 / RW_N) for i in idx]
    bonus = sums[len(os_):]
    var = [s * (1.0 / RW_N) for s in _seg_many(ones_bd, [d * d for d in ds])]
    return [(ds[i] * lax.rsqrt(var[i] + RW_LN_EPS) * lnws[i] + lnbs[i] + bonus[i] * vs[i]) * _silu(gs[i])
            for i in idx]


def _rw_chunks(rs, ks, vs, lws, avs, bvs, sbds, tril):
    n = RW_CHUNK
    pairs = range(len(rs))
    lane = lax.broadcasted_iota(jnp.int32, (n, 128), 1)
    row = lax.broadcasted_iota(jnp.int32, (n, 128), 0)
    m0 = lane < RW_N
    col = jnp.where(m0, lane, lane - RW_N)
    strict = col < row
    incl = col <= row
    eye = (col == row).astype(F32)

    def bd(x):
        xb = x.astype(BF16)
        zero = jnp.zeros_like(xb)
        return jnp.concatenate([jnp.where(m0, xb, zero), jnp.where(m0, zero, xb)], axis=0)

    cls = [_mm_exact(tril, lws[q]) for q in pairs]
    a_h = [avs[q] * jnp.exp(cls[q] - lws[q]) for q in pairs]
    r_h = [rs[q] * jnp.exp(cls[q]) for q in pairs]
    b_c = [bvs[q] * jnp.exp(-cls[q]) for q in pairs]
    k_c = [ks[q] * jnp.exp(-cls[q]) for q in pairs]
    lhs = [jnp.concatenate([a_h[q], r_h[q]], axis=0) for q in pairs]
    from_state = [_mm(lhs[q], sbds[q], NT) for q in pairs]
    m_bk = [_mm(lhs[q], jnp.concatenate([bd(b_c[q]), bd(k_c[q])], axis=0), NT) for q in pairs]
    m_ab = [jnp.where(strict, m[:n, :128], 0.0) for m in m_bk]
    m_ak = [jnp.where(strict, m[:n, 128:], 0.0) for m in m_bk]
    m_rb = [jnp.where(incl, m[n:, :128], 0.0) for m in m_bk]
    m_rk = [jnp.where(incl, m[n:, 128:], 0.0) for m in m_bk]
    ps = [_mm(m, bd(m)) for m in m_ab]
    xs = [eye + m for m in m_ab]
    for level in range(1, 5):
        both = [_mm(jnp.concatenate([ps[q], xs[q]], axis=0), bd(ps[q])) for q in pairs]
        ps = [m[:n] for m in both]
        xs = [xs[q] + both[q][n:] for q in pairs]
    xs = [xs[q] + _mm(xs[q], bd(ps[q])) for q in pairs]
    from_v = [_mm(jnp.concatenate([m_ak[q], m_rk[q]], axis=0), bd(vs[q])) for q in pairs]
    w_all = [from_state[q][:n] + from_v[q][:n] for q in pairs]
    u_all = [_mm(xs[q], bd(w_all[q])) for q in pairs]
    o_all = [from_state[q][n:] + from_v[q][n:] + _mm(m_rb[q], bd(u_all[q])) for q in pairs]

    vi = lax.broadcasted_iota(jnp.int32, (128, 128), 0)
    ki = lax.broadcasted_iota(jnp.int32, (128, 128), 1)
    same_head = (vi < RW_N) == (ki < RW_N)
    new = []
    for q in pairs:
        cl_end = cls[q][n - 1:n, :]
        e_end = jnp.exp(cl_end - cls[q])
        upd = _mm(jnp.concatenate([u_all[q], vs[q]], axis=0),
                  jnp.concatenate([bvs[q] * e_end, ks[q] * e_end], axis=0), TN)
        new.append(sbds[q] * jnp.exp(cl_end) + jnp.where(same_head, upd, 0.0))
    return o_all, new


def _rwkv_prompt_kernel(r_ref, k_ref, v_ref, g_ref, wdad_ref, mu_ref, w0_ref, a0_ref, kkw_ref, kaw_ref, rkw_ref,
                        lnw_ref, lnb_ref, w2a2_ref, tril_ref, ones_ref, o_ref, s_ref,
                        sbd_ref, pr_ref, pk_ref, pv_ref, pwd_ref, lw_ref, la_ref):
    c = pl.program_id(1)
    n = RW_CHUNK

    @pl.when(c == 0)
    def _():
        sbd_ref[...] = jnp.zeros_like(sbd_ref)
        pr_ref[...] = jnp.zeros_like(pr_ref)
        pk_ref[...] = jnp.zeros_like(pk_ref)
        pv_ref[...] = jnp.zeros_like(pv_ref)
        pwd_ref[...] = jnp.zeros_like(pwd_ref)

    row = lax.broadcasted_iota(jnp.int32, (n, 128), 0)

    def shift(z_ref, prev_ref, sl, mu_lo):
        z = z_ref[:, sl]
        zp = jnp.where(row == 0, prev_ref[:, sl], pltpu.roll(z, 1, axis=0))
        prev_ref[:, sl] = z[n - 1:n, :]
        return z + (zp - z) * mu_ref[:, mu_lo:mu_lo + 128]

    wdad = shift(wdad_ref, pwd_ref, slice(0, 128), 3 * D_MODEL)
    lora_w, lora_a = _rw_lora(wdad, w2a2_ref[...])
    lw_ref[...] = lora_w
    la_ref[...] = lora_a
    ones_bd = ones_ref[...]
    pairs = [slice(q * 128, (q + 1) * 128) for q in range(RW_HEADS // 2)]
    rs, ks, vs, lws, avs, bvs = _rw_prep_all(
        [(shift(r_ref, pr_ref, sl, q * 128), shift(k_ref, pk_ref, sl, D_MODEL + q * 128),
          shift(v_ref, pv_ref, sl, 2 * D_MODEL + q * 128), lw_ref[:, sl], la_ref[:, sl],
          w0_ref[:, sl], a0_ref[:, sl], kkw_ref[:, sl], kaw_ref[:, sl]) for q, sl in enumerate(pairs)], ones_bd)
    o_all, sbd_new = _rw_chunks(rs, ks, vs, lws, avs, bvs, [sbd_ref[q] for q in range(len(pairs))], tril_ref[...])
    for q in range(len(pairs)):
        sbd_ref[q] = sbd_new[q]
    outs = _rw_posts(o_all, rs, ks, vs, [g_ref[:, sl] for sl in pairs], [rkw_ref[:, sl] for sl in pairs],
                     [lnw_ref[:, sl] for sl in pairs], [lnb_ref[:, sl] for sl in pairs], ones_bd)
    for sl, out in zip(pairs, outs):
        o_ref[:, sl] = out.astype(o_ref.dtype)

    @pl.when(c == pl.num_programs(1) - 1)
    def _():
        s_ref[0] = sbd_ref[...]


def _rw_consts():
    tril = jnp.asarray(np.tril(np.ones((RW_CHUNK, RW_CHUNK))), BF16)
    ones_bd = jnp.asarray(np.kron(np.eye(2), np.ones((RW_N, RW_N))), BF16)
    return tril, ones_bd


def _rwkv_prompt(proj, p, bsz, t, l):
    n = RW_CHUNK
    n_t = t // n
    tril, ones_bd = _rw_consts()
    col = lambda blk: (lambda b, c: (b * n_t + c, blk))
    const = lambda shape: pl.BlockSpec(shape, lambda b, c: (0, 0))
    layer = lambda shape: pl.BlockSpec((None,) + shape, lambda b, c: (l, 0, 0))
    vec = layer((1, D_MODEL))
    prev = pltpu.VMEM((1, D_MODEL), F32)
    tile = pltpu.VMEM((n, D_MODEL), F32)
    o, sbd = pl.pallas_call(
        _rwkv_prompt_kernel,
        grid=(bsz, n_t),
        in_specs=[
            pl.BlockSpec((n, D_MODEL), col(C_R)), pl.BlockSpec((n, D_MODEL), col(C_K)),
            pl.BlockSpec((n, D_MODEL), col(C_V)), pl.BlockSpec((n, D_MODEL), col(C_GR)),
            pl.BlockSpec((n, 128), lambda b, c: (b * n_t + c, C_WDAD)),
            layer((1, RW_SHIFT)),
            vec, vec, vec, vec, vec, vec, vec,
            layer((128, D_MODEL)), const((n, n)), const((128, 128)),
        ],
        out_specs=[
            pl.BlockSpec((n, D_MODEL), lambda b, c: (b * n_t + c, 0)),
            pl.BlockSpec((1, RW_HEADS // 2, 128, 128), lambda b, c: (b, 0, 0, 0)),
        ],
        out_shape=[
            jax.ShapeDtypeStruct((bsz * t, D_MODEL), BF16),
            jax.ShapeDtypeStruct((bsz, RW_HEADS // 2, 128, 128), F32),
        ],
        scratch_shapes=[pltpu.VMEM((RW_HEADS // 2, 128, 128), F32), prev, prev, prev, pltpu.VMEM((1, 128), F32),
                        tile, tile],
        compiler_params=_params(("parallel", "arbitrary")),
        name="rwkv_prompt",
    )(proj, proj, proj, proj, proj, p["mu"], p["w0"], p["a0"], p["kk"], p["ka"], p["rk"], p["lnw"], p["lnb"],
      p["w2a2"], tril, ones_bd)
    s6 = sbd.reshape(bsz, RW_HEADS // 2, 2, RW_N, 2, RW_N)
    state = jnp.stack([s6[:, :, 0, :, 0, :], s6[:, :, 1, :, 1, :]], axis=2)
    return o, state.reshape(bsz, RW_HEADS, RW_N, RW_N)


def _rwkv_sample_kernel(r_ref, k_ref, v_ref, g_ref, wdad_ref, sh_ref, mu_ref, w0_ref, a0_ref, kkw_ref, kaw_ref,
                        rkw_ref, lnw_ref, lnb_ref, w2a2_ref, ones_ref, s_ref, o_ref, so_ref,
                        rs_ref, ks_ref, vs_ref, ob_ref, kr_ref, wr_ref, ar_ref, br_ref):
    ones_bd = ones_ref[...]
    tb = SAMPLE_TB

    def store_rows(ref, sl, x):
        for i in range(tb):
            ref[i, :, sl] = x[i:i + 1, :]

    def shift(z, lo, width):
        return z + (sh_ref[:, lo:lo + width] - z) * mu_ref[:, lo:lo + width]

    wdad = shift(wdad_ref[...], 3 * D_MODEL, 128)
    lora_w, lora_a = _rw_lora(wdad, w2a2_ref[...])
    pairs = [slice(q * 128, (q + 1) * 128) for q in range(RW_HEADS // 2)]
    rs, ks, vs, lws, avs, bvs = _rw_prep_all(
        [(shift(r_ref[:, sl], q * 128, 128), shift(k_ref[:, sl], D_MODEL + q * 128, 128),
          shift(v_ref[:, sl], 2 * D_MODEL + q * 128, 128), lora_w[:, sl], lora_a[:, sl], w0_ref[:, sl],
          a0_ref[:, sl], kkw_ref[:, sl], kaw_ref[:, sl]) for q, sl in enumerate(pairs)], ones_bd)
    for q, sl in enumerate(pairs):
        rs_ref[:, sl] = rs[q]
        ks_ref[:, sl] = ks[q]
        vs_ref[:, sl] = vs[q]
        store_rows(kr_ref, sl, ks[q])
        store_rows(wr_ref, sl, jnp.exp(lws[q]))
        store_rows(ar_ref, sl, avs[q])
        store_rows(br_ref, sl, bvs[q])
    ob_ref[...] = jnp.zeros_like(ob_ref)
    rows = lax.broadcasted_iota(jnp.int32, (tb, 1), 0)

    def per_seq(b, carry):
        sel = rows == b
        hs = range(RW_HEADS)
        lanes = [slice(h * RW_N, (h + 1) * RW_N) for h in hs]

        def row_of(ref, h):
            pair = slice((h // 2) * 128, (h // 2 + 1) * 128)
            return ref[b, :, pair][:, (h % 2) * RW_N:(h % 2 + 1) * RW_N]

        vcol = [_col_bcast(jnp.where(sel, vs_ref[:, lanes[h]], 0.0), RW_N) for h in hs]
        s_old = [s_ref[b, h] for h in hs]
        sa = [jnp.sum(s_old[h] * row_of(ar_ref, h), axis=-1, keepdims=True) for h in hs]
        s_new = [s_old[h] * row_of(wr_ref, h) + sa[h] * row_of(br_ref, h) + vcol[h] * row_of(kr_ref, h)
                 for h in hs]
        for h in hs:
            so_ref[b, h] = s_new[h]
        outs = [_mm(jnp.where(sel, rs_ref[:, lanes[h]], 0.0), s_new[h], NT) for h in hs]
        for h in hs:
            ob_ref[:, lanes[h]] += outs[h]
        return carry

    lax.fori_loop(0, tb, per_seq, 0)
    outs = _rw_posts([ob_ref[:, sl] for sl in pairs], rs, ks, vs, [g_ref[:, sl] for sl in pairs],
                     [rkw_ref[:, sl] for sl in pairs], [lnw_ref[:, sl] for sl in pairs],
                     [lnb_ref[:, sl] for sl in pairs], ones_bd)
    for sl, out in zip(pairs, outs):
        o_ref[:, sl] = out


def _rwkv_sample(proj, p, shift_state, state, l, new_state):
    bsz = state.shape[1]
    tb = SAMPLE_TB
    _, ones_bd = _rw_consts()
    col = lambda blk: (lambda i: (i, blk))
    layer = lambda shape: pl.BlockSpec((None,) + shape, lambda i: (l, 0, 0))
    vec = layer((1, D_MODEL))
    tile = pltpu.VMEM((tb, D_MODEL), F32)
    sspec = pl.BlockSpec((None, tb, RW_HEADS, RW_N, RW_N), lambda i: (l, i, 0, 0, 0))
    return _sample_call(
        _rwkv_sample_kernel, "rwkv_sample", bsz,
        [
            pl.BlockSpec((tb, D_MODEL), col(C_R)), pl.BlockSpec((tb, D_MODEL), col(C_K)),
            pl.BlockSpec((tb, D_MODEL), col(C_V)), pl.BlockSpec((tb, D_MODEL), col(C_GR)),
            pl.BlockSpec((tb, 128), lambda i: (i, C_WDAD)),
            pl.BlockSpec((None, tb, RW_SHIFT), lambda i: (l, i, 0)),
            layer((1, RW_SHIFT)),
            vec, vec, vec, vec, vec, vec, vec,
            layer((128, D_MODEL)),
            pl.BlockSpec((128, 128), lambda i: (0, 0)),
            sspec,
        ],
        (proj, proj, proj, proj, proj, shift_state, p["mu"], p["w0"], p["a0"], p["kk"], p["ka"], p["rk"],
         p["lnw"], p["lnb"], p["w2a2"], ones_bd, state),
        sspec, state.shape, [tile] * 4 + [pltpu.VMEM((tb, 1, D_MODEL), F32)] * 4, new_state)


def _mb_post(y, xs, z, d_rep, nw):
    y = (y + d_rep * xs) * _silu(z)
    ms = jnp.mean(y * y, axis=-1, keepdims=True)
    return y * lax.rsqrt(ms + RMS_EPS) * nw


def _ssd_prompt_kernel(x_ref, bc_ref, z_ref, dt_ref, cw_ref, cb_ref, dtb_ref, alog_ref, d_ref, nw_ref, tril_ref,
                       o_ref, s_ref, h_ref, px_ref, pbc_ref):
    c = pl.program_id(1)
    n = MB_CHUNK

    @pl.when(c == 0)
    def _():
        h_ref[...] = jnp.zeros_like(h_ref)
        px_ref[...] = jnp.zeros_like(px_ref)
        pbc_ref[...] = jnp.zeros_like(pbc_ref)

    def conv(u_ref, prev_ref, sl, wsl):
        u = u_ref[:, sl]
        prev = prev_ref[:, sl]
        w = cw_ref[:, wsl]
        row = lax.broadcasted_iota(jnp.int32, (8, u.shape[1]), 0)
        acc = u * w[MB_CONV - 1:MB_CONV, :] + cb_ref[:, wsl]
        for s in range(1, MB_CONV):
            us = pltpu.roll(u, s, axis=0)
            top = jnp.where(row < s, pltpu.roll(prev, s, axis=0), us[0:8])
            us = jnp.concatenate([top, us[8:]], axis=0)
            acc = acc + us * w[MB_CONV - 1 - s:MB_CONV - s, :]
        prev_ref[:, sl] = u[n - 8:n, :]
        return _silu(acc)

    tril = tril_ref[...]
    ti = lax.broadcasted_iota(jnp.int32, (n, n), 0)
    si = lax.broadcasted_iota(jnp.int32, (n, n), 1)
    incl = si <= ti
    lane = lax.broadcasted_iota(jnp.int32, (n, 4 * MB_P), 1)
    sum_cum = jnp.concatenate([jnp.ones((n, MB_N), BF16), (ti <= si).astype(BF16)], axis=1)
    nbc = MB_GROUPS * MB_N
    groups = range(MB_GROUPS)
    gx = [slice(g * 256, (g + 1) * 256) for g in groups]

    xs = [conv(x_ref, px_ref, gx[g], gx[g]) for g in groups]
    bm = [conv(bc_ref, pbc_ref, slice(g * MB_N, (g + 1) * MB_N),
               slice(D_MODEL + g * MB_N, D_MODEL + (g + 1) * MB_N)) for g in groups]
    cm = [conv(bc_ref, pbc_ref, slice(nbc + g * MB_N, nbc + (g + 1) * MB_N),
               slice(D_MODEL + nbc + g * MB_N, D_MODEL + nbc + (g + 1) * MB_N)) for g in groups]
    dt = [_softplus(dt_ref[:, gx[g]] + dtb_ref[:, gx[g]]) for g in groups]
    a = [dt[g] * (-jnp.exp(alog_ref[:, gx[g]])) for g in groups]
    xdt = [xs[g] * dt[g] for g in groups]
    cs = [_mm_exact(tril, a[g]) for g in groups]
    tr = [_mm_exact(sum_cum, a[g], TN, x_is_lhs=True) for g in groups]
    gmat = [_mm(cm[g], bm[g], NT) for g in groups]
    h_all = [h_ref[g] for g in groups]
    y = [jnp.exp(cs[g]) * _mm(cm[g], h_all[g], NT) for g in groups]
    for g in groups:
        cs_t = tr[g][:, MB_N:]
        for e in range(4):
            hd = slice(e * MB_P, (e + 1) * MB_P)
            lmat = jnp.where(incl, jnp.exp(cs[g][:, hd] - cs_t[hd, :]), 0.0)
            me = (lane >= e * MB_P) & (lane < (e + 1) * MB_P)
            y[g] = y[g] + _mm(gmat[g] * lmat, jnp.where(me, xdt[g], 0.0))
    for g in groups:
        cs_end = cs[g][n - 1:n, :]
        h_new = h_all[g] * jnp.exp(tr[g][:, :MB_N]) + _mm(xdt[g] * jnp.exp(cs_end - cs[g]), bm[g], TN)
        h_ref[g] = h_new
    for g in groups:
        o_ref[:, gx[g]] = _mb_post(y[g], xs[g], z_ref[:, gx[g]], d_ref[:, gx[g]],
                                   nw_ref[:, gx[g]]).astype(o_ref.dtype)

    @pl.when(c == pl.num_programs(1) - 1)
    def _():
        for g in range(MB_GROUPS):
            for e in range(4):
                s_ref[0, 4 * g + e] = h_ref[g, e * MB_P:(e + 1) * MB_P, :]


def _ssd_prompt(proj, p, bsz, t, l):
    n = MB_CHUNK
    n_t = t // n
    tril = jnp.asarray(np.tril(np.ones((n, n))), BF16)
    col = lambda blk: (lambda b, c: (b * n_t + c, blk))
    const = lambda shape: pl.BlockSpec(shape, lambda b, c: (0, 0))
    layer = lambda shape: pl.BlockSpec((None,) + shape, lambda b, c: (l, 0, 0))
    vec = layer((1, D_MODEL))
    return pl.pallas_call(
        _ssd_prompt_kernel,
        grid=(bsz, n_t),
        in_specs=[
            pl.BlockSpec((n, D_MODEL), col(C_X)), pl.BlockSpec((n, D_MODEL), col(C_BC)),
            pl.BlockSpec((n, D_MODEL), col(C_Z)), pl.BlockSpec((n, D_MODEL), col(C_DT)),
            layer((MB_CONV, MB_XBC)), layer((1, MB_XBC)),
            vec, vec, vec, vec,
            const((n, n)),
        ],
        out_specs=[
            pl.BlockSpec((n, D_MODEL), lambda b, c: (b * n_t + c, 0)),
            pl.BlockSpec((1, MB_HEADS, MB_P, MB_N), lambda b, c: (b, 0, 0, 0)),
        ],
        out_shape=[
            jax.ShapeDtypeStruct((bsz * t, D_MODEL), BF16),
            jax.ShapeDtypeStruct((bsz, MB_HEADS, MB_P, MB_N), F32),
        ],
        scratch_shapes=[pltpu.VMEM((MB_GROUPS, 4 * MB_P, MB_N), F32), pltpu.VMEM((8, D_MODEL), F32),
                        pltpu.VMEM((8, D_MODEL), F32)],
        compiler_params=_params(("parallel", "arbitrary")),
        name="ssd_prompt",
    )(proj, proj, proj, proj, p["cw"], p["cb"], p["dtb"], p["alog"], p["d"], p["nw"], tril)


def _ssd_sample_kernel(xbc_ref, z_ref, dt_ref, cs_ref, cw_ref, cb_ref, dtb_ref, alog_ref, d_ref, nw_ref, s_ref,
                       o_ref, so_ref, xs_ref, xd_ref, bs_ref, cms_ref, dec_ref, ob_ref):
    tb = SAMPLE_TB
    w = cw_ref[...]
    acc = xbc_ref[...] * w[MB_CONV - 1:MB_CONV, :] + cb_ref[...]
    for i in range(MB_CONV - 1):
        acc = acc + cs_ref[i] * w[i:i + 1, :]
    xbc = _silu(acc)
    xs = xbc[:, :D_MODEL]
    dt = _softplus(dt_ref[...] + dtb_ref[...])
    xs_ref[...] = xs
    xd_ref[...] = xs * dt
    bs_ref[...] = xbc[:, D_MODEL:D_MODEL + MB_GROUPS * MB_N]
    cms_ref[...] = xbc[:, D_MODEL + MB_GROUPS * MB_N:]
    dec = jnp.exp(dt * (-jnp.exp(alog_ref[...])))
    for i in range(tb):
        dec_ref[i] = dec[i:i + 1, :]
    ob_ref[...] = jnp.zeros_like(ob_ref)
    rows = lax.broadcasted_iota(jnp.int32, (tb, 1), 0)

    def per_seq(b, carry):
        sel = rows == b
        hs = range(MB_HEADS)
        lanes = [slice(h * MB_P, (h + 1) * MB_P) for h in hs]
        grp = [slice((h // 4) * MB_N, (h // 4 + 1) * MB_N) for h in hs]

        def dec_of(h):
            pair = slice((h // 2) * 128, (h // 2 + 1) * 128)
            return dec_ref[b, :, pair][:, (h % 2) * MB_P:(h % 2) * MB_P + 1]

        upd = [_mm(jnp.where(sel, xd_ref[:, lanes[h]], 0.0), bs_ref[:, grp[h]], TN) for h in hs]
        h_new = [dec_of(h) * s_ref[b, h] + upd[h] for h in hs]
        for h in hs:
            so_ref[b, h] = h_new[h]
        outs = [_mm(jnp.where(sel, cms_ref[:, grp[h]], 0.0), h_new[h], NT) for h in hs]
        for h in hs:
            ob_ref[:, lanes[h]] += outs[h]
        return carry

    lax.fori_loop(0, tb, per_seq, 0)
    for g in range(MB_GROUPS):
        sl = slice(g * 256, (g + 1) * 256)
        o_ref[:, sl] = _mb_post(ob_ref[:, sl], xs_ref[:, sl], z_ref[:, sl], d_ref[:, sl], nw_ref[:, sl])


def _ssd_sample(proj, p, conv_state, state, l, new_state):
    bsz = state.shape[1]
    tb = SAMPLE_TB
    layer = lambda shape: pl.BlockSpec((None,) + shape, lambda i: (l, 0, 0))
    vec = layer((1, D_MODEL))
    tile = pltpu.VMEM((tb, D_MODEL), F32)
    half = pltpu.VMEM((tb, MB_GROUPS * MB_N), F32)
    sspec = pl.BlockSpec((None, tb, MB_HEADS, MB_P, MB_N), lambda i: (l, i, 0, 0, 0))
    return _sample_call(
        _ssd_sample_kernel, "ssd_sample", bsz,
        [
            pl.BlockSpec((tb, MB_XBC), lambda i: (i, C_X // 2)),
            pl.BlockSpec((tb, D_MODEL), lambda i: (i, C_Z)),
            pl.BlockSpec((tb, D_MODEL), lambda i: (i, C_DT)),
            pl.BlockSpec((None, MB_CONV - 1, tb, MB_XBC), lambda i: (l, 0, i, 0)),
            layer((MB_CONV, MB_XBC)), layer((1, MB_XBC)),
            vec, vec, vec, vec,
            sspec,
        ],
        (proj, proj, proj, conv_state, p["cw"], p["cb"], p["dtb"], p["alog"], p["d"], p["nw"], state),
        sspec, state.shape, [tile, tile, half, half, pltpu.VMEM((tb, 1, D_MODEL), F32), tile], new_state)


def _prep_w_in(w_in):
    o = np.cumsum([0, 1024, 1024, 1024, 1024, RW_SHIFT, 1024, 1024, MB_XBC, MB_HEADS, 3 * D_MODEL])
    zrw, grw, zmb, xbc, dtc, gate = o[4], o[5], o[6], o[7], o[8], o[9]
    parts = [
        w_in[..., :zrw], w_in[..., zrw:zrw + 3 * D_MODEL], w_in[..., grw:zmb], w_in[..., xbc:dtc],
        w_in[..., zmb:xbc], w_in[..., gate:],
        jnp.repeat(w_in[..., dtc:gate], MB_P, axis=-1),
        w_in[..., zrw + 3 * D_MODEL:grw],
    ]
    return jnp.concatenate(parts, axis=-1).astype(BF16)


def _rep(v):
    return jnp.repeat(v, MB_P, axis=-1)[:, None, :]


def kernel(x_prompt, x_sample, state_hgrn, state_rwkv, state_rwkv_shift, state_ssm, state_conv, norm_w, w_in, hg_lb, hg_norm_w, rw_mu, rw_w0, rw_w2, rw_a0, rw_a2, rw_k_k, rw_k_a, rw_r_k, rw_ln_w, rw_ln_b, mb_conv_w, mb_conv_b, mb_dt_bias, mb_A_log, mb_D, mb_norm_w, w_o_hg, w_o_rw, w_o_mb, w_out, final_norm_w):
    bp, t, _ = x_prompt.shape
    bs = x_sample.shape[0]
    sm = jax.nn.softmax(hg_lb.astype(F32), axis=0)
    row = lambda a: a[:, None, :]
    lbs = row(jnp.cumsum(sm, axis=0) - sm[0:1])
    w_in_r = _prep_w_in(w_in)
    rw_p = dict(mu=row(rw_mu), w0=row(rw_w0), a0=row(rw_a0), kk=row(rw_k_k), ka=row(rw_k_a),
                rk=rw_r_k.reshape(DEPTH, 1, D_MODEL), lnw=row(rw_ln_w), lnb=row(rw_ln_b),
                w2a2=jnp.concatenate([rw_w2, rw_a2], axis=1))
    mb_p = dict(cw=mb_conv_w, cb=row(mb_conv_b), dtb=_rep(mb_dt_bias), alog=_rep(mb_A_log), d=_rep(mb_D),
                nw=row(mb_norm_w))
    nw_in, nw_hg = row(norm_w), row(hg_norm_w)
    whg, wrw, wmb, wout = (w.astype(BF16) for w in (w_o_hg, w_o_rw, w_o_mb, w_out))
    fw = final_norm_w[None, :]
    conv_t = jnp.swapaxes(state_conv, 1, 2)

    hp = x_prompt.reshape(bp * t, D_MODEL)
    hs = x_sample.reshape(bs, D_MODEL)
    p_states, s_small = [], []
    s_hg = s_rw = s_ssm = None
    yp = ys = None
    for l in range(DEPTH):
        final = l == DEPTH - 1

        proj = _inproj(hp, nw_in, w_in_r, l)
        o_hg, p_hg = _hgrn_prompt(proj, lbs, nw_hg, bp, t, l)
        o_rw, p_rw = _rwkv_prompt(proj, rw_p, bp, t, l)
        o_mb, p_ssm = _ssd_prompt(proj, mb_p, bp, t, l)
        proj3 = proj.reshape(bp, t, NCOL)
        p_shift = jnp.concatenate([proj3[:, -1, C_R * 1024:C_GR * 1024], proj3[:, -1, C_WDAD * 128:]], axis=-1)
        p_conv = proj3[:, t - (MB_CONV - 1):, C_X * 1024:C_Z * 1024]
        res = _merge(o_hg, o_rw, o_mb, proj, hp, whg, wrw, wmb, wout, fw, final, l)
        hp = res[0]
        if final:
            yp = res[1]
        p_states.append((p_hg, p_rw, p_shift, p_ssm, p_conv))

        proj = _inproj(hs, nw_in, w_in_r, l)
        o_hg, s_hg = _hgrn_sample(proj, lbs, nw_hg, state_hgrn, l, s_hg)
        o_rw, s_rw = _rwkv_sample(proj, rw_p, state_rwkv_shift, state_rwkv, l, s_rw)
        o_mb, s_ssm = _ssd_sample(proj, mb_p, conv_t, state_ssm, l, s_ssm)
        s_shift = jnp.concatenate([proj[:, C_R * 1024:C_GR * 1024], proj[:, C_WDAD * 128:]], axis=-1)
        s_conv = jnp.concatenate([state_conv[l][:, 1:], proj[:, None, C_X * 1024:C_Z * 1024]], axis=1)
        res = _merge(o_hg, o_rw, o_mb, proj, hs, whg, wrw, wmb, wout, fw, final, l)
        hs = res[0]
        if final:
            ys = res[1]
        s_small.append((s_shift, s_conv))

    stack = lambda states, i: jnp.stack([s[i] for s in states])
    return (yp.reshape(bp, t, D_MODEL), ys.reshape(bs, 1, D_MODEL),
            *[stack(p_states, i) for i in range(5)],
            s_hg, s_rw, stack(s_small, 0), s_ssm, stack(s_small, 1))
```

```python
import functools

import numpy as np
import jax
import jax.numpy as jnp
from jax import lax
from jax.experimental import pallas as pl
from jax.experimental.pallas import tpu as pltpu

F32 = jnp.float32
BF16 = jnp.bfloat16
HI = lax.Precision.HIGHEST

D_MODEL = 1024
DEPTH = 2
HG_HEADS, HG_DK, HG_DV = 8, 128, 128
HG_F_MIN = 1e-20
LOG2E = 1.4426950408889634
RW_HEADS, RW_N = 16, 64
RW_LORA = 64
RW_SHIFT = 3 * D_MODEL + 2 * RW_LORA
RW_LN_EPS = 64e-5
RW_DECAY_SCALE = float(np.exp(-0.5))
MB_HEADS, MB_P, MB_GROUPS, MB_N, MB_CONV = 16, 64, 4, 128, 4
MB_XBC = D_MODEL + 2 * MB_GROUPS * MB_N
RMS_EPS = 1e-6

C_Q, C_F, C_I, C_GH, C_R, C_K, C_V, C_GR, C_X, C_BC, C_Z, C_G0, C_G1, C_G2, C_DT = range(15)
C_WDAD = 15 * 8
NCOL = 15 * 1024 + 128

HG_BLK = 16
RW_CHUNK = 64
MB_CHUNK = 64
SAMPLE_TB = 8

VMEM_LIMIT = 48 * 1024 * 1024

NN = (((1,), (0,)), ((), ()))
NT = (((1,), (1,)), ((), ()))
TN = (((0,), (0,)), ((), ()))


def _mm(a, b, dims=NN, hi=False):
    if hi:
        return lax.dot_general(a.astype(F32), b.astype(F32), dims, precision=HI, preferred_element_type=F32)
    return lax.dot_general(a.astype(BF16), b.astype(BF16), dims, preferred_element_type=F32)


def _mm_exact(e, x, dims=NN, x_is_lhs=False, passes=3):
    e = e.astype(BF16)
    acc = None
    for _ in range(passes):
        p = x.astype(BF16)
        x = x - p.astype(F32)
        t = (lax.dot_general(p, e, dims, preferred_element_type=F32) if x_is_lhs
             else lax.dot_general(e, p, dims, preferred_element_type=F32))
        acc = t if acc is None else acc + t
    return acc


def _mm3(a, b, dims=NN):
    a_hi = a.astype(BF16)
    a_lo = (a - a_hi.astype(F32)).astype(BF16)
    b_hi = b.astype(BF16)
    b_lo = (b - b_hi.astype(F32)).astype(BF16)
    dot = lambda x, y: lax.dot_general(x, y, dims, preferred_element_type=F32)
    return dot(a_hi, b_hi) + (dot(a_hi, b_lo) + dot(a_lo, b_hi))


def _seg_many(ones_bd, xs):
    m = xs[0].shape[0]
    out = _mm_exact(ones_bd, jnp.concatenate(xs, axis=0) if len(xs) > 1 else xs[0], x_is_lhs=True, passes=2)
    return [out[i * m:(i + 1) * m] for i in range(len(xs))]


def _sigmoid(x):
    return jax.nn.sigmoid(x)


def _silu(x):
    return x * jax.nn.sigmoid(x)


def _softplus(x):
    return jnp.maximum(x, 0.0) + jnp.log1p(jnp.exp(-jnp.abs(x)))


def _col_bcast(rows, width):
    ones = jnp.ones((rows.shape[0], width), BF16)
    p1 = rows.astype(BF16)
    r1 = rows - p1.astype(F32)
    p2 = r1.astype(BF16)
    p3 = (r1 - p2.astype(F32)).astype(BF16)
    out = lax.dot_general(p1, ones, TN, preferred_element_type=F32)
    out = out + lax.dot_general(p2, ones, TN, preferred_element_type=F32)
    return out + lax.dot_general(p3, ones, TN, preferred_element_type=F32)


def _params(sem):
    return pltpu.CompilerParams(dimension_semantics=sem, vmem_limit_bytes=VMEM_LIMIT)


def _inproj_kernel(x_ref, nw_ref, w_ref, o_ref, xn_ref):
    @pl.when(pl.program_id(1) == 0)
    def _():
        x = x_ref[...]
        ms = jnp.mean(x * x, axis=-1, keepdims=True)
        xn_ref[...] = (x * lax.rsqrt(ms + RMS_EPS) * nw_ref[...]).astype(BF16)

    o_ref[...] = jnp.dot(xn_ref[...], w_ref[...], preferred_element_type=F32)


def _inproj(x2d, nw, w, l):
    m = x2d.shape[0]
    tm = min(1024, m)
    tn = 1408
    return pl.pallas_call(
        _inproj_kernel,
        grid=(m // tm, NCOL // tn),
        in_specs=[
            pl.BlockSpec((tm, D_MODEL), lambda i, j: (i, 0)),
            pl.BlockSpec((None, 1, D_MODEL), lambda i, j: (l, 0, 0)),
            pl.BlockSpec((None, D_MODEL, tn), lambda i, j: (l, 0, j)),
        ],
        out_specs=pl.BlockSpec((tm, tn), lambda i, j: (i, j)),
        out_shape=jax.ShapeDtypeStruct((m, NCOL), F32),
        scratch_shapes=[pltpu.VMEM((tm, D_MODEL), BF16)],
        compiler_params=_params(("parallel", "arbitrary")),
        name="inproj",
    )(x2d, nw, w)


def _merge_kernel(ohg, orw, omb, g0, g1, g2, h_ref, whg, wrw, wmb, wout, fw_ref, hn_ref, *y_ref):
    u = _sigmoid(g0[...]) * _mm(ohg[...], whg[...])
    u = u + _sigmoid(g1[...]) * _mm(orw[...], wrw[...])
    u = u + _sigmoid(g2[...]) * _mm(omb[...], wmb[...])
    hn = h_ref[...] + _mm(u, wout[...])
    hn_ref[...] = hn
    if y_ref:
        ms = jnp.mean(hn * hn, axis=-1, keepdims=True)
        y_ref[0][...] = hn * lax.rsqrt(ms + RMS_EPS) * fw_ref[...]


def _merge(ohg, orw, omb, proj, h2d, whg, wrw, wmb, wout, fw, final, l):
    m = h2d.shape[0]
    tm = min(256, m)
    row = lambda i: (i, 0)
    const = lambda i: (0, 0)
    wspec = pl.BlockSpec((None, D_MODEL, D_MODEL), lambda i: (l, 0, 0))
    out_shape = [jax.ShapeDtypeStruct((m, D_MODEL), F32)]
    out_specs = [pl.BlockSpec((tm, D_MODEL), row)]
    if final:
        out_shape.append(jax.ShapeDtypeStruct((m, D_MODEL), F32))
        out_specs.append(pl.BlockSpec((tm, D_MODEL), row))
    return pl.pallas_call(
        _merge_kernel,
        grid=(m // tm,),
        in_specs=[
            pl.BlockSpec((tm, D_MODEL), row), pl.BlockSpec((tm, D_MODEL), row), pl.BlockSpec((tm, D_MODEL), row),
            pl.BlockSpec((tm, D_MODEL), lambda i: (i, C_G0)),
            pl.BlockSpec((tm, D_MODEL), lambda i: (i, C_G1)),
            pl.BlockSpec((tm, D_MODEL), lambda i: (i, C_G2)),
            pl.BlockSpec((tm, D_MODEL), row),
            wspec, wspec, wspec, wspec,
            pl.BlockSpec((1, D_MODEL), const),
        ],
        out_specs=out_specs,
        out_shape=out_shape,
        compiler_params=_params(("parallel",)),
        name="merge",
    )(ohg, orw, omb, proj, proj, proj, h2d, whg, wrw, wmb, wout, fw)


def _sample_call(body, name, bsz, in_specs, args, state_spec, state_shape, scratch, new_state):
    tb = SAMPLE_TB
    aliases = {}
    if new_state is not None:
        in_specs = in_specs + [pl.BlockSpec(memory_space=pl.ANY)]
        args = args + (new_state,)
        aliases = {len(args) - 1: 1}
        inner = body
        body = lambda *refs: inner(*refs[:len(args) - 1], *refs[len(args):])
    return pl.pallas_call(
        body,
        grid=(bsz // tb,),
        in_specs=in_specs,
        out_specs=[pl.BlockSpec((tb, D_MODEL), lambda i: (i, 0)), state_spec],
        out_shape=[jax.ShapeDtypeStruct((bsz, D_MODEL), F32), jax.ShapeDtypeStruct(state_shape, F32)],
        scratch_shapes=scratch,
        input_output_aliases=aliases,
        compiler_params=_params(("parallel",)),
        name=name,
    )(*args)


def _hgrn_prep(q, f, lb):
    sig = _sigmoid(f)
    fg = lb + (1.0 - lb) * sig
    logf = jnp.log(jnp.maximum(fg, HG_F_MIN))
    kx = (1.0 - lb) * (1.0 - sig)
    qx = _silu(q) * (HG_DK ** -0.5)
    return qx, kx, logf


def _hgrn_post(o, g, nw):
    ms = jnp.mean(o * o, axis=-1, keepdims=True)
    return o * lax.rsqrt(ms + RMS_EPS) * nw * _silu(g)


def _hgrn_prompt_kernel(q_ref, f_ref, i_ref, g_ref, lb_ref, nw_ref, tril_ref, o_ref, s_ref,
                        st_ref, qs_ref, ks_ref, bs_ref, ob_ref, *, nblk):
    c = pl.program_id(1)

    @pl.when(c == 0)
    def _():
        st_ref[...] = jnp.zeros_like(st_ref)

    tril = tril_ref[...]
    heads = [slice(h * 128, (h + 1) * 128) for h in range(HG_HEADS)]
    for sl in heads:
        qx, kx, logf = _hgrn_prep(q_ref[:, sl], f_ref[:, sl], lb_ref[:, sl])
        qs_ref[:, sl] = qx
        ks_ref[:, sl] = kx
        bs_ref[:, sl] = _mm_exact(tril, logf) * LOG2E
    half = HG_BLK // 2
    lane8 = lax.broadcasted_iota(jnp.int32, (half, HG_DK), 1)
    top16 = lax.broadcasted_iota(jnp.int32, (HG_BLK, HG_DK), 0) < half
    lower = (lax.broadcasted_iota(jnp.int32, (HG_BLK, HG_BLK), 1)
             <= lax.broadcasted_iota(jnp.int32, (HG_BLK, HG_BLK), 0))

    hs = range(HG_HEADS)

    def intra_weights(j):
        r0 = j * HG_BLK if isinstance(j, int) else pl.multiple_of(j * HG_BLK, HG_BLK)
        top, bot, blk = pl.ds(r0, half), pl.ds(r0 + half, half), pl.ds(r0, HG_BLK)
        amat = []
        for sl in heads:
            q_t, q_b, b_t, b_b = qs_ref[top, sl], qs_ref[bot, sl], bs_ref[top, sl], bs_ref[bot, sl]
            k16, b16 = ks_ref[blk, sl], bs_ref[blk, sl]
            a_top = jnp.zeros((half, HG_DK), F32)
            a_bot = jnp.zeros((half, HG_DK), F32)
            for jj in range(half):
                s = jnp.sum(q_t * jnp.exp2(b_t - b16[jj:jj + 1]) * k16[jj:jj + 1], axis=-1, keepdims=True)
                a_top = jnp.where(lane8 == jj, s, a_top)
                jb = half + jj
                s = jnp.sum(q_b * jnp.exp2(b_b - b16[jb:jb + 1]) * k16[jb:jb + 1], axis=-1, keepdims=True)
                a_bot = jnp.where(lane8 == jb, s, a_bot)
            b_mid = b16[half - 1:half]
            k_dec = jnp.where(top16, k16 * jnp.exp2(b_mid - b16), 0.0)
            cross = _mm(q_b * jnp.exp2(b_b - b_mid), k_dec, NT)
            full = jnp.concatenate([a_top[:, :HG_BLK], a_bot[:, :HG_BLK] + cross], axis=0)
            amat.append(jnp.where(lower, full, 0.0))
        return tuple(amat)

    def apply_block(j, amat):
        rows = pl.ds(pl.multiple_of(j * HG_BLK, HG_BLK), HG_BLK)
        qb = [qs_ref[rows, sl] for sl in heads]
        kb = [ks_ref[rows, sl] for sl in heads]
        bb = [bs_ref[rows, sl] for sl in heads]
        vb = [i_ref[rows, sl] for sl in heads]
        st = [st_ref[h] for h in hs]
        intra = [_mm(amat[h], vb[h]) for h in hs]
        inter = [_mm(qb[h] * jnp.exp2(bb[h]), st[h], NT) for h in hs]
        for h in hs:
            b_end = bb[h][HG_BLK - 1:HG_BLK, :]
            st_ref[h] = st[h] * jnp.exp2(b_end) + _mm(vb[h], kb[h] * jnp.exp2(b_end - bb[h]), TN)
        for h, sl in enumerate(heads):
            ob_ref[rows, sl] = intra[h] + inter[h]

    def step(i, amats):
        apply_block(2 * i - 2, amats[0])
        apply_block(2 * i - 1, amats[1])
        return intra_weights(2 * i), intra_weights(2 * i + 1)

    amats = lax.fori_loop(1, nblk // 2, step, (intra_weights(0), intra_weights(1)))
    apply_block(nblk - 2, amats[0])
    apply_block(nblk - 1, amats[1])
    nw = nw_ref[...]
    for sl in heads:
        o_ref[:, sl] = _hgrn_post(ob_ref[:, sl], g_ref[:, sl], nw).astype(o_ref.dtype)

    @pl.when(c == pl.num_programs(1) - 1)
    def _():
        for h in range(HG_HEADS):
            s_ref[0, h] = st_ref[h].T


def _hgrn_prompt(proj, lb, nw, bsz, t, l):
    tt = min(256, t)
    n_t = t // tt
    tril = jnp.asarray(np.kron(np.eye(tt // HG_BLK), np.tril(np.ones((HG_BLK, HG_BLK)))), BF16)
    col = lambda blk: (lambda b, c: (b * n_t + c, blk))
    const = lambda shape: pl.BlockSpec(shape, lambda b, c: (0, 0))
    layer = lambda shape: pl.BlockSpec((None,) + shape, lambda b, c: (l, 0, 0))
    tile = pltpu.VMEM((tt, D_MODEL), F32)
    return pl.pallas_call(
        functools.partial(_hgrn_prompt_kernel, nblk=tt // HG_BLK),
        grid=(bsz, n_t),
        in_specs=[
            pl.BlockSpec((tt, D_MODEL), col(C_Q)), pl.BlockSpec((tt, D_MODEL), col(C_F)),
            pl.BlockSpec((tt, D_MODEL), col(C_I)), pl.BlockSpec((tt, D_MODEL), col(C_GH)),
            layer((1, D_MODEL)), layer((1, 128)), const((tt, tt)),
        ],
        out_specs=[
            pl.BlockSpec((tt, D_MODEL), lambda b, c: (b * n_t + c, 0)),
            pl.BlockSpec((1, HG_HEADS, HG_DK, HG_DV), lambda b, c: (b, 0, 0, 0)),
        ],
        out_shape=[
            jax.ShapeDtypeStruct((bsz * t, D_MODEL), BF16),
            jax.ShapeDtypeStruct((bsz, HG_HEADS, HG_DK, HG_DV), F32),
        ],
        scratch_shapes=[pltpu.VMEM((HG_HEADS, HG_DV, HG_DK), F32), tile, tile, tile, tile],
        compiler_params=_params(("parallel", "arbitrary")),
        name="hgrn_prompt",
    )(proj, proj, proj, proj, lb, nw, tril)


def _hgrn_sample_kernel(q_ref, f_ref, i_ref, g_ref, lb_ref, nw_ref, s_ref, o_ref, so_ref,
                        qs_ref, ks_ref, fs_ref, ob_ref):
    qx, kx, logf = _hgrn_prep(q_ref[...], f_ref[...], lb_ref[...])
    qs_ref[...] = qx
    ks_ref[...] = kx
    fs_ref[...] = jnp.exp(logf)
    ob_ref[...] = jnp.zeros_like(ob_ref)
    rows = lax.broadcasted_iota(jnp.int32, (SAMPLE_TB, 1), 0)

    heads = [slice(h * 128, (h + 1) * 128) for h in range(HG_HEADS)]

    def per_seq(b, carry):
        sel = rows == b
        fcol = [_col_bcast(jnp.where(sel, fs_ref[:, sl], 0.0), HG_DV) for sl in heads]
        kv = [_mm(jnp.where(sel, ks_ref[:, sl], 0.0), i_ref[:, sl], TN) for sl in heads]
        s_new = [fcol[h] * s_ref[b, h] + kv[h] for h in range(HG_HEADS)]
        for h in range(HG_HEADS):
            so_ref[b, h] = s_new[h]
        outs = [_mm(jnp.where(sel, qs_ref[:, sl], 0.0), s_new[h]) for h, sl in enumerate(heads)]
        for sl, o in zip(heads, outs):
            ob_ref[:, sl] += o
        return carry

    lax.fori_loop(0, SAMPLE_TB, per_seq, 0)
    nw = nw_ref[...]
    for h in range(HG_HEADS):
        sl = slice(h * 128, (h + 1) * 128)
        o_ref[:, sl] = _hgrn_post(ob_ref[:, sl], g_ref[:, sl], nw)


def _hgrn_sample(proj, lb, nw, state, l, new_state):
    bsz = state.shape[1]
    tb = SAMPLE_TB
    col = lambda blk: (lambda i: (i, blk))
    tile = pltpu.VMEM((tb, D_MODEL), F32)
    sspec = pl.BlockSpec((None, tb, HG_HEADS, HG_DK, HG_DV), lambda i: (l, i, 0, 0, 0))
    return _sample_call(
        _hgrn_sample_kernel, "hgrn_sample", bsz,
        [
            pl.BlockSpec((tb, D_MODEL), col(C_Q)), pl.BlockSpec((tb, D_MODEL), col(C_F)),
            pl.BlockSpec((tb, D_MODEL), col(C_I)), pl.BlockSpec((tb, D_MODEL), col(C_GH)),
            pl.BlockSpec((None, 1, D_MODEL), lambda i: (l, 0, 0)),
            pl.BlockSpec((None, 1, 128), lambda i: (l, 0, 0)),
            sspec,
        ],
        (proj, proj, proj, proj, lb, nw, state), sspec, state.shape, [tile, tile, tile, tile], new_state)


def _rw_lora(wdad, w2a2):
    lo = lax.broadcasted_iota(jnp.int32, wdad.shape, 1) < RW_LORA
    lora_w = _mm3(jnp.where(lo, jnp.tanh(wdad), 0.0), w2a2)
    lora_a = _mm3(jnp.where(lo, 0.0, wdad), w2a2)
    return lora_w, lora_a


def _rw_prep_all(items, ones_bd):
    rs, ks, vs, lws, a_sigs, kks = [], [], [], [], [], []
    for r, k0, v, lora_w, lora_a, w0, a0, kk_w, ka_w in items:
        lws.append(-RW_DECAY_SCALE * _sigmoid(w0 + lora_w))
        a_sig = _sigmoid(a0 + lora_a)
        a_sigs.append(a_sig)
        kks.append(k0 * kk_w)
        rs.append(r)
        vs.append(v)
        ks.append(k0 * (1.0 + (a_sig - 1.0) * ka_w))
    sq = _seg_many(ones_bd, [kk * kk for kk in kks])
    kks = [kk / jnp.maximum(jnp.sqrt(s), 1e-12) for kk, s in zip(kks, sq)]
    return rs, ks, vs, lws, [-kk for kk in kks], [kk * a for kk, a in zip(kks, a_sigs)]


def _rw_posts(os_, rs, ks, vs, gs, rk_ws, lnws, lnbs, ones_bd):
    idx = range(len(os_))
    sums = _seg_many(ones_bd, list(os_) + [rs[i] * ks[i] * rk_ws[i] for i in idx])
    ds = [os_[i] - sums[i] * (1.0 / RW_N) for i in idx]
    bonus = sums[len(os_):]
    var = [s * (1.0 / RW_N) for s in _seg_many(ones_bd, [d * d for d in ds])]
    return [(ds[i] * lax.rsqrt(var[i] + RW_LN_EPS) * lnws[i] + lnbs[i] + bonus[i] * vs[i]) * _silu(gs[i])
            for i in idx]


def _rw_chunks(rs, ks, vs, lws, avs, bvs, sbds, tril):
    n = RW_CHUNK
    pairs = range(len(rs))
    lane = lax.broadcasted_iota(jnp.int32, (n, 128), 1)
    row = lax.broadcasted_iota(jnp.int32, (n, 128), 0)
    m0 = lane < RW_N
    col = jnp.where(m0, lane, lane - RW_N)
    strict = col < row
    incl = col <= row
    eye = (col == row).astype(F32)

    def bd(x):
        xb = x.astype(BF16)
        zero = jnp.zeros_like(xb)
        return jnp.concatenate([jnp.where(m0, xb, zero), jnp.where(m0, zero, xb)], axis=0)

    cls = [_mm_exact(tril, lws[q]) for q in pairs]
    a_h = [avs[q] * jnp.exp(cls[q] - lws[q]) for q in pairs]
    r_h = [rs[q] * jnp.exp(cls[q]) for q in pairs]
    b_c = [bvs[q] * jnp.exp(-cls[q]) for q in pairs]
    k_c = [ks[q] * jnp.exp(-cls[q]) for q in pairs]
    lhs = [jnp.concatenate([a_h[q], r_h[q]], axis=0) for q in pairs]
    from_state = [_mm(lhs[q], sbds[q], NT) for q in pairs]
    m_bk = [_mm(lhs[q], jnp.concatenate([bd(b_c[q]), bd(k_c[q])], axis=0), NT) for q in pairs]
    m_ab = [jnp.where(strict, m[:n, :128], 0.0) for m in m_bk]
    m_ak = [jnp.where(strict, m[:n, 128:], 0.0) for m in m_bk]
    m_rb = [jnp.where(incl, m[n:, :128], 0.0) for m in m_bk]
    m_rk = [jnp.where(incl, m[n:, 128:], 0.0) for m in m_bk]
    ps = [_mm(m, bd(m)) for m in m_ab]
    xs = [eye + m for m in m_ab]
    for level in range(1, 5):
        both = [_mm(jnp.concatenate([ps[q], xs[q]], axis=0), bd(ps[q])) for q in pairs]
        ps = [m[:n] for m in both]
        xs = [xs[q] + both[q][n:] for q in pairs]
    xs = [xs[q] + _mm(xs[q], bd(ps[q])) for q in pairs]
    from_v = [_mm(jnp.concatenate([m_ak[q], m_rk[q]], axis=0), bd(vs[q])) for q in pairs]
    w_all = [from_state[q][:n] + from_v[q][:n] for q in pairs]
    u_all = [_mm(xs[q], bd(w_all[q])) for q in pairs]
    o_all = [from_state[q][n:] + from_v[q][n:] + _mm(m_rb[q], bd(u_all[q])) for q in pairs]

    vi = lax.broadcasted_iota(jnp.int32, (128, 128), 0)
    ki = lax.broadcasted_iota(jnp.int32, (128, 128), 1)
    same_head = (vi < RW_N) == (ki < RW_N)
    new = []
    for q in pairs:
        cl_end = cls[q][n - 1:n, :]
        e_end = jnp.exp(cl_end - cls[q])
        upd = _mm(jnp.concatenate([u_all[q], vs[q]], axis=0),
                  jnp.concatenate([bvs[q] * e_end, ks[q] * e_end], axis=0), TN)
        new.append(sbds[q] * jnp.exp(cl_end) + jnp.where(same_head, upd, 0.0))
    return o_all, new


def _rwkv_prompt_kernel(r_ref, k_ref, v_ref, g_ref, wdad_ref, mu_ref, w0_ref, a0_ref, kkw_ref, kaw_ref, rkw_ref,
                        lnw_ref, lnb_ref, w2a2_ref, tril_ref, ones_ref, o_ref, s_ref,
                        sbd_ref, pr_ref, pk_ref, pv_ref, pwd_ref, lw_ref, la_ref):
    c = pl.program_id(1)
    n = RW_CHUNK

    @pl.when(c == 0)
    def _():
        sbd_ref[...] = jnp.zeros_like(sbd_ref)
        pr_ref[...] = jnp.zeros_like(pr_ref)
        pk_ref[...] = jnp.zeros_like(pk_ref)
        pv_ref[...] = jnp.zeros_like(pv_ref)
        pwd_ref[...] = jnp.zeros_like(pwd_ref)

    row = lax.broadcasted_iota(jnp.int32, (n, 128), 0)

    def shift(z_ref, prev_ref, sl, mu_lo):
        z = z_ref[:, sl]
        zp = jnp.where(row == 0, prev_ref[:, sl], pltpu.roll(z, 1, axis=0))
        prev_ref[:, sl] = z[n - 1:n, :]
        return z + (zp - z) * mu_ref[:, mu_lo:mu_lo + 128]

    wdad = shift(wdad_ref, pwd_ref, slice(0, 128), 3 * D_MODEL)
    lora_w, lora_a = _rw_lora(wdad, w2a2_ref[...])
    lw_ref[...] = lora_w
    la_ref[...] = lora_a
    ones_bd = ones_ref[...]
    pairs = [slice(q * 128, (q + 1) * 128) for q in range(RW_HEADS // 2)]
    rs, ks, vs, lws, avs, bvs = _rw_prep_all(
        [(shift(r_ref, pr_ref, sl, q * 128), shift(k_ref, pk_ref, sl, D_MODEL + q * 128),
          shift(v_ref, pv_ref, sl, 2 * D_MODEL + q * 128), lw_ref[:, sl], la_ref[:, sl],
          w0_ref[:, sl], a0_ref[:, sl], kkw_ref[:, sl], kaw_ref[:, sl]) for q, sl in enumerate(pairs)], ones_bd)
    o_all, sbd_new = _rw_chunks(rs, ks, vs, lws, avs, bvs, [sbd_ref[q] for q in range(len(pairs))], tril_ref[...])
    for q in range(len(pairs)):
        sbd_ref[q] = sbd_new[q]
    outs = _rw_posts(o_all, rs, ks, vs, [g_ref[:, sl] for sl in pairs], [rkw_ref[:, sl] for sl in pairs],
                     [lnw_ref[:, sl] for sl in pairs], [lnb_ref[:, sl] for sl in pairs], ones_bd)
    for sl, out in zip(pairs, outs):
        o_ref[:, sl] = out.astype(o_ref.dtype)

    @pl.when(c == pl.num_programs(1) - 1)
    def _():
        s_ref[0] = sbd_ref[...]


def _rw_consts():
    tril = jnp.asarray(np.tril(np.ones((RW_CHUNK, RW_CHUNK))), BF16)
    ones_bd = jnp.asarray(np.kron(np.eye(2), np.ones((RW_N, RW_N))), BF16)
    return tril, ones_bd


def _rwkv_prompt(proj, p, bsz, t, l):
    n = RW_CHUNK
    n_t = t // n
    tril, ones_bd = _rw_consts()
    col = lambda blk: (lambda b, c: (b * n_t + c, blk))
    const = lambda shape: pl.BlockSpec(shape, lambda b, c: (0, 0))
    layer = lambda shape: pl.BlockSpec((None,) + shape, lambda b, c: (l, 0, 0))
    vec = layer((1, D_MODEL))
    prev = pltpu.VMEM((1, D_MODEL), F32)
    tile = pltpu.VMEM((n, D_MODEL), F32)
    o, sbd = pl.pallas_call(
        _rwkv_prompt_kernel,
        grid=(bsz, n_t),
        in_specs=[
            pl.BlockSpec((n, D_MODEL), col(C_R)), pl.BlockSpec((n, D_MODEL), col(C_K)),
            pl.BlockSpec((n, D_MODEL), col(C_V)), pl.BlockSpec((n, D_MODEL), col(C_GR)),
            pl.BlockSpec((n, 128), lambda b, c: (b * n_t + c, C_WDAD)),
            layer((1, RW_SHIFT)),
            vec, vec, vec, vec, vec, vec, vec,
            layer((128, D_MODEL)), const((n, n)), const((128, 128)),
        ],
        out_specs=[
            pl.BlockSpec((n, D_MODEL), lambda b, c: (b * n_t + c, 0)),
            pl.BlockSpec((1, RW_HEADS // 2, 128, 128), lambda b, c: (b, 0, 0, 0)),
        ],
        out_shape=[
            jax.ShapeDtypeStruct((bsz * t, D_MODEL), BF16),
            jax.ShapeDtypeStruct((bsz, RW_HEADS // 2, 128, 128), F32),
        ],
        scratch_shapes=[pltpu.VMEM((RW_HEADS // 2, 128, 128), F32), prev, prev, prev, pltpu.VMEM((1, 128), F32),
                        tile, tile],
        compiler_params=_params(("parallel", "arbitrary")),
        name="rwkv_prompt",
    )(proj, proj, proj, proj, proj, p["mu"], p["w0"], p["a0"], p["kk"], p["ka"], p["rk"], p["lnw"], p["lnb"],
      p["w2a2"], tril, ones_bd)
    s6 = sbd.reshape(bsz, RW_HEADS // 2, 2, RW_N, 2, RW_N)
    state = jnp.stack([s6[:, :, 0, :, 0, :], s6[:, :, 1, :, 1, :]], axis=2)
    return o, state.reshape(bsz, RW_HEADS, RW_N, RW_N)


def _rwkv_sample_kernel(r_ref, k_ref, v_ref, g_ref, wdad_ref, sh_ref, mu_ref, w0_ref, a0_ref, kkw_ref, kaw_ref,
                        rkw_ref, lnw_ref, lnb_ref, w2a2_ref, ones_ref, s_ref, o_ref, so_ref,
                        rs_ref, ks_ref, vs_ref, ob_ref, kr_ref, wr_ref, ar_ref, br_ref):
    ones_bd = ones_ref[...]
    tb = SAMPLE_TB

    def store_rows(ref, sl, x):
        for i in range(tb):
            ref[i, :, sl] = x[i:i + 1, :]

    def shift(z, lo, width):
        return z + (sh_ref[:, lo:lo + width] - z) * mu_ref[:, lo:lo + width]

    wdad = shift(wdad_ref[...], 3 * D_MODEL, 128)
    lora_w, lora_a = _rw_lora(wdad, w2a2_ref[...])
    pairs = [slice(q * 128, (q + 1) * 128) for q in range(RW_HEADS // 2)]
    rs, ks, vs, lws, avs, bvs = _rw_prep_all(
        [(shift(r_ref[:, sl], q * 128, 128), shift(k_ref[:, sl], D_MODEL + q * 128, 128),
          shift(v_ref[:, sl], 2 * D_MODEL + q * 128, 128), lora_w[:, sl], lora_a[:, sl], w0_ref[:, sl],
          a0_ref[:, sl], kkw_ref[:, sl], kaw_ref[:, sl]) for q, sl in enumerate(pairs)], ones_bd)
    for q, sl in enumerate(pairs):
        rs_ref[:, sl] = rs[q]
        ks_ref[:, sl] = ks[q]
        vs_ref[:, sl] = vs[q]
        store_rows(kr_ref, sl, ks[q])
        store_rows(wr_ref, sl, jnp.exp(lws[q]))
        store_rows(ar_ref, sl, avs[q])
        store_rows(br_ref, sl, bvs[q])
    ob_ref[...] = jnp.zeros_like(ob_ref)
    rows = lax.broadcasted_iota(jnp.int32, (tb, 1), 0)

    hs = range(RW_HEADS)
    lanes = [slice(h * RW_N, (h + 1) * RW_N) for h in hs]
    unroll = 2

    def per_seqs(i, carry):
        bs = [i * unroll + j for j in range(unroll)]
        work = [(b, rows == b, h) for b in bs for h in hs]

        def row_of(ref, b, h):
            pair = slice((h // 2) * 128, (h // 2 + 1) * 128)
            return ref[b, :, pair][:, (h % 2) * RW_N:(h % 2 + 1) * RW_N]

        vcol = [_col_bcast(jnp.where(sel, vs_ref[:, lanes[h]], 0.0), RW_N) for b, sel, h in work]
        s_old = [s_ref[b, h] for b, sel, h in work]
        sa = [jnp.sum(s * row_of(ar_ref, b, h), axis=-1, keepdims=True) for s, (b, sel, h) in zip(s_old, work)]
        s_new = [s_old[n] * row_of(wr_ref, b, h) + sa[n] * row_of(br_ref, b, h) + vcol[n] * row_of(kr_ref, b, h)
                 for n, (b, sel, h) in enumerate(work)]
        for s, (b, sel, h) in zip(s_new, work):
            so_ref[b, h] = s
        outs = [_mm(jnp.where(sel, rs_ref[:, lanes[h]], 0.0), s, NT) for s, (b, sel, h) in zip(s_new, work)]
        for h in hs:
            ob_ref[:, lanes[h]] += sum(outs[j * RW_HEADS + h] for j in range(unroll))
        return carry

    lax.fori_loop(0, tb // unroll, per_seqs, 0)
    outs = _rw_posts([ob_ref[:, sl] for sl in pairs], rs, ks, vs, [g_ref[:, sl] for sl in pairs],
                     [rkw_ref[:, sl] for sl in pairs], [lnw_ref[:, sl] for sl in pairs],
                     [lnb_ref[:, sl] for sl in pairs], ones_bd)
    for sl, out in zip(pairs, outs):
        o_ref[:, sl] = out


def _rwkv_sample(proj, p, shift_state, state, l, new_state):
    bsz = state.shape[1]
    tb = SAMPLE_TB
    _, ones_bd = _rw_consts()
    col = lambda blk: (lambda i: (i, blk))
    layer = lambda shape: pl.BlockSpec((None,) + shape, lambda i: (l, 0, 0))
    vec = layer((1, D_MODEL))
    tile = pltpu.VMEM((tb, D_MODEL), F32)
    sspec = pl.BlockSpec((None, tb, RW_HEADS, RW_N, RW_N), lambda i: (l, i, 0, 0, 0))
    return _sample_call(
        _rwkv_sample_kernel, "rwkv_sample", bsz,
        [
            pl.BlockSpec((tb, D_MODEL), col(C_R)), pl.BlockSpec((tb, D_MODEL), col(C_K)),
            pl.BlockSpec((tb, D_MODEL), col(C_V)), pl.BlockSpec((tb, D_MODEL), col(C_GR)),
            pl.BlockSpec((tb, 128), lambda i: (i, C_WDAD)),
            pl.BlockSpec((None, tb, RW_SHIFT), lambda i: (l, i, 0)),
            layer((1, RW_SHIFT)),
            vec, vec, vec, vec, vec, vec, vec,
            layer((128, D_MODEL)),
            pl.BlockSpec((128, 128), lambda i: (0, 0)),
            sspec,
        ],
        (proj, proj, proj, proj, proj, shift_state, p["mu"], p["w0"], p["a0"], p["kk"], p["ka"], p["rk"],
         p["lnw"], p["lnb"], p["w2a2"], ones_bd, state),
        sspec, state.shape, [tile] * 4 + [pltpu.VMEM((tb, 1, D_MODEL), F32)] * 4, new_state)


def _mb_post(y, xs, z, d_rep, nw):
    y = (y + d_rep * xs) * _silu(z)
    ms = jnp.mean(y * y, axis=-1, keepdims=True)
    return y * lax.rsqrt(ms + RMS_EPS) * nw


def _ssd_prompt_kernel(x_ref, bc_ref, z_ref, dt_ref, cw_ref, cb_ref, dtb_ref, alog_ref, d_ref, nw_ref, tril_ref,
                       o_ref, s_ref, h_ref, px_ref, pbc_ref):
    c = pl.program_id(1)
    n = MB_CHUNK

    @pl.when(c == 0)
    def _():
        h_ref[...] = jnp.zeros_like(h_ref)
        px_ref[...] = jnp.zeros_like(px_ref)
        pbc_ref[...] = jnp.zeros_like(pbc_ref)

    def conv(u_ref, prev_ref, sl, wsl):
        u = u_ref[:, sl]
        prev = prev_ref[:, sl]
        w = cw_ref[:, wsl]
        row = lax.broadcasted_iota(jnp.int32, (8, u.shape[1]), 0)
        acc = u * w[MB_CONV - 1:MB_CONV, :] + cb_ref[:, wsl]
        for s in range(1, MB_CONV):
            us = pltpu.roll(u, s, axis=0)
            top = jnp.where(row < s, pltpu.roll(prev, s, axis=0), us[0:8])
            us = jnp.concatenate([top, us[8:]], axis=0)
            acc = acc + us * w[MB_CONV - 1 - s:MB_CONV - s, :]
        prev_ref[:, sl] = u[n - 8:n, :]
        return _silu(acc)

    tril = tril_ref[...]
    ti = lax.broadcasted_iota(jnp.int32, (n, n), 0)
    si = lax.broadcasted_iota(jnp.int32, (n, n), 1)
    incl = si <= ti
    lane = lax.broadcasted_iota(jnp.int32, (n, 4 * MB_P), 1)
    sum_cum = jnp.concatenate([jnp.ones((n, MB_N), BF16), (ti <= si).astype(BF16)], axis=1)
    nbc = MB_GROUPS * MB_N
    groups = range(MB_GROUPS)
    gx = [slice(g * 256, (g + 1) * 256) for g in groups]

    xs = [conv(x_ref, px_ref, gx[g], gx[g]) for g in groups]
    bm = [conv(bc_ref, pbc_ref, slice(g * MB_N, (g + 1) * MB_N),
               slice(D_MODEL + g * MB_N, D_MODEL + (g + 1) * MB_N)) for g in groups]
    cm = [conv(bc_ref, pbc_ref, slice(nbc + g * MB_N, nbc + (g + 1) * MB_N),
               slice(D_MODEL + nbc + g * MB_N, D_MODEL + nbc + (g + 1) * MB_N)) for g in groups]
    dt = [_softplus(dt_ref[:, gx[g]] + dtb_ref[:, gx[g]]) for g in groups]
    a = [dt[g] * (-jnp.exp(alog_ref[:, gx[g]])) for g in groups]
    xdt = [xs[g] * dt[g] for g in groups]
    cs = [_mm_exact(tril, a[g]) for g in groups]
    tr = [_mm_exact(sum_cum, a[g], TN, x_is_lhs=True) for g in groups]
    gmat = [_mm(cm[g], bm[g], NT) for g in groups]
    h_all = [h_ref[g] for g in groups]
    y = [jnp.exp(cs[g]) * _mm(cm[g], h_all[g], NT) for g in groups]
    for g in groups:
        cs_t = tr[g][:, MB_N:]
        for e in range(4):
            hd = slice(e * MB_P, (e + 1) * MB_P)
            lmat = jnp.where(incl, jnp.exp(cs[g][:, hd] - cs_t[hd, :]), 0.0)
            me = (lane >= e * MB_P) & (lane < (e + 1) * MB_P)
            y[g] = y[g] + _mm(gmat[g] * lmat, jnp.where(me, xdt[g], 0.0))
    for g in groups:
        cs_end = cs[g][n - 1:n, :]
        h_new = h_all[g] * jnp.exp(tr[g][:, :MB_N]) + _mm(xdt[g] * jnp.exp(cs_end - cs[g]), bm[g], TN)
        h_ref[g] = h_new
    for g in groups:
        o_ref[:, gx[g]] = _mb_post(y[g], xs[g], z_ref[:, gx[g]], d_ref[:, gx[g]],
                                   nw_ref[:, gx[g]]).astype(o_ref.dtype)

    @pl.when(c == pl.num_programs(1) - 1)
    def _():
        for g in range(MB_GROUPS):
            for e in range(4):
                s_ref[0, 4 * g + e] = h_ref[g, e * MB_P:(e + 1) * MB_P, :]


def _ssd_prompt(proj, p, bsz, t, l):
    n = MB_CHUNK
    n_t = t // n
    tril = jnp.asarray(np.tril(np.ones((n, n))), BF16)
    col = lambda blk: (lambda b, c: (b * n_t + c, blk))
    const = lambda shape: pl.BlockSpec(shape, lambda b, c: (0, 0))
    layer = lambda shape: pl.BlockSpec((None,) + shape, lambda b, c: (l, 0, 0))
    vec = layer((1, D_MODEL))
    return pl.pallas_call(
        _ssd_prompt_kernel,
        grid=(bsz, n_t),
        in_specs=[
            pl.BlockSpec((n, D_MODEL), col(C_X)), pl.BlockSpec((n, D_MODEL), col(C_BC)),
            pl.BlockSpec((n, D_MODEL), col(C_Z)), pl.BlockSpec((n, D_MODEL), col(C_DT)),
            layer((MB_CONV, MB_XBC)), layer((1, MB_XBC)),
            vec, vec, vec, vec,
            const((n, n)),
        ],
        out_specs=[
            pl.BlockSpec((n, D_MODEL), lambda b, c: (b * n_t + c, 0)),
            pl.BlockSpec((1, MB_HEADS, MB_P, MB_N), lambda b, c: (b, 0, 0, 0)),
        ],
        out_shape=[
            jax.ShapeDtypeStruct((bsz * t, D_MODEL), BF16),
            jax.ShapeDtypeStruct((bsz, MB_HEADS, MB_P, MB_N), F32),
        ],
        scratch_shapes=[pltpu.VMEM((MB_GROUPS, 4 * MB_P, MB_N), F32), pltpu.VMEM((8, D_MODEL), F32),
                        pltpu.VMEM((8, D_MODEL), F32)],
        compiler_params=_params(("parallel", "arbitrary")),
        name="ssd_prompt",
    )(proj, proj, proj, proj, p["cw"], p["cb"], p["dtb"], p["alog"], p["d"], p["nw"], tril)


def _ssd_sample_kernel(xbc_ref, z_ref, dt_ref, cs_ref, cw_ref, cb_ref, dtb_ref, alog_ref, d_ref, nw_ref, s_ref,
                       o_ref, so_ref, xs_ref, xd_ref, bs_ref, cms_ref, dec_ref, ob_ref):
    tb = SAMPLE_TB
    w = cw_ref[...]
    acc = xbc_ref[...] * w[MB_CONV - 1:MB_CONV, :] + cb_ref[...]
    for i in range(MB_CONV - 1):
        acc = acc + cs_ref[i] * w[i:i + 1, :]
    xbc = _silu(acc)
    xs = xbc[:, :D_MODEL]
    dt = _softplus(dt_ref[...] + dtb_ref[...])
    xs_ref[...] = xs
    xd_ref[...] = xs * dt
    bs_ref[...] = xbc[:, D_MODEL:D_MODEL + MB_GROUPS * MB_N]
    cms_ref[...] = xbc[:, D_MODEL + MB_GROUPS * MB_N:]
    dec = jnp.exp(dt * (-jnp.exp(alog_ref[...])))
    for i in range(tb):
        dec_ref[i] = dec[i:i + 1, :]
    ob_ref[...] = jnp.zeros_like(ob_ref)
    rows = lax.broadcasted_iota(jnp.int32, (tb, 1), 0)

    def per_seq(b, carry):
        sel = rows == b
        hs = range(MB_HEADS)
        lanes = [slice(h * MB_P, (h + 1) * MB_P) for h in hs]
        grp = [slice((h // 4) * MB_N, (h // 4 + 1) * MB_N) for h in hs]

        def dec_of(h):
            pair = slice((h // 2) * 128, (h // 2 + 1) * 128)
            return dec_ref[b, :, pair][:, (h % 2) * MB_P:(h % 2) * MB_P + 1]

        upd = [_mm(jnp.where(sel, xd_ref[:, lanes[h]], 0.0), bs_ref[:, grp[h]], TN) for h in hs]
        h_new = [dec_of(h) * s_ref[b, h] + upd[h] for h in hs]
        for h in hs:
            so_ref[b, h] = h_new[h]
        outs = [_mm(jnp.where(sel, cms_ref[:, grp[h]], 0.0), h_new[h], NT) for h in hs]
        for h in hs:
            ob_ref[:, lanes[h]] += outs[h]
        return carry

    lax.fori_loop(0, tb, per_seq, 0)
    for g in range(MB_GROUPS):
        sl = slice(g * 256, (g + 1) * 256)
        o_ref[:, sl] = _mb_post(ob_ref[:, sl], xs_ref[:, sl], z_ref[:, sl], d_ref[:, sl], nw_ref[:, sl])


def _ssd_sample(proj, p, conv_state, state, l, new_state):
    bsz = state.shape[1]
    tb = SAMPLE_TB
    layer = lambda shape: pl.BlockSpec((None,) + shape, lambda i: (l, 0, 0))
    vec = layer((1, D_MODEL))
    tile = pltpu.VMEM((tb, D_MODEL), F32)
    half = pltpu.VMEM((tb, MB_GROUPS * MB_N), F32)
    sspec = pl.BlockSpec((None, tb, MB_HEADS, MB_P, MB_N), lambda i: (l, i, 0, 0, 0))
    return _sample_call(
        _ssd_sample_kernel, "ssd_sample", bsz,
        [
            pl.BlockSpec((tb, MB_XBC), lambda i: (i, C_X // 2)),
            pl.BlockSpec((tb, D_MODEL), lambda i: (i, C_Z)),
            pl.BlockSpec((tb, D_MODEL), lambda i: (i, C_DT)),
            pl.BlockSpec((None, MB_CONV - 1, tb, MB_XBC), lambda i: (l, 0, i, 0)),
            layer((MB_CONV, MB_XBC)), layer((1, MB_XBC)),
            vec, vec, vec, vec,
            sspec,
        ],
        (proj, proj, proj, conv_state, p["cw"], p["cb"], p["dtb"], p["alog"], p["d"], p["nw"], state),
        sspec, state.shape, [tile, tile, half, half, pltpu.VMEM((tb, 1, D_MODEL), F32), tile], new_state)


def _prep_w_in(w_in):
    o = np.cumsum([0, 1024, 1024, 1024, 1024, RW_SHIFT, 1024, 1024, MB_XBC, MB_HEADS, 3 * D_MODEL])
    zrw, grw, zmb, xbc, dtc, gate = o[4], o[5], o[6], o[7], o[8], o[9]
    parts = [
        w_in[..., :zrw], w_in[..., zrw:zrw + 3 * D_MODEL], w_in[..., grw:zmb], w_in[..., xbc:dtc],
        w_in[..., zmb:xbc], w_in[..., gate:],
        jnp.repeat(w_in[..., dtc:gate], MB_P, axis=-1),
        w_in[..., zrw + 3 * D_MODEL:grw],
    ]
    return jnp.concatenate(parts, axis=-1).astype(BF16)


def _rep(v):
    return jnp.repeat(v, MB_P, axis=-1)[:, None, :]


def kernel(x_prompt, x_sample, state_hgrn, state_rwkv, state_rwkv_shift, state_ssm, state_conv, norm_w, w_in, hg_lb, hg_norm_w, rw_mu, rw_w0, rw_w2, rw_a0, rw_a2, rw_k_k, rw_k_a, rw_r_k, rw_ln_w, rw_ln_b, mb_conv_w, mb_conv_b, mb_dt_bias, mb_A_log, mb_D, mb_norm_w, w_o_hg, w_o_rw, w_o_mb, w_out, final_norm_w):
    bp, t, _ = x_prompt.shape
    bs = x_sample.shape[0]
    sm = jax.nn.softmax(hg_lb.astype(F32), axis=0)
    row = lambda a: a[:, None, :]
    lbs = row(jnp.cumsum(sm, axis=0) - sm[0:1])
    w_in_r = _prep_w_in(w_in)
    rw_p = dict(mu=row(rw_mu), w0=row(rw_w0), a0=row(rw_a0), kk=row(rw_k_k), ka=row(rw_k_a),
                rk=rw_r_k.reshape(DEPTH, 1, D_MODEL), lnw=row(rw_ln_w), lnb=row(rw_ln_b),
                w2a2=jnp.concatenate([rw_w2, rw_a2], axis=1))
    mb_p = dict(cw=mb_conv_w, cb=row(mb_conv_b), dtb=_rep(mb_dt_bias), alog=_rep(mb_A_log), d=_rep(mb_D),
                nw=row(mb_norm_w))
    nw_in, nw_hg = row(norm_w), row(hg_norm_w)
    whg, wrw, wmb, wout = (w.astype(BF16) for w in (w_o_hg, w_o_rw, w_o_mb, w_out))
    fw = final_norm_w[None, :]
    conv_t = jnp.swapaxes(state_conv, 1, 2)

    hp = x_prompt.reshape(bp * t, D_MODEL)
    hs = x_sample.reshape(bs, D_MODEL)
    p_states, s_small = [], []
    s_hg = s_rw = s_ssm = None
    yp = ys = None
    for l in range(DEPTH):
        final = l == DEPTH - 1

        proj = _inproj(hp, nw_in, w_in_r, l)
        o_hg, p_hg = _hgrn_prompt(proj, lbs, nw_hg, bp, t, l)
        o_rw, p_rw = _rwkv_prompt(proj, rw_p, bp, t, l)
        o_mb, p_ssm = _ssd_prompt(proj, mb_p, bp, t, l)
        proj3 = proj.reshape(bp, t, NCOL)
        p_shift = jnp.concatenate([proj3[:, -1, C_R * 1024:C_GR * 1024], proj3[:, -1, C_WDAD * 128:]], axis=-1)
        p_conv = proj3[:, t - (MB_CONV - 1):, C_X * 1024:C_Z * 1024]
        res = _merge(o_hg, o_rw, o_mb, proj, hp, whg, wrw, wmb, wout, fw, final, l)
        hp = res[0]
        if final:
            yp = res[1]
        p_states.append((p_hg, p_rw, p_shift, p_ssm, p_conv))

        proj = _inproj(hs, nw_in, w_in_r, l)
        o_hg, s_hg = _hgrn_sample(proj, lbs, nw_hg, state_hgrn, l, s_hg)
        o_rw, s_rw = _rwkv_sample(proj, rw_p, state_rwkv_shift, state_rwkv, l, s_rw)
        o_mb, s_ssm = _ssd_sample(proj, mb_p, conv_t, state_ssm, l, s_ssm)
        s_shift = jnp.concatenate([proj[:, C_R * 1024:C_GR * 1024], proj[:, C_WDAD * 128:]], axis=-1)
        s_conv = jnp.concatenate([state_conv[l][:, 1:], proj[:, None, C_X * 1024:C_Z * 1024]], axis=1)
        res = _merge(o_hg, o_rw, o_mb, proj, hs, whg, wrw, wmb, wout, fw, final, l)
        hs = res[0]
        if final:
            ys = res[1]
        s_small.append((s_shift, s_conv))

    stack = lambda states, i: jnp.stack([s[i] for s in states])
    return (yp.reshape(bp, t, D_MODEL), ys.reshape(bs, 1, D_MODEL),
            *[stack(p_states, i) for i in range(5)],
            s_hg, s_rw, stack(s_small, 0), s_ssm, stack(s_small, 1))
```

```python
import functools

import numpy as np
import jax
import jax.numpy as jnp
from jax import lax
from jax.experimental import pallas as pl
from jax.experimental.pallas import tpu as pltpu

F32 = jnp.float32
BF16 = jnp.bfloat16
HI = lax.Precision.HIGHEST

D_MODEL = 1024
DEPTH = 2
HG_HEADS, HG_DK, HG_DV = 8, 128, 128
HG_F_MIN = 1e-20
LOG2E = 1.4426950408889634
RW_HEADS, RW_N = 16, 64
RW_LORA = 64
RW_SHIFT = 3 * D_MODEL + 2 * RW_LORA
RW_LN_EPS = 64e-5
RW_DECAY_SCALE = float(np.exp(-0.5))
MB_HEADS, MB_P, MB_GROUPS, MB_N, MB_CONV = 16, 64, 4, 128, 4
MB_XBC = D_MODEL + 2 * MB_GROUPS * MB_N
RMS_EPS = 1e-6

C_Q, C_F, C_I, C_GH, C_R, C_K, C_V, C_GR, C_X, C_BC, C_Z, C_G0, C_G1, C_G2, C_DT = range(15)
C_WDAD = 15 * 8
NCOL = 15 * 1024 + 128

HG_BLK = 16
HG_NB = 2
RW_CHUNK = 64
RW_NB = 2
MB_CHUNK = 64
SAMPLE_TB = 8

VMEM_LIMIT = 48 * 1024 * 1024

NN = (((1,), (0,)), ((), ()))
NT = (((1,), (1,)), ((), ()))
TN = (((0,), (0,)), ((), ()))


def _mm(a, b, dims=NN, hi=False):
    if hi:
        return lax.dot_general(a.astype(F32), b.astype(F32), dims, precision=HI, preferred_element_type=F32)
    return lax.dot_general(a.astype(BF16), b.astype(BF16), dims, preferred_element_type=F32)


def _mm_exact(e, x, dims=NN, x_is_lhs=False, passes=3):
    e = e.astype(BF16)
    acc = None
    for _ in range(passes):
        p = x.astype(BF16)
        x = x - p.astype(F32)
        t = (lax.dot_general(p, e, dims, preferred_element_type=F32) if x_is_lhs
             else lax.dot_general(e, p, dims, preferred_element_type=F32))
        acc = t if acc is None else acc + t
    return acc


def _mm3(a, b, dims=NN):
    a_hi = a.astype(BF16)
    a_lo = (a - a_hi.astype(F32)).astype(BF16)
    b_hi = b.astype(BF16)
    b_lo = (b - b_hi.astype(F32)).astype(BF16)
    dot = lambda x, y: lax.dot_general(x, y, dims, preferred_element_type=F32)
    return dot(a_hi, b_hi) + (dot(a_hi, b_lo) + dot(a_lo, b_hi))


def _seg_many(ones_bd, xs):
    m = xs[0].shape[0]
    out = _mm_exact(ones_bd, jnp.concatenate(xs, axis=0) if len(xs) > 1 else xs[0], x_is_lhs=True, passes=2)
    return [out[i * m:(i + 1) * m] for i in range(len(xs))]


def _sigmoid(x):
    return jax.nn.sigmoid(x)


def _silu(x):
    return x * jax.nn.sigmoid(x)


def _softplus(x):
    return jnp.maximum(x, 0.0) + jnp.log1p(jnp.exp(-jnp.abs(x)))


def _col_bcast(rows, width):
    ones = jnp.ones((rows.shape[0], width), BF16)
    p1 = rows.astype(BF16)
    r1 = rows - p1.astype(F32)
    p2 = r1.astype(BF16)
    p3 = (r1 - p2.astype(F32)).astype(BF16)
    out = lax.dot_general(p1, ones, TN, preferred_element_type=F32)
    out = out + lax.dot_general(p2, ones, TN, preferred_element_type=F32)
    return out + lax.dot_general(p3, ones, TN, preferred_element_type=F32)


def _params(sem):
    return pltpu.CompilerParams(dimension_semantics=sem, vmem_limit_bytes=VMEM_LIMIT)


def _inproj_kernel(x_ref, nw_ref, w_ref, o_ref, xn_ref):
    @pl.when(pl.program_id(1) == 0)
    def _():
        x = x_ref[...]
        ms = jnp.mean(x * x, axis=-1, keepdims=True)
        xn_ref[...] = (x * lax.rsqrt(ms + RMS_EPS) * nw_ref[...]).astype(BF16)

    o_ref[...] = jnp.dot(xn_ref[...], w_ref[...], preferred_element_type=F32)


def _inproj(x2d, nw, w, l):
    m = x2d.shape[0]
    tm = min(1024, m)
    tn = 1408
    return pl.pallas_call(
        _inproj_kernel,
        grid=(m // tm, NCOL // tn),
        in_specs=[
            pl.BlockSpec((tm, D_MODEL), lambda i, j: (i, 0)),
            pl.BlockSpec((None, 1, D_MODEL), lambda i, j: (l, 0, 0)),
            pl.BlockSpec((None, D_MODEL, tn), lambda i, j: (l, 0, j)),
        ],
        out_specs=pl.BlockSpec((tm, tn), lambda i, j: (i, j)),
        out_shape=jax.ShapeDtypeStruct((m, NCOL), F32),
        scratch_shapes=[pltpu.VMEM((tm, D_MODEL), BF16)],
        compiler_params=_params(("parallel", "arbitrary")),
        name="inproj",
    )(x2d, nw, w)


def _merge_kernel(ohg, orw, omb, g0, g1, g2, h_ref, whg, wrw, wmb, wout, fw_ref, hn_ref, *y_ref):
    u = _sigmoid(g0[...]) * _mm(ohg[...], whg[...])
    u = u + _sigmoid(g1[...]) * _mm(orw[...], wrw[...])
    u = u + _sigmoid(g2[...]) * _mm(omb[...], wmb[...])
    hn = h_ref[...] + _mm(u, wout[...])
    hn_ref[...] = hn
    if y_ref:
        ms = jnp.mean(hn * hn, axis=-1, keepdims=True)
        y_ref[0][...] = hn * lax.rsqrt(ms + RMS_EPS) * fw_ref[...]


def _merge(ohg, orw, omb, proj, h2d, whg, wrw, wmb, wout, fw, final, l):
    m = h2d.shape[0]
    tm = min(256, m)
    row = lambda i: (i, 0)
    const = lambda i: (0, 0)
    wspec = pl.BlockSpec((None, D_MODEL, D_MODEL), lambda i: (l, 0, 0))
    out_shape = [jax.ShapeDtypeStruct((m, D_MODEL), F32)]
    out_specs = [pl.BlockSpec((tm, D_MODEL), row)]
    if final:
        out_shape.append(jax.ShapeDtypeStruct((m, D_MODEL), F32))
        out_specs.append(pl.BlockSpec((tm, D_MODEL), row))
    return pl.pallas_call(
        _merge_kernel,
        grid=(m // tm,),
        in_specs=[
            pl.BlockSpec((tm, D_MODEL), row), pl.BlockSpec((tm, D_MODEL), row), pl.BlockSpec((tm, D_MODEL), row),
            pl.BlockSpec((tm, D_MODEL), lambda i: (i, C_G0)),
            pl.BlockSpec((tm, D_MODEL), lambda i: (i, C_G1)),
            pl.BlockSpec((tm, D_MODEL), lambda i: (i, C_G2)),
            pl.BlockSpec((tm, D_MODEL), row),
            wspec, wspec, wspec, wspec,
            pl.BlockSpec((1, D_MODEL), const),
        ],
        out_specs=out_specs,
        out_shape=out_shape,
        compiler_params=_params(("parallel",)),
        name="merge",
    )(ohg, orw, omb, proj, proj, proj, h2d, whg, wrw, wmb, wout, fw)


def _sample_call(body, name, bsz, in_specs, args, state_spec, state_shape, scratch, new_state):
    tb = SAMPLE_TB
    aliases = {}
    if new_state is not None:
        in_specs = in_specs + [pl.BlockSpec(memory_space=pl.ANY)]
        args = args + (new_state,)
        aliases = {len(args) - 1: 1}
        inner = body
        body = lambda *refs: inner(*refs[:len(args) - 1], *refs[len(args):])
    return pl.pallas_call(
        body,
        grid=(bsz // tb,),
        in_specs=in_specs,
        out_specs=[pl.BlockSpec((tb, D_MODEL), lambda i: (i, 0)), state_spec],
        out_shape=[jax.ShapeDtypeStruct((bsz, D_MODEL), F32), jax.ShapeDtypeStruct(state_shape, F32)],
        scratch_shapes=scratch,
        input_output_aliases=aliases,
        compiler_params=_params(("parallel",)),
        name=name,
    )(*args)


def _hgrn_prep(q, f, lb):
    sig = _sigmoid(f)
    fg = lb + (1.0 - lb) * sig
    logf = jnp.log(jnp.maximum(fg, HG_F_MIN))
    kx = (1.0 - lb) * (1.0 - sig)
    qx = _silu(q) * (HG_DK ** -0.5)
    return qx, kx, logf


def _hgrn_post(o, g, nw):
    ms = jnp.mean(o * o, axis=-1, keepdims=True)
    return o * lax.rsqrt(ms + RMS_EPS) * nw * _silu(g)


def _hgrn_prompt_kernel(q_ref, f_ref, i_ref, g_ref, lb_ref, nw_ref, tril_ref, o_ref, s_ref,
                        st_ref, qs_ref, ks_ref, bs_ref, ob_ref, *, nblk):
    c = pl.program_id(1)
    nb = q_ref.shape[0]

    @pl.when(c == 0)
    def _():
        st_ref[...] = jnp.zeros_like(st_ref)

    tril = tril_ref[...]
    heads = [(i, h, slice(h * 128, (h + 1) * 128)) for i in range(nb) for h in range(HG_HEADS)]
    for i, h, sl in heads:
        qx, kx, logf = _hgrn_prep(q_ref[i, :, sl], f_ref[i, :, sl], lb_ref[:, sl])
        qs_ref[i, :, sl] = qx
        ks_ref[i, :, sl] = kx
        bs_ref[i, :, sl] = _mm_exact(tril, logf) * LOG2E
    half = HG_BLK // 2
    lane8 = lax.broadcasted_iota(jnp.int32, (half, HG_DK), 1)
    top16 = lax.broadcasted_iota(jnp.int32, (HG_BLK, HG_DK), 0) < half
    lower = (lax.broadcasted_iota(jnp.int32, (HG_BLK, HG_BLK), 1)
             <= lax.broadcasted_iota(jnp.int32, (HG_BLK, HG_BLK), 0))

    def intra_weights(j):
        r0 = j * HG_BLK if isinstance(j, int) else pl.multiple_of(j * HG_BLK, HG_BLK)
        top, bot, blk = pl.ds(r0, half), pl.ds(r0 + half, half), pl.ds(r0, HG_BLK)
        amat = []
        for i, h, sl in heads:
            q_t, q_b, b_t, b_b = qs_ref[i, top, sl], qs_ref[i, bot, sl], bs_ref[i, top, sl], bs_ref[i, bot, sl]
            k16, b16 = ks_ref[i, blk, sl], bs_ref[i, blk, sl]
            a_top = jnp.zeros((half, HG_DK), F32)
            a_bot = jnp.zeros((half, HG_DK), F32)
            for jj in range(half):
                s = jnp.sum(q_t * jnp.exp2(b_t - b16[jj:jj + 1]) * k16[jj:jj + 1], axis=-1, keepdims=True)
                a_top = jnp.where(lane8 == jj, s, a_top)
                jb = half + jj
                s = jnp.sum(q_b * jnp.exp2(b_b - b16[jb:jb + 1]) * k16[jb:jb + 1], axis=-1, keepdims=True)
                a_bot = jnp.where(lane8 == jb, s, a_bot)
            b_mid = b16[half - 1:half]
            k_dec = jnp.where(top16, k16 * jnp.exp2(b_mid - b16), 0.0)
            cross = _mm(q_b * jnp.exp2(b_b - b_mid), k_dec, NT)
            full = jnp.concatenate([a_top[:, :HG_BLK], a_bot[:, :HG_BLK] + cross], axis=0)
            amat.append(jnp.where(lower, full, 0.0))
        return tuple(amat)

    def apply_block(j, amat):
        rows = pl.ds(pl.multiple_of(j * HG_BLK, HG_BLK), HG_BLK)
        idx = range(len(heads))
        qb = [qs_ref[i, rows, sl] for i, h, sl in heads]
        kb = [ks_ref[i, rows, sl] for i, h, sl in heads]
        bb = [bs_ref[i, rows, sl] for i, h, sl in heads]
        vb = [i_ref[i, rows, sl] for i, h, sl in heads]
        st = [st_ref[i, h] for i, h, sl in heads]
        intra = [_mm(amat[n], vb[n]) for n in idx]
        inter = [_mm(qb[n] * jnp.exp2(bb[n]), st[n], NT) for n in idx]
        for n, (i, h, sl) in enumerate(heads):
            b_end = bb[n][HG_BLK - 1:HG_BLK, :]
            st_ref[i, h] = st[n] * jnp.exp2(b_end) + _mm(vb[n], kb[n] * jnp.exp2(b_end - bb[n]), TN)
        for n, (i, h, sl) in enumerate(heads):
            ob_ref[i, rows, sl] = intra[n] + inter[n]

    def step(j, amats):
        apply_block(j - 1, amats)
        return intra_weights(j)

    apply_block(nblk - 1, lax.fori_loop(1, nblk, step, intra_weights(0)))
    nw = nw_ref[...]
    for i, h, sl in heads:
        o_ref[i, :, sl] = _hgrn_post(ob_ref[i, :, sl], g_ref[i, :, sl], nw).astype(o_ref.dtype)

    @pl.when(c == pl.num_programs(1) - 1)
    def _():
        for i, h, sl in heads:
            s_ref[i, h] = st_ref[i, h].T


def _hgrn_prompt(proj, lb, nw, bsz, t, l):
    tt = min(256, t)
    n_t = t // tt
    nb = HG_NB if bsz % HG_NB == 0 else 1
    tril = jnp.asarray(np.kron(np.eye(tt // HG_BLK), np.tril(np.ones((HG_BLK, HG_BLK)))), BF16)
    proj3 = proj.reshape(bsz, t, NCOL)
    col = lambda blk: (lambda b, c: (b, c, blk))
    const = lambda shape: pl.BlockSpec(shape, lambda b, c: (0, 0))
    layer = lambda shape: pl.BlockSpec((None,) + shape, lambda b, c: (l, 0, 0))
    tile = pltpu.VMEM((nb, tt, D_MODEL), F32)
    o, s = pl.pallas_call(
        functools.partial(_hgrn_prompt_kernel, nblk=tt // HG_BLK),
        grid=(bsz // nb, n_t),
        in_specs=[
            pl.BlockSpec((nb, tt, D_MODEL), col(C_Q)), pl.BlockSpec((nb, tt, D_MODEL), col(C_F)),
            pl.BlockSpec((nb, tt, D_MODEL), col(C_I)), pl.BlockSpec((nb, tt, D_MODEL), col(C_GH)),
            layer((1, D_MODEL)), layer((1, 128)), const((tt, tt)),
        ],
        out_specs=[
            pl.BlockSpec((nb, tt, D_MODEL), lambda b, c: (b, c, 0)),
            pl.BlockSpec((nb, HG_HEADS, HG_DK, HG_DV), lambda b, c: (b, 0, 0, 0)),
        ],
        out_shape=[
            jax.ShapeDtypeStruct((bsz, t, D_MODEL), BF16),
            jax.ShapeDtypeStruct((bsz, HG_HEADS, HG_DK, HG_DV), F32),
        ],
        scratch_shapes=[pltpu.VMEM((nb, HG_HEADS, HG_DV, HG_DK), F32), tile, tile, tile, tile],
        compiler_params=_params(("parallel", "arbitrary")),
        name="hgrn_prompt",
    )(proj3, proj3, proj3, proj3, lb, nw, tril)
    return o.reshape(bsz * t, D_MODEL), s


def _hgrn_sample_kernel(q_ref, f_ref, i_ref, g_ref, lb_ref, nw_ref, s_ref, o_ref, so_ref,
                        qs_ref, ks_ref, fs_ref, ob_ref):
    qx, kx, logf = _hgrn_prep(q_ref[...], f_ref[...], lb_ref[...])
    qs_ref[...] = qx
    ks_ref[...] = kx
    fs_ref[...] = jnp.exp(logf)
    ob_ref[...] = jnp.zeros_like(ob_ref)
    rows = lax.broadcasted_iota(jnp.int32, (SAMPLE_TB, 1), 0)

    heads = [slice(h * 128, (h + 1) * 128) for h in range(HG_HEADS)]

    unroll = 2

    def per_seqs(i, carry):
        work = [(i * unroll + j, rows == i * unroll + j, h, sl) for j in range(unroll)
                for h, sl in enumerate(heads)]
        fcol = [_col_bcast(jnp.where(sel, fs_ref[:, sl], 0.0), HG_DV) for b, sel, h, sl in work]
        kv = [_mm(jnp.where(sel, ks_ref[:, sl], 0.0), i_ref[:, sl], TN) for b, sel, h, sl in work]
        s_new = [fcol[n] * s_ref[b, h] + kv[n] for n, (b, sel, h, sl) in enumerate(work)]
        for s, (b, sel, h, sl) in zip(s_new, work):
            so_ref[b, h] = s
        outs = [_mm(jnp.where(sel, qs_ref[:, sl], 0.0), s) for s, (b, sel, h, sl) in zip(s_new, work)]
        for h, sl in enumerate(heads):
            ob_ref[:, sl] += sum(outs[j * HG_HEADS + h] for j in range(unroll))
        return carry

    lax.fori_loop(0, SAMPLE_TB // unroll, per_seqs, 0)
    nw = nw_ref[...]
    for h in range(HG_HEADS):
        sl = slice(h * 128, (h + 1) * 128)
        o_ref[:, sl] = _hgrn_post(ob_ref[:, sl], g_ref[:, sl], nw)


def _hgrn_sample(proj, lb, nw, state, l, new_state):
    bsz = state.shape[1]
    tb = SAMPLE_TB
    col = lambda blk: (lambda i: (i, blk))
    tile = pltpu.VMEM((tb, D_MODEL), F32)
    sspec = pl.BlockSpec((None, tb, HG_HEADS, HG_DK, HG_DV), lambda i: (l, i, 0, 0, 0))
    return _sample_call(
        _hgrn_sample_kernel, "hgrn_sample", bsz,
        [
            pl.BlockSpec((tb, D_MODEL), col(C_Q)), pl.BlockSpec((tb, D_MODEL), col(C_F)),
            pl.BlockSpec((tb, D_MODEL), col(C_I)), pl.BlockSpec((tb, D_MODEL), col(C_GH)),
            pl.BlockSpec((None, 1, D_MODEL), lambda i: (l, 0, 0)),
            pl.BlockSpec((None, 1, 128), lambda i: (l, 0, 0)),
            sspec,
        ],
        (proj, proj, proj, proj, lb, nw, state), sspec, state.shape, [tile, tile, tile, tile], new_state)


def _rw_lora(wdad, w2a2):
    lo = lax.broadcasted_iota(jnp.int32, wdad.shape, 1) < RW_LORA
    lora_w = _mm3(jnp.where(lo, jnp.tanh(wdad), 0.0), w2a2)
    lora_a = _mm3(jnp.where(lo, 0.0, wdad), w2a2)
    return lora_w, lora_a


def _rw_prep_all(items, ones_bd):
    rs, ks, vs, lws, a_sigs, kks = [], [], [], [], [], []
    for r, k0, v, lora_w, lora_a, w0, a0, kk_w, ka_w in items:
        lws.append(-RW_DECAY_SCALE * _sigmoid(w0 + lora_w))
        a_sig = _sigmoid(a0 + lora_a)
        a_sigs.append(a_sig)
        kks.append(k0 * kk_w)
        rs.append(r)
        vs.append(v)
        ks.append(k0 * (1.0 + (a_sig - 1.0) * ka_w))
    sq = _seg_many(ones_bd, [kk * kk for kk in kks])
    kks = [kk / jnp.maximum(jnp.sqrt(s), 1e-12) for kk, s in zip(kks, sq)]
    return rs, ks, vs, lws, [-kk for kk in kks], [kk * a for kk, a in zip(kks, a_sigs)]


def _rw_posts(os_, rs, ks, vs, gs, rk_ws, lnws, lnbs, ones_bd):
    idx = range(len(os_))
    sums = _seg_many(ones_bd, list(os_) + [rs[i] * ks[i] * rk_ws[i] for i in idx])
    ds = [os_[i] - sums[i] * (1.0 / RW_N) for i in idx]
    bonus = sums[len(os_):]
    var = [s * (1.0 / RW_N) for s in _seg_many(ones_bd, [d * d for d in ds])]
    return [(ds[i] * lax.rsqrt(var[i] + RW_LN_EPS) * lnws[i] + lnbs[i] + bonus[i] * vs[i]) * _silu(gs[i])
            for i in idx]


def _rw_chunks(rs, ks, vs, lws, avs, bvs, sbds, tril):
    n = RW_CHUNK
    pairs = range(len(rs))
    lane = lax.broadcasted_iota(jnp.int32, (n, 128), 1)
    row = lax.broadcasted_iota(jnp.int32, (n, 128), 0)
    m0 = lane < RW_N
    col = jnp.where(m0, lane, lane - RW_N)
    strict = col < row
    incl = col <= row
    eye = (col == row).astype(F32)

    def bd(x):
        xb = x.astype(BF16)
        zero = jnp.zeros_like(xb)
        return jnp.concatenate([jnp.where(m0, xb, zero), jnp.where(m0, zero, xb)], axis=0)

    cls = [_mm_exact(tril, lws[q]) for q in pairs]
    a_h = [avs[q] * jnp.exp(cls[q] - lws[q]) for q in pairs]
    r_h = [rs[q] * jnp.exp(cls[q]) for q in pairs]
    b_c = [bvs[q] * jnp.exp(-cls[q]) for q in pairs]
    k_c = [ks[q] * jnp.exp(-cls[q]) for q in pairs]
    lhs = [jnp.concatenate([a_h[q], r_h[q]], axis=0) for q in pairs]
    from_state = [_mm(lhs[q], sbds[q], NT) for q in pairs]
    m_bk = [_mm(lhs[q], jnp.concatenate([bd(b_c[q]), bd(k_c[q])], axis=0), NT) for q in pairs]
    m_ab = [jnp.where(strict, m[:n, :128], 0.0) for m in m_bk]
    m_ak = [jnp.where(strict, m[:n, 128:], 0.0) for m in m_bk]
    m_rb = [jnp.where(incl, m[n:, :128], 0.0) for m in m_bk]
    m_rk = [jnp.where(incl, m[n:, 128:], 0.0) for m in m_bk]
    ps = [_mm(m, bd(m)) for m in m_ab]
    xs = [eye + m for m in m_ab]
    for level in range(1, 5):
        both = [_mm(jnp.concatenate([ps[q], xs[q]], axis=0), bd(ps[q])) for q in pairs]
        ps = [m[:n] for m in both]
        xs = [xs[q] + both[q][n:] for q in pairs]
    xs = [xs[q] + _mm(xs[q], bd(ps[q])) for q in pairs]
    from_v = [_mm(jnp.concatenate([m_ak[q], m_rk[q]], axis=0), bd(vs[q])) for q in pairs]
    w_all = [from_state[q][:n] + from_v[q][:n] for q in pairs]
    u_all = [_mm(xs[q], bd(w_all[q])) for q in pairs]
    o_all = [from_state[q][n:] + from_v[q][n:] + _mm(m_rb[q], bd(u_all[q])) for q in pairs]

    vi = lax.broadcasted_iota(jnp.int32, (128, 128), 0)
    ki = lax.broadcasted_iota(jnp.int32, (128, 128), 1)
    same_head = (vi < RW_N) == (ki < RW_N)
    new = []
    for q in pairs:
        cl_end = cls[q][n - 1:n, :]
        e_end = jnp.exp(cl_end - cls[q])
        upd = _mm(jnp.concatenate([u_all[q], vs[q]], axis=0),
                  jnp.concatenate([bvs[q] * e_end, ks[q] * e_end], axis=0), TN)
        new.append(sbds[q] * jnp.exp(cl_end) + jnp.where(same_head, upd, 0.0))
    return o_all, new


def _rwkv_prompt_kernel(r_ref, k_ref, v_ref, g_ref, wdad_ref, mu_ref, w0_ref, a0_ref, kkw_ref, kaw_ref, rkw_ref,
                        lnw_ref, lnb_ref, w2a2_ref, tril_ref, ones_ref, o_ref, s_ref,
                        sbd_ref, pr_ref, pk_ref, pv_ref, pwd_ref, lw_ref, la_ref):
    c = pl.program_id(1)
    n = RW_CHUNK
    nb = r_ref.shape[0]

    @pl.when(c == 0)
    def _():
        sbd_ref[...] = jnp.zeros_like(sbd_ref)
        pr_ref[...] = jnp.zeros_like(pr_ref)
        pk_ref[...] = jnp.zeros_like(pk_ref)
        pv_ref[...] = jnp.zeros_like(pv_ref)
        pwd_ref[...] = jnp.zeros_like(pwd_ref)

    row = lax.broadcasted_iota(jnp.int32, (n, 128), 0)

    def shift(z_ref, prev_ref, i, sl, mu_lo):
        z = z_ref[i, :, sl]
        zp = jnp.where(row == 0, prev_ref[i, :, sl], pltpu.roll(z, 1, axis=0))
        prev_ref[i, :, sl] = z[n - 1:n, :]
        return z + (zp - z) * mu_ref[:, mu_lo:mu_lo + 128]

    for i in range(nb):
        lora_w, lora_a = _rw_lora(shift(wdad_ref, pwd_ref, i, slice(0, 128), 3 * D_MODEL), w2a2_ref[...])
        lw_ref[i] = lora_w
        la_ref[i] = lora_a
    ones_bd = ones_ref[...]
    lanes = [slice(q * 128, (q + 1) * 128) for q in range(RW_HEADS // 2)]
    pairs = [(i, q, sl) for i in range(nb) for q, sl in enumerate(lanes)]
    rs, ks, vs, lws, avs, bvs = _rw_prep_all(
        [(shift(r_ref, pr_ref, i, sl, q * 128), shift(k_ref, pk_ref, i, sl, D_MODEL + q * 128),
          shift(v_ref, pv_ref, i, sl, 2 * D_MODEL + q * 128), lw_ref[i, :, sl], la_ref[i, :, sl],
          w0_ref[:, sl], a0_ref[:, sl], kkw_ref[:, sl], kaw_ref[:, sl]) for i, q, sl in pairs], ones_bd)
    o_all, sbd_new = _rw_chunks(rs, ks, vs, lws, avs, bvs, [sbd_ref[i, q] for i, q, sl in pairs], tril_ref[...])
    for (i, q, sl), new in zip(pairs, sbd_new):
        sbd_ref[i, q] = new
    outs = _rw_posts(o_all, rs, ks, vs, [g_ref[i, :, sl] for i, q, sl in pairs],
                     [rkw_ref[:, sl] for i, q, sl in pairs], [lnw_ref[:, sl] for i, q, sl in pairs],
                     [lnb_ref[:, sl] for i, q, sl in pairs], ones_bd)
    for (i, q, sl), out in zip(pairs, outs):
        o_ref[i, :, sl] = out.astype(o_ref.dtype)

    @pl.when(c == pl.num_programs(1) - 1)
    def _():
        s_ref[...] = sbd_ref[...]


def _rw_consts():
    tril = jnp.asarray(np.tril(np.ones((RW_CHUNK, RW_CHUNK))), BF16)
    ones_bd = jnp.asarray(np.kron(np.eye(2), np.ones((RW_N, RW_N))), BF16)
    return tril, ones_bd


def _rwkv_prompt(proj, p, bsz, t, l):
    n = RW_CHUNK
    n_t = t // n
    nb = RW_NB if bsz % RW_NB == 0 else 1
    tril, ones_bd = _rw_consts()
    proj3 = proj.reshape(bsz, t, NCOL)
    col = lambda blk: (lambda b, c: (b, c, blk))
    const = lambda shape: pl.BlockSpec(shape, lambda b, c: (0, 0))
    layer = lambda shape: pl.BlockSpec((None,) + shape, lambda b, c: (l, 0, 0))
    vec = layer((1, D_MODEL))
    prev = pltpu.VMEM((nb, 1, D_MODEL), F32)
    tile = pltpu.VMEM((nb, n, D_MODEL), F32)
    o, sbd = pl.pallas_call(
        _rwkv_prompt_kernel,
        grid=(bsz // nb, n_t),
        in_specs=[
            pl.BlockSpec((nb, n, D_MODEL), col(C_R)), pl.BlockSpec((nb, n, D_MODEL), col(C_K)),
            pl.BlockSpec((nb, n, D_MODEL), col(C_V)), pl.BlockSpec((nb, n, D_MODEL), col(C_GR)),
            pl.BlockSpec((nb, n, 128), lambda b, c: (b, c, C_WDAD)),
            layer((1, RW_SHIFT)),
            vec, vec, vec, vec, vec, vec, vec,
            layer((128, D_MODEL)), const((n, n)), const((128, 128)),
        ],
        out_specs=[
            pl.BlockSpec((nb, n, D_MODEL), lambda b, c: (b, c, 0)),
            pl.BlockSpec((nb, RW_HEADS // 2, 128, 128), lambda b, c: (b, 0, 0, 0)),
        ],
        out_shape=[
            jax.ShapeDtypeStruct((bsz, t, D_MODEL), BF16),
            jax.ShapeDtypeStruct((bsz, RW_HEADS // 2, 128, 128), F32),
        ],
        scratch_shapes=[pltpu.VMEM((nb, RW_HEADS // 2, 128, 128), F32), prev, prev, prev,
                        pltpu.VMEM((nb, 1, 128), F32), tile, tile],
        compiler_params=_params(("parallel", "arbitrary")),
        name="rwkv_prompt",
    )(proj3, proj3, proj3, proj3, proj3, p["mu"], p["w0"], p["a0"], p["kk"], p["ka"], p["rk"], p["lnw"], p["lnb"],
      p["w2a2"], tril, ones_bd)
    s6 = sbd.reshape(bsz, RW_HEADS // 2, 2, RW_N, 2, RW_N)
    state = jnp.stack([s6[:, :, 0, :, 0, :], s6[:, :, 1, :, 1, :]], axis=2)
    return o.reshape(bsz * t, D_MODEL), state.reshape(bsz, RW_HEADS, RW_N, RW_N)


def _rwkv_sample_kernel(r_ref, k_ref, v_ref, g_ref, wdad_ref, sh_ref, mu_ref, w0_ref, a0_ref, kkw_ref, kaw_ref,
                        rkw_ref, lnw_ref, lnb_ref, w2a2_ref, ones_ref, s_ref, o_ref, so_ref,
                        rs_ref, ks_ref, vs_ref, ob_ref, kr_ref, wr_ref, ar_ref, br_ref):
    ones_bd = ones_ref[...]
    tb = SAMPLE_TB

    def store_rows(ref, sl, x):
        for i in range(tb):
            ref[i, :, sl] = x[i:i + 1, :]

    def shift(z, lo, width):
        return z + (sh_ref[:, lo:lo + width] - z) * mu_ref[:, lo:lo + width]

    wdad = shift(wdad_ref[...], 3 * D_MODEL, 128)
    lora_w, lora_a = _rw_lora(wdad, w2a2_ref[...])
    pairs = [slice(q * 128, (q + 1) * 128) for q in range(RW_HEADS // 2)]
    rs, ks, vs, lws, avs, bvs = _rw_prep_all(
        [(shift(r_ref[:, sl], q * 128, 128), shift(k_ref[:, sl], D_MODEL + q * 128, 128),
          shift(v_ref[:, sl], 2 * D_MODEL + q * 128, 128), lora_w[:, sl], lora_a[:, sl], w0_ref[:, sl],
          a0_ref[:, sl], kkw_ref[:, sl], kaw_ref[:, sl]) for q, sl in enumerate(pairs)], ones_bd)
    for q, sl in enumerate(pairs):
        rs_ref[:, sl] = rs[q]
        ks_ref[:, sl] = ks[q]
        vs_ref[:, sl] = vs[q]
        store_rows(kr_ref, sl, ks[q])
        store_rows(wr_ref, sl, jnp.exp(lws[q]))
        store_rows(ar_ref, sl, avs[q])
        store_rows(br_ref, sl, bvs[q])
    ob_ref[...] = jnp.zeros_like(ob_ref)
    rows = lax.broadcasted_iota(jnp.int32, (tb, 1), 0)

    hs = range(RW_HEADS)
    lanes = [slice(h * RW_N, (h + 1) * RW_N) for h in hs]
    unroll = 2

    def per_seqs(i, carry):
        bs = [i * unroll + j for j in range(unroll)]
        work = [(b, rows == b, h) for b in bs for h in hs]

        def row_of(ref, b, h):
            pair = slice((h // 2) * 128, (h // 2 + 1) * 128)
            return ref[b, :, pair][:, (h % 2) * RW_N:(h % 2 + 1) * RW_N]

        vcol = [_col_bcast(jnp.where(sel, vs_ref[:, lanes[h]], 0.0), RW_N) for b, sel, h in work]
        s_old = [s_ref[b, h] for b, sel, h in work]
        sa = [jnp.sum(s * row_of(ar_ref, b, h), axis=-1, keepdims=True) for s, (b, sel, h) in zip(s_old, work)]
        s_new = [s_old[n] * row_of(wr_ref, b, h) + sa[n] * row_of(br_ref, b, h) + vcol[n] * row_of(kr_ref, b, h)
                 for n, (b, sel, h) in enumerate(work)]
        for s, (b, sel, h) in zip(s_new, work):
            so_ref[b, h] = s
        outs = [_mm(jnp.where(sel, rs_ref[:, lanes[h]], 0.0), s, NT) for s, (b, sel, h) in zip(s_new, work)]
        for h in hs:
            ob_ref[:, lanes[h]] += sum(outs[j * RW_HEADS + h] for j in range(unroll))
        return carry

    lax.fori_loop(0, tb // unroll, per_seqs, 0)
    outs = _rw_posts([ob_ref[:, sl] for sl in pairs], rs, ks, vs, [g_ref[:, sl] for sl in pairs],
                     [rkw_ref[:, sl] for sl in pairs], [lnw_ref[:, sl] for sl in pairs],
                     [lnb_ref[:, sl] for sl in pairs], ones_bd)
    for sl, out in zip(pairs, outs):
        o_ref[:, sl] = out


def _rwkv_sample(proj, p, shift_state, state, l, new_state):
    bsz = state.shape[1]
    tb = SAMPLE_TB
    _, ones_bd = _rw_consts()
    col = lambda blk: (lambda i: (i, blk))
    layer = lambda shape: pl.BlockSpec((None,) + shape, lambda i: (l, 0, 0))
    vec = layer((1, D_MODEL))
    tile = pltpu.VMEM((tb, D_MODEL), F32)
    sspec = pl.BlockSpec((None, tb, RW_HEADS, RW_N, RW_N), lambda i: (l, i, 0, 0, 0))
    return _sample_call(
        _rwkv_sample_kernel, "rwkv_sample", bsz,
        [
            pl.BlockSpec((tb, D_MODEL), col(C_R)), pl.BlockSpec((tb, D_MODEL), col(C_K)),
            pl.BlockSpec((tb, D_MODEL), col(C_V)), pl.BlockSpec((tb, D_MODEL), col(C_GR)),
            pl.BlockSpec((tb, 128), lambda i: (i, C_WDAD)),
            pl.BlockSpec((None, tb, RW_SHIFT), lambda i: (l, i, 0)),
            layer((1, RW_SHIFT)),
            vec, vec, vec, vec, vec, vec, vec,
            layer((128, D_MODEL)),
            pl.BlockSpec((128, 128), lambda i: (0, 0)),
            sspec,
        ],
        (proj, proj, proj, proj, proj, shift_state, p["mu"], p["w0"], p["a0"], p["kk"], p["ka"], p["rk"],
         p["lnw"], p["lnb"], p["w2a2"], ones_bd, state),
        sspec, state.shape, [tile] * 4 + [pltpu.VMEM((tb, 1, D_MODEL), F32)] * 4, new_state)


def _mb_post(y, xs, z, d_rep, nw):
    y = (y + d_rep * xs) * _silu(z)
    ms = jnp.mean(y * y, axis=-1, keepdims=True)
    return y * lax.rsqrt(ms + RMS_EPS) * nw


def _ssd_prompt_kernel(x_ref, bc_ref, z_ref, dt_ref, cw_ref, cb_ref, dtb_ref, alog_ref, d_ref, nw_ref, tril_ref,
                       o_ref, s_ref, h_ref, px_ref, pbc_ref):
    c = pl.program_id(1)
    n = MB_CHUNK

    @pl.when(c == 0)
    def _():
        h_ref[...] = jnp.zeros_like(h_ref)
        px_ref[...] = jnp.zeros_like(px_ref)
        pbc_ref[...] = jnp.zeros_like(pbc_ref)

    def conv(u_ref, prev_ref, sl, wsl):
        u = u_ref[:, sl]
        prev = prev_ref[:, sl]
        w = cw_ref[:, wsl]
        row = lax.broadcasted_iota(jnp.int32, (8, u.shape[1]), 0)
        acc = u * w[MB_CONV - 1:MB_CONV, :] + cb_ref[:, wsl]
        for s in range(1, MB_CONV):
            us = pltpu.roll(u, s, axis=0)
            top = jnp.where(row < s, pltpu.roll(prev, s, axis=0), us[0:8])
            us = jnp.concatenate([top, us[8:]], axis=0)
            acc = acc + us * w[MB_CONV - 1 - s:MB_CONV - s, :]
        prev_ref[:, sl] = u[n - 8:n, :]
        return _silu(acc)

    tril = tril_ref[...]
    ti = lax.broadcasted_iota(jnp.int32, (n, n), 0)
    si = lax.broadcasted_iota(jnp.int32, (n, n), 1)
    incl = si <= ti
    lane = lax.broadcasted_iota(jnp.int32, (n, 4 * MB_P), 1)
    sum_cum = jnp.concatenate([jnp.ones((n, MB_N), BF16), (ti <= si).astype(BF16)], axis=1)
    nbc = MB_GROUPS * MB_N
    groups = range(MB_GROUPS)
    gx = [slice(g * 256, (g + 1) * 256) for g in groups]

    xs = [conv(x_ref, px_ref, gx[g], gx[g]) for g in groups]
    bm = [conv(bc_ref, pbc_ref, slice(g * MB_N, (g + 1) * MB_N),
               slice(D_MODEL + g * MB_N, D_MODEL + (g + 1) * MB_N)) for g in groups]
    cm = [conv(bc_ref, pbc_ref, slice(nbc + g * MB_N, nbc + (g + 1) * MB_N),
               slice(D_MODEL + nbc + g * MB_N, D_MODEL + nbc + (g + 1) * MB_N)) for g in groups]
    dt = [_softplus(dt_ref[:, gx[g]] + dtb_ref[:, gx[g]]) for g in groups]
    a = [dt[g] * (-jnp.exp(alog_ref[:, gx[g]])) for g in groups]
    xdt = [xs[g] * dt[g] for g in groups]
    cs = [_mm_exact(tril, a[g]) for g in groups]
    tr = [_mm_exact(sum_cum, a[g], TN, x_is_lhs=True) for g in groups]
    gmat = [_mm(cm[g], bm[g], NT) for g in groups]
    h_all = [h_ref[g] for g in groups]
    y = [jnp.exp(cs[g]) * _mm(cm[g], h_all[g], NT) for g in groups]
    for g in groups:
        cs_t = tr[g][:, MB_N:]
        for e in range(4):
            hd = slice(e * MB_P, (e + 1) * MB_P)
            lmat = jnp.where(incl, jnp.exp(cs[g][:, hd] - cs_t[hd, :]), 0.0)
            me = (lane >= e * MB_P) & (lane < (e + 1) * MB_P)
            y[g] = y[g] + _mm(gmat[g] * lmat, jnp.where(me, xdt[g], 0.0))
    for g in groups:
        cs_end = cs[g][n - 1:n, :]
        h_new = h_all[g] * jnp.exp(tr[g][:, :MB_N]) + _mm(xdt[g] * jnp.exp(cs_end - cs[g]), bm[g], TN)
        h_ref[g] = h_new
    for g in groups:
        o_ref[:, gx[g]] = _mb_post(y[g], xs[g], z_ref[:, gx[g]], d_ref[:, gx[g]],
                                   nw_ref[:, gx[g]]).astype(o_ref.dtype)

    @pl.when(c == pl.num_programs(1) - 1)
    def _():
        for g in range(MB_GROUPS):
            for e in range(4):
                s_ref[0, 4 * g + e] = h_ref[g, e * MB_P:(e + 1) * MB_P, :]


def _ssd_prompt(proj, p, bsz, t, l):
    n = MB_CHUNK
    n_t = t // n
    tril = jnp.asarray(np.tril(np.ones((n, n))), BF16)
    col = lambda blk: (lambda b, c: (b * n_t + c, blk))
    const = lambda shape: pl.BlockSpec(shape, lambda b, c: (0, 0))
    layer = lambda shape: pl.BlockSpec((None,) + shape, lambda b, c: (l, 0, 0))
    vec = layer((1, D_MODEL))
    return pl.pallas_call(
        _ssd_prompt_kernel,
        grid=(bsz, n_t),
        in_specs=[
            pl.BlockSpec((n, D_MODEL), col(C_X)), pl.BlockSpec((n, D_MODEL), col(C_BC)),
            pl.BlockSpec((n, D_MODEL), col(C_Z)), pl.BlockSpec((n, D_MODEL), col(C_DT)),
            layer((MB_CONV, MB_XBC)), layer((1, MB_XBC)),
            vec, vec, vec, vec,
            const((n, n)),
        ],
        out_specs=[
            pl.BlockSpec((n, D_MODEL), lambda b, c: (b * n_t + c, 0)),
            pl.BlockSpec((1, MB_HEADS, MB_P, MB_N), lambda b, c: (b, 0, 0, 0)),
        ],
        out_shape=[
            jax.ShapeDtypeStruct((bsz * t, D_MODEL), BF16),
            jax.ShapeDtypeStruct((bsz, MB_HEADS, MB_P, MB_N), F32),
        ],
        scratch_shapes=[pltpu.VMEM((MB_GROUPS, 4 * MB_P, MB_N), F32), pltpu.VMEM((8, D_MODEL), F32),
                        pltpu.VMEM((8, D_MODEL), F32)],
        compiler_params=_params(("parallel", "arbitrary")),
        name="ssd_prompt",
    )(proj, proj, proj, proj, p["cw"], p["cb"], p["dtb"], p["alog"], p["d"], p["nw"], tril)


def _ssd_sample_kernel(xbc_ref, z_ref, dt_ref, cs_ref, cw_ref, cb_ref, dtb_ref, alog_ref, d_ref, nw_ref, s_ref,
                       o_ref, so_ref, xs_ref, xd_ref, bs_ref, cms_ref, dec_ref, ob_ref):
    tb = SAMPLE_TB
    w = cw_ref[...]
    acc = xbc_ref[...] * w[MB_CONV - 1:MB_CONV, :] + cb_ref[...]
    for i in range(MB_CONV - 1):
        acc = acc + cs_ref[i] * w[i:i + 1, :]
    xbc = _silu(acc)
    xs = xbc[:, :D_MODEL]
    dt = _softplus(dt_ref[...] + dtb_ref[...])
    xs_ref[...] = xs
    xd_ref[...] = xs * dt
    bs_ref[...] = xbc[:, D_MODEL:D_MODEL + MB_GROUPS * MB_N]
    cms_ref[...] = xbc[:, D_MODEL + MB_GROUPS * MB_N:]
    dec = jnp.exp(dt * (-jnp.exp(alog_ref[...])))
    for i in range(tb):
        dec_ref[i] = dec[i:i + 1, :]
    ob_ref[...] = jnp.zeros_like(ob_ref)
    rows = lax.broadcasted_iota(jnp.int32, (tb, 1), 0)

    hs = range(MB_HEADS)
    lanes = [slice(h * MB_P, (h + 1) * MB_P) for h in hs]
    grp = [slice((h // 4) * MB_N, (h // 4 + 1) * MB_N) for h in hs]
    unroll = 2

    def per_seqs(i, carry):
        work = [(i * unroll + j, rows == i * unroll + j, h) for j in range(unroll) for h in hs]

        def dec_of(b, h):
            pair = slice((h // 2) * 128, (h // 2 + 1) * 128)
            return dec_ref[b, :, pair][:, (h % 2) * MB_P:(h % 2) * MB_P + 1]

        upd = [_mm(jnp.where(sel, xd_ref[:, lanes[h]], 0.0), bs_ref[:, grp[h]], TN) for b, sel, h in work]
        h_new = [dec_of(b, h) * s_ref[b, h] + upd[n] for n, (b, sel, h) in enumerate(work)]
        for s, (b, sel, h) in zip(h_new, work):
            so_ref[b, h] = s
        outs = [_mm(jnp.where(sel, cms_ref[:, grp[h]], 0.0), s, NT) for s, (b, sel, h) in zip(h_new, work)]
        for h in hs:
            ob_ref[:, lanes[h]] += sum(outs[j * MB_HEADS + h] for j in range(unroll))
        return carry

    lax.fori_loop(0, tb // unroll, per_seqs, 0)
    for g in range(MB_GROUPS):
        sl = slice(g * 256, (g + 1) * 256)
        o_ref[:, sl] = _mb_post(ob_ref[:, sl], xs_ref[:, sl], z_ref[:, sl], d_ref[:, sl], nw_ref[:, sl])


def _ssd_sample(proj, p, conv_state, state, l, new_state):
    bsz = state.shape[1]
    tb = SAMPLE_TB
    layer = lambda shape: pl.BlockSpec((None,) + shape, lambda i: (l, 0, 0))
    vec = layer((1, D_MODEL))
    tile = pltpu.VMEM((tb, D_MODEL), F32)
    half = pltpu.VMEM((tb, MB_GROUPS * MB_N), F32)
    sspec = pl.BlockSpec((None, tb, MB_HEADS, MB_P, MB_N), lambda i: (l, i, 0, 0, 0))
    return _sample_call(
        _ssd_sample_kernel, "ssd_sample", bsz,
        [
            pl.BlockSpec((tb, MB_XBC), lambda i: (i, C_X // 2)),
            pl.BlockSpec((tb, D_MODEL), lambda i: (i, C_Z)),
            pl.BlockSpec((tb, D_MODEL), lambda i: (i, C_DT)),
            pl.BlockSpec((None, MB_CONV - 1, tb, MB_XBC), lambda i: (l, 0, i, 0)),
            layer((MB_CONV, MB_XBC)), layer((1, MB_XBC)),
            vec, vec, vec, vec,
            sspec,
        ],
        (proj, proj, proj, conv_state, p["cw"], p["cb"], p["dtb"], p["alog"], p["d"], p["nw"], state),
        sspec, state.shape, [tile, tile, half, half, pltpu.VMEM((tb, 1, D_MODEL), F32), tile], new_state)


def _prep_w_in(w_in):
    o = np.cumsum([0, 1024, 1024, 1024, 1024, RW_SHIFT, 1024, 1024, MB_XBC, MB_HEADS, 3 * D_MODEL])
    zrw, grw, zmb, xbc, dtc, gate = o[4], o[5], o[6], o[7], o[8], o[9]
    parts = [
        w_in[..., :zrw], w_in[..., zrw:zrw + 3 * D_MODEL], w_in[..., grw:zmb], w_in[..., xbc:dtc],
        w_in[..., zmb:xbc], w_in[..., gate:],
        jnp.repeat(w_in[..., dtc:gate], MB_P, axis=-1),
        w_in[..., zrw + 3 * D_MODEL:grw],
    ]
    return jnp.concatenate(parts, axis=-1).astype(BF16)


def _rep(v):
    return jnp.repeat(v, MB_P, axis=-1)[:, None, :]


def kernel(x_prompt, x_sample, state_hgrn, state_rwkv, state_rwkv_shift, state_ssm, state_conv, norm_w, w_in, hg_lb, hg_norm_w, rw_mu, rw_w0, rw_w2, rw_a0, rw_a2, rw_k_k, rw_k_a, rw_r_k, rw_ln_w, rw_ln_b, mb_conv_w, mb_conv_b, mb_dt_bias, mb_A_log, mb_D, mb_norm_w, w_o_hg, w_o_rw, w_o_mb, w_out, final_norm_w):
    bp, t, _ = x_prompt.shape
    bs = x_sample.shape[0]
    sm = jax.nn.softmax(hg_lb.astype(F32), axis=0)
    row = lambda a: a[:, None, :]
    lbs = row(jnp.cumsum(sm, axis=0) - sm[0:1])
    w_in_r = _prep_w_in(w_in)
    rw_p = dict(mu=row(rw_mu), w0=row(rw_w0), a0=row(rw_a0), kk=row(rw_k_k), ka=row(rw_k_a),
                rk=rw_r_k.reshape(DEPTH, 1, D_MODEL), lnw=row(rw_ln_w), lnb=row(rw_ln_b),
                w2a2=jnp.concatenate([rw_w2, rw_a2], axis=1))
    mb_p = dict(cw=mb_conv_w, cb=row(mb_conv_b), dtb=_rep(mb_dt_bias), alog=_rep(mb_A_log), d=_rep(mb_D),
                nw=row(mb_norm_w))
    nw_in, nw_hg = row(norm_w), row(hg_norm_w)
    whg, wrw, wmb, wout = (w.astype(BF16) for w in (w_o_hg, w_o_rw, w_o_mb, w_out))
    fw = final_norm_w[None, :]
    conv_t = jnp.swapaxes(state_conv, 1, 2)

    hp = x_prompt.reshape(bp * t, D_MODEL)
    hs = x_sample.reshape(bs, D_MODEL)
    p_states, s_small = [], []
    s_hg = s_rw = s_ssm = None
    yp = ys = None
    for l in range(DEPTH):
        final = l == DEPTH - 1

        proj = _inproj(hp, nw_in, w_in_r, l)
        o_hg, p_hg = _hgrn_prompt(proj, lbs, nw_hg, bp, t, l)
        o_rw, p_rw = _rwkv_prompt(proj, rw_p, bp, t, l)
        o_mb, p_ssm = _ssd_prompt(proj, mb_p, bp, t, l)
        proj3 = proj.reshape(bp, t, NCOL)
        p_shift = jnp.concatenate([proj3[:, -1, C_R * 1024:C_GR * 1024], proj3[:, -1, C_WDAD * 128:]], axis=-1)
        p_conv = proj3[:, t - (MB_CONV - 1):, C_X * 1024:C_Z * 1024]
        res = _merge(o_hg, o_rw, o_mb, proj, hp, whg, wrw, wmb, wout, fw, final, l)
        hp = res[0]
        if final:
            yp = res[1]
        p_states.append((p_hg, p_rw, p_shift, p_ssm, p_conv))

        proj = _inproj(hs, nw_in, w_in_r, l)
        o_hg, s_hg = _hgrn_sample(proj, lbs, nw_hg, state_hgrn, l, s_hg)
        o_rw, s_rw = _rwkv_sample(proj, rw_p, state_rwkv_shift, state_rwkv, l, s_rw)
        o_mb, s_ssm = _ssd_sample(proj, mb_p, conv_t, state_ssm, l, s_ssm)
        s_shift = jnp.concatenate([proj[:, C_R * 1024:C_GR * 1024], proj[:, C_WDAD * 128:]], axis=-1)
        s_conv = jnp.concatenate([state_conv[l][:, 1:], proj[:, None, C_X * 1024:C_Z * 1024]], axis=1)
        res = _merge(o_hg, o_rw, o_mb, proj, hs, whg, wrw, wmb, wout, fw, final, l)
        hs = res[0]
        if final:
            ys = res[1]
        s_small.append((s_shift, s_conv))

    stack = lambda states, i: jnp.stack([s[i] for s in states])
    return (yp.reshape(bp, t, D_MODEL), ys.reshape(bs, 1, D_MODEL),
            *[stack(p_states, i) for i in range(5)],
            s_hg, s_rw, stack(s_small, 0), s_ssm, stack(s_small, 1))
```

```python
import functools

import numpy as np
import jax
import jax.numpy as jnp
from jax import lax
from jax.experimental import pallas as pl
from jax.experimental.pallas import tpu as pltpu

F32 = jnp.float32
BF16 = jnp.bfloat16
HI = lax.Precision.HIGHEST

D_MODEL = 1024
DEPTH = 2
HG_HEADS, HG_DK, HG_DV = 8, 128, 128
HG_F_MIN = 1e-20
LOG2E = 1.4426950408889634
RW_HEADS, RW_N = 16, 64
RW_LORA = 64
RW_SHIFT = 3 * D_MODEL + 2 * RW_LORA
RW_LN_EPS = 64e-5
RW_DECAY_SCALE = float(np.exp(-0.5))
MB_HEADS, MB_P, MB_GROUPS, MB_N, MB_CONV = 16, 64, 4, 128, 4
MB_XBC = D_MODEL + 2 * MB_GROUPS * MB_N
RMS_EPS = 1e-6

C_Q, C_F, C_I, C_GH, C_R, C_K, C_V, C_GR, C_X, C_BC, C_Z, C_G0, C_G1, C_G2 = range(14)
C_WDAD = 14 * 8
C_DT = 14 * 8 + 1
NCOL = 14 * 1024 + 2 * 128

HG_BLK = 16
HG_NB = 2
RW_CHUNK = 64
RW_NB = 2
MB_CHUNK = 64
SAMPLE_TB = 8

VMEM_LIMIT = 48 * 1024 * 1024

NN = (((1,), (0,)), ((), ()))
NT = (((1,), (1,)), ((), ()))
TN = (((0,), (0,)), ((), ()))


def _mm(a, b, dims=NN, hi=False):
    if hi:
        return lax.dot_general(a.astype(F32), b.astype(F32), dims, precision=HI, preferred_element_type=F32)
    return lax.dot_general(a.astype(BF16), b.astype(BF16), dims, preferred_element_type=F32)


def _mm_exact(e, x, dims=NN, x_is_lhs=False, passes=3):
    e = e.astype(BF16)
    acc = None
    for _ in range(passes):
        p = x.astype(BF16)
        x = x - p.astype(F32)
        t = (lax.dot_general(p, e, dims, preferred_element_type=F32) if x_is_lhs
             else lax.dot_general(e, p, dims, preferred_element_type=F32))
        acc = t if acc is None else acc + t
    return acc


def _mm3(a, b, dims=NN):
    a_hi = a.astype(BF16)
    a_lo = (a - a_hi.astype(F32)).astype(BF16)
    b_hi = b.astype(BF16)
    b_lo = (b - b_hi.astype(F32)).astype(BF16)
    dot = lambda x, y: lax.dot_general(x, y, dims, preferred_element_type=F32)
    return dot(a_hi, b_hi) + (dot(a_hi, b_lo) + dot(a_lo, b_hi))


def _seg_many(ones_bd, xs):
    m = xs[0].shape[0]
    out = _mm_exact(ones_bd, jnp.concatenate(xs, axis=0) if len(xs) > 1 else xs[0], x_is_lhs=True, passes=2)
    return [out[i * m:(i + 1) * m] for i in range(len(xs))]


def _sigmoid(x):
    return jax.nn.sigmoid(x)


def _silu(x):
    return x * jax.nn.sigmoid(x)


def _softplus(x):
    return jnp.maximum(x, 0.0) + jnp.log1p(jnp.exp(-jnp.abs(x)))


def _col_bcast(rows, width):
    ones = jnp.ones((rows.shape[0], width), BF16)
    p1 = rows.astype(BF16)
    r1 = rows - p1.astype(F32)
    p2 = r1.astype(BF16)
    p3 = (r1 - p2.astype(F32)).astype(BF16)
    out = lax.dot_general(p1, ones, TN, preferred_element_type=F32)
    out = out + lax.dot_general(p2, ones, TN, preferred_element_type=F32)
    return out + lax.dot_general(p3, ones, TN, preferred_element_type=F32)


def _params(sem):
    return pltpu.CompilerParams(dimension_semantics=sem, vmem_limit_bytes=VMEM_LIMIT)


def _inproj_kernel(x_ref, nw_ref, w_ref, o_ref, xn_ref):
    @pl.when(pl.program_id(1) == 0)
    def _():
        x = x_ref[...]
        ms = jnp.mean(x * x, axis=-1, keepdims=True)
        xn_ref[...] = (x * lax.rsqrt(ms + RMS_EPS) * nw_ref[...]).astype(BF16)

    o_ref[...] = jnp.dot(xn_ref[...], w_ref[...], preferred_element_type=F32)


def _inproj(x2d, nw, w, l):
    m = x2d.shape[0]
    tm = min(1024, m)
    tn = 768
    return pl.pallas_call(
        _inproj_kernel,
        grid=(m // tm, NCOL // tn),
        in_specs=[
            pl.BlockSpec((tm, D_MODEL), lambda i, j: (i, 0)),
            pl.BlockSpec((None, 1, D_MODEL), lambda i, j: (l, 0, 0)),
            pl.BlockSpec((None, D_MODEL, tn), lambda i, j: (l, 0, j)),
        ],
        out_specs=pl.BlockSpec((tm, tn), lambda i, j: (i, j)),
        out_shape=jax.ShapeDtypeStruct((m, NCOL), F32),
        scratch_shapes=[pltpu.VMEM((tm, D_MODEL), BF16)],
        compiler_params=_params(("parallel", "arbitrary")),
        name="inproj",
    )(x2d, nw, w)


def _merge_kernel(ohg, orw, omb, g0, g1, g2, h_ref, whg, wrw, wmb, wout, fw_ref, hn_ref, *y_ref):
    u = _sigmoid(g0[...]) * _mm(ohg[...], whg[...])
    u = u + _sigmoid(g1[...]) * _mm(orw[...], wrw[...])
    u = u + _sigmoid(g2[...]) * _mm(omb[...], wmb[...])
    hn = h_ref[...] + _mm(u, wout[...])
    hn_ref[...] = hn
    if y_ref:
        ms = jnp.mean(hn * hn, axis=-1, keepdims=True)
        y_ref[0][...] = hn * lax.rsqrt(ms + RMS_EPS) * fw_ref[...]


def _merge(ohg, orw, omb, proj, h2d, whg, wrw, wmb, wout, fw, final, l):
    m = h2d.shape[0]
    tm = min(256, m)
    row = lambda i: (i, 0)
    const = lambda i: (0, 0)
    wspec = pl.BlockSpec((None, D_MODEL, D_MODEL), lambda i: (l, 0, 0))
    out_shape = [jax.ShapeDtypeStruct((m, D_MODEL), F32)]
    out_specs = [pl.BlockSpec((tm, D_MODEL), row)]
    if final:
        out_shape.append(jax.ShapeDtypeStruct((m, D_MODEL), F32))
        out_specs.append(pl.BlockSpec((tm, D_MODEL), row))
    return pl.pallas_call(
        _merge_kernel,
        grid=(m // tm,),
        in_specs=[
            pl.BlockSpec((tm, D_MODEL), row), pl.BlockSpec((tm, D_MODEL), row), pl.BlockSpec((tm, D_MODEL), row),
            pl.BlockSpec((tm, D_MODEL), lambda i: (i, C_G0)),
            pl.BlockSpec((tm, D_MODEL), lambda i: (i, C_G1)),
            pl.BlockSpec((tm, D_MODEL), lambda i: (i, C_G2)),
            pl.BlockSpec((tm, D_MODEL), row),
            wspec, wspec, wspec, wspec,
            pl.BlockSpec((1, D_MODEL), const),
        ],
        out_specs=out_specs,
        out_shape=out_shape,
        compiler_params=_params(("parallel",)),
        name="merge",
    )(ohg, orw, omb, proj, proj, proj, h2d, whg, wrw, wmb, wout, fw)


def _sample_call(body, name, bsz, in_specs, args, state_spec, state_shape, scratch, new_state):
    tb = SAMPLE_TB
    aliases = {}
    if new_state is not None:
        in_specs = in_specs + [pl.BlockSpec(memory_space=pl.ANY)]
        args = args + (new_state,)
        aliases = {len(args) - 1: 1}
        inner = body
        body = lambda *refs: inner(*refs[:len(args) - 1], *refs[len(args):])
    return pl.pallas_call(
        body,
        grid=(bsz // tb,),
        in_specs=in_specs,
        out_specs=[pl.BlockSpec((tb, D_MODEL), lambda i: (i, 0)), state_spec],
        out_shape=[jax.ShapeDtypeStruct((bsz, D_MODEL), F32), jax.ShapeDtypeStruct(state_shape, F32)],
        scratch_shapes=scratch,
        input_output_aliases=aliases,
        compiler_params=_params(("parallel",)),
        name=name,
    )(*args)


def _hgrn_prep(q, f, lb):
    sig = _sigmoid(f)
    fg = lb + (1.0 - lb) * sig
    logf = jnp.log(jnp.maximum(fg, HG_F_MIN))
    kx = (1.0 - lb) * (1.0 - sig)
    qx = _silu(q) * (HG_DK ** -0.5)
    return qx, kx, logf


def _hgrn_post(o, g, nw):
    ms = jnp.mean(o * o, axis=-1, keepdims=True)
    return o * lax.rsqrt(ms + RMS_EPS) * nw * _silu(g)


def _hgrn_prompt_kernel(q_ref, f_ref, i_ref, g_ref, lb_ref, nw_ref, tril_ref, o_ref, s_ref,
                        st_ref, qs_ref, ks_ref, bs_ref, ob_ref, *, nblk):
    c = pl.program_id(1)
    nb = q_ref.shape[0]

    @pl.when(c == 0)
    def _():
        st_ref[...] = jnp.zeros_like(st_ref)

    tril = tril_ref[...]
    heads = [(i, h, slice(h * 128, (h + 1) * 128)) for i in range(nb) for h in range(HG_HEADS)]
    for i, h, sl in heads:
        qx, kx, logf = _hgrn_prep(q_ref[i, :, sl], f_ref[i, :, sl], lb_ref[:, sl])
        qs_ref[i, :, sl] = qx
        ks_ref[i, :, sl] = kx
        bs_ref[i, :, sl] = _mm_exact(tril, logf) * LOG2E
    half = HG_BLK // 2
    lane8 = lax.broadcasted_iota(jnp.int32, (half, HG_DK), 1)
    top16 = lax.broadcasted_iota(jnp.int32, (HG_BLK, HG_DK), 0) < half
    lower = (lax.broadcasted_iota(jnp.int32, (HG_BLK, HG_BLK), 1)
             <= lax.broadcasted_iota(jnp.int32, (HG_BLK, HG_BLK), 0))

    def intra_weights(j):
        r0 = j * HG_BLK if isinstance(j, int) else pl.multiple_of(j * HG_BLK, HG_BLK)
        top, bot, blk = pl.ds(r0, half), pl.ds(r0 + half, half), pl.ds(r0, HG_BLK)
        amat = []
        for i, h, sl in heads:
            q_t, q_b, b_t, b_b = qs_ref[i, top, sl], qs_ref[i, bot, sl], bs_ref[i, top, sl], bs_ref[i, bot, sl]
            k16, b16 = ks_ref[i, blk, sl], bs_ref[i, blk, sl]
            a_top = jnp.zeros((half, HG_DK), F32)
            a_bot = jnp.zeros((half, HG_DK), F32)
            for jj in range(half):
                s = jnp.sum(q_t * jnp.exp2(b_t - b16[jj:jj + 1]) * k16[jj:jj + 1], axis=-1, keepdims=True)
                a_top = jnp.where(lane8 == jj, s, a_top)
                jb = half + jj
                s = jnp.sum(q_b * jnp.exp2(b_b - b16[jb:jb + 1]) * k16[jb:jb + 1], axis=-1, keepdims=True)
                a_bot = jnp.where(lane8 == jb, s, a_bot)
            b_mid = b16[half - 1:half]
            k_dec = jnp.where(top16, k16 * jnp.exp2(b_mid - b16), 0.0)
            cross = _mm(q_b * jnp.exp2(b_b - b_mid), k_dec, NT)
            full = jnp.concatenate([a_top[:, :HG_BLK], a_bot[:, :HG_BLK] + cross], axis=0)
            amat.append(jnp.where(lower, full, 0.0))
        return tuple(amat)

    def apply_block(j, amat):
        rows = pl.ds(pl.multiple_of(j * HG_BLK, HG_BLK), HG_BLK)
        idx = range(len(heads))
        qb = [qs_ref[i, rows, sl] for i, h, sl in heads]
        kb = [ks_ref[i, rows, sl] for i, h, sl in heads]
        bb = [bs_ref[i, rows, sl] for i, h, sl in heads]
        vb = [i_ref[i, rows, sl] for i, h, sl in heads]
        st = [st_ref[i, h] for i, h, sl in heads]
        intra = [_mm(amat[n], vb[n]) for n in idx]
        inter = [_mm(qb[n] * jnp.exp2(bb[n]), st[n], NT) for n in idx]
        for n, (i, h, sl) in enumerate(heads):
            b_end = bb[n][HG_BLK - 1:HG_BLK, :]
            st_ref[i, h] = st[n] * jnp.exp2(b_end) + _mm(vb[n], kb[n] * jnp.exp2(b_end - bb[n]), TN)
        for n, (i, h, sl) in enumerate(heads):
            ob_ref[i, rows, sl] = intra[n] + inter[n]

    def step(j, amats):
        apply_block(j - 1, amats)
        return intra_weights(j)

    apply_block(nblk - 1, lax.fori_loop(1, nblk, step, intra_weights(0)))
    nw = nw_ref[...]
    for i, h, sl in heads:
        o_ref[i, :, sl] = _hgrn_post(ob_ref[i, :, sl], g_ref[i, :, sl], nw).astype(o_ref.dtype)

    @pl.when(c == pl.num_programs(1) - 1)
    def _():
        for i, h, sl in heads:
            s_ref[i, h] = st_ref[i, h].T


def _hgrn_prompt(proj, lb, nw, bsz, t, l):
    tt = min(256, t)
    n_t = t // tt
    nb = HG_NB if bsz % HG_NB == 0 else 1
    tril = jnp.asarray(np.kron(np.eye(tt // HG_BLK), np.tril(np.ones((HG_BLK, HG_BLK)))), BF16)
    proj3 = proj.reshape(bsz, t, NCOL)
    col = lambda blk: (lambda b, c: (b, c, blk))
    const = lambda shape: pl.BlockSpec(shape, lambda b, c: (0, 0))
    layer = lambda shape: pl.BlockSpec((None,) + shape, lambda b, c: (l, 0, 0))
    tile = pltpu.VMEM((nb, tt, D_MODEL), F32)
    o, s = pl.pallas_call(
        functools.partial(_hgrn_prompt_kernel, nblk=tt // HG_BLK),
        grid=(bsz // nb, n_t),
        in_specs=[
            pl.BlockSpec((nb, tt, D_MODEL), col(C_Q)), pl.BlockSpec((nb, tt, D_MODEL), col(C_F)),
            pl.BlockSpec((nb, tt, D_MODEL), col(C_I)), pl.BlockSpec((nb, tt, D_MODEL), col(C_GH)),
            layer((1, D_MODEL)), layer((1, 128)), const((tt, tt)),
        ],
        out_specs=[
            pl.BlockSpec((nb, tt, D_MODEL), lambda b, c: (b, c, 0)),
            pl.BlockSpec((nb, HG_HEADS, HG_DK, HG_DV), lambda b, c: (b, 0, 0, 0)),
        ],
        out_shape=[
            jax.ShapeDtypeStruct((bsz, t, D_MODEL), BF16),
            jax.ShapeDtypeStruct((bsz, HG_HEADS, HG_DK, HG_DV), F32),
        ],
        scratch_shapes=[pltpu.VMEM((nb, HG_HEADS, HG_DV, HG_DK), F32), tile, tile, tile, tile],
        compiler_params=_params(("parallel", "arbitrary")),
        name="hgrn_prompt",
    )(proj3, proj3, proj3, proj3, lb, nw, tril)
    return o.reshape(bsz * t, D_MODEL), s


def _hgrn_sample_kernel(q_ref, f_ref, i_ref, g_ref, lb_ref, nw_ref, s_ref, o_ref, so_ref,
                        qs_ref, ks_ref, fs_ref, ob_ref):
    qx, kx, logf = _hgrn_prep(q_ref[...], f_ref[...], lb_ref[...])
    qs_ref[...] = qx
    ks_ref[...] = kx
    fs_ref[...] = jnp.exp(logf)
    ob_ref[...] = jnp.zeros_like(ob_ref)
    rows = lax.broadcasted_iota(jnp.int32, (SAMPLE_TB, 1), 0)

    heads = [slice(h * 128, (h + 1) * 128) for h in range(HG_HEADS)]

    unroll = 2

    def per_seqs(i, carry):
        work = [(i * unroll + j, rows == i * unroll + j, h, sl) for j in range(unroll)
                for h, sl in enumerate(heads)]
        fcol = [_col_bcast(jnp.where(sel, fs_ref[:, sl], 0.0), HG_DV) for b, sel, h, sl in work]
        kv = [_mm(jnp.where(sel, ks_ref[:, sl], 0.0), i_ref[:, sl], TN) for b, sel, h, sl in work]
        s_new = [fcol[n] * s_ref[b, h] + kv[n] for n, (b, sel, h, sl) in enumerate(work)]
        for s, (b, sel, h, sl) in zip(s_new, work):
            so_ref[b, h] = s
        outs = [_mm(jnp.where(sel, qs_ref[:, sl], 0.0), s) for s, (b, sel, h, sl) in zip(s_new, work)]
        for h, sl in enumerate(heads):
            ob_ref[:, sl] += sum(outs[j * HG_HEADS + h] for j in range(unroll))
        return carry

    lax.fori_loop(0, SAMPLE_TB // unroll, per_seqs, 0)
    nw = nw_ref[...]
    for h in range(HG_HEADS):
        sl = slice(h * 128, (h + 1) * 128)
        o_ref[:, sl] = _hgrn_post(ob_ref[:, sl], g_ref[:, sl], nw)


def _hgrn_sample(proj, lb, nw, state, l, new_state):
    bsz = state.shape[1]
    tb = SAMPLE_TB
    col = lambda blk: (lambda i: (i, blk))
    tile = pltpu.VMEM((tb, D_MODEL), F32)
    sspec = pl.BlockSpec((None, tb, HG_HEADS, HG_DK, HG_DV), lambda i: (l, i, 0, 0, 0))
    return _sample_call(
        _hgrn_sample_kernel, "hgrn_sample", bsz,
        [
            pl.BlockSpec((tb, D_MODEL), col(C_Q)), pl.BlockSpec((tb, D_MODEL), col(C_F)),
            pl.BlockSpec((tb, D_MODEL), col(C_I)), pl.BlockSpec((tb, D_MODEL), col(C_GH)),
            pl.BlockSpec((None, 1, D_MODEL), lambda i: (l, 0, 0)),
            pl.BlockSpec((None, 1, 128), lambda i: (l, 0, 0)),
            sspec,
        ],
        (proj, proj, proj, proj, lb, nw, state), sspec, state.shape, [tile, tile, tile, tile], new_state)


def _rw_lora(wdad, w2a2):
    lo = lax.broadcasted_iota(jnp.int32, wdad.shape, 1) < RW_LORA
    lora_w = _mm3(jnp.where(lo, jnp.tanh(wdad), 0.0), w2a2)
    lora_a = _mm3(jnp.where(lo, 0.0, wdad), w2a2)
    return lora_w, lora_a


def _rw_prep_all(items, ones_bd):
    rs, ks, vs, lws, a_sigs, kks = [], [], [], [], [], []
    for r, k0, v, lora_w, lora_a, w0, a0, kk_w, ka_w in items:
        lws.append(-RW_DECAY_SCALE * _sigmoid(w0 + lora_w))
        a_sig = _sigmoid(a0 + lora_a)
        a_sigs.append(a_sig)
        kks.append(k0 * kk_w)
        rs.append(r)
        vs.append(v)
        ks.append(k0 * (1.0 + (a_sig - 1.0) * ka_w))
    sq = _seg_many(ones_bd, [kk * kk for kk in kks])
    kks = [kk / jnp.maximum(jnp.sqrt(s), 1e-12) for kk, s in zip(kks, sq)]
    return rs, ks, vs, lws, [-kk for kk in kks], [kk * a for kk, a in zip(kks, a_sigs)]


def _rw_posts(os_, rs, ks, vs, gs, rk_ws, lnws, lnbs, ones_bd):
    idx = range(len(os_))
    sums = _seg_many(ones_bd, list(os_) + [rs[i] * ks[i] * rk_ws[i] for i in idx])
    ds = [os_[i] - sums[i] * (1.0 / RW_N) for i in idx]
    bonus = sums[len(os_):]
    var = [s * (1.0 / RW_N) for s in _seg_many(ones_bd, [d * d for d in ds])]
    return [(ds[i] * lax.rsqrt(var[i] + RW_LN_EPS) * lnws[i] + lnbs[i] + bonus[i] * vs[i]) * _silu(gs[i])
            for i in idx]


def _rw_chunks(rs, ks, vs, lws, avs, bvs, sbds, tril):
    n = RW_CHUNK
    pairs = range(len(rs))
    lane = lax.broadcasted_iota(jnp.int32, (n, 128), 1)
    row = lax.broadcasted_iota(jnp.int32, (n, 128), 0)
    m0 = lane < RW_N
    col = jnp.where(m0, lane, lane - RW_N)
    strict = col < row
    incl = col <= row
    eye = (col == row).astype(F32)

    def bd(x):
        xb = x.astype(BF16)
        zero = jnp.zeros_like(xb)
        return jnp.concatenate([jnp.where(m0, xb, zero), jnp.where(m0, zero, xb)], axis=0)

    cls = [_mm_exact(tril, lws[q]) for q in pairs]
    a_h = [avs[q] * jnp.exp(cls[q] - lws[q]) for q in pairs]
    r_h = [rs[q] * jnp.exp(cls[q]) for q in pairs]
    b_c = [bvs[q] * jnp.exp(-cls[q]) for q in pairs]
    k_c = [ks[q] * jnp.exp(-cls[q]) for q in pairs]
    lhs = [jnp.concatenate([a_h[q], r_h[q]], axis=0) for q in pairs]
    from_state = [_mm(lhs[q], sbds[q], NT) for q in pairs]
    m_bk = [_mm(lhs[q], jnp.concatenate([bd(b_c[q]), bd(k_c[q])], axis=0), NT) for q in pairs]
    m_ab = [jnp.where(strict, m[:n, :128], 0.0) for m in m_bk]
    m_ak = [jnp.where(strict, m[:n, 128:], 0.0) for m in m_bk]
    m_rb = [jnp.where(incl, m[n:, :128], 0.0) for m in m_bk]
    m_rk = [jnp.where(incl, m[n:, 128:], 0.0) for m in m_bk]
    ps = [_mm(m, bd(m)) for m in m_ab]
    xs = [eye + m for m in m_ab]
    for level in range(1, 5):
        both = [_mm(jnp.concatenate([ps[q], xs[q]], axis=0), bd(ps[q])) for q in pairs]
        ps = [m[:n] for m in both]
        xs = [xs[q] + both[q][n:] for q in pairs]
    xs = [xs[q] + _mm(xs[q], bd(ps[q])) for q in pairs]
    from_v = [_mm(jnp.concatenate([m_ak[q], m_rk[q]], axis=0), bd(vs[q])) for q in pairs]
    w_all = [from_state[q][:n] + from_v[q][:n] for q in pairs]
    u_all = [_mm(xs[q], bd(w_all[q])) for q in pairs]
    o_all = [from_state[q][n:] + from_v[q][n:] + _mm(m_rb[q], bd(u_all[q])) for q in pairs]

    vi = lax.broadcasted_iota(jnp.int32, (128, 128), 0)
    ki = lax.broadcasted_iota(jnp.int32, (128, 128), 1)
    same_head = (vi < RW_N) == (ki < RW_N)
    new = []
    for q in pairs:
        cl_end = cls[q][n - 1:n, :]
        e_end = jnp.exp(cl_end - cls[q])
        upd = _mm(jnp.concatenate([u_all[q], vs[q]], axis=0),
                  jnp.concatenate([bvs[q] * e_end, ks[q] * e_end], axis=0), TN)
        new.append(sbds[q] * jnp.exp(cl_end) + jnp.where(same_head, upd, 0.0))
    return o_all, new


def _rwkv_prompt_kernel(r_ref, k_ref, v_ref, g_ref, wdad_ref, mu_ref, w0_ref, a0_ref, kkw_ref, kaw_ref, rkw_ref,
                        lnw_ref, lnb_ref, w2a2_ref, tril_ref, ones_ref, o_ref, s_ref,
                        sbd_ref, pr_ref, pk_ref, pv_ref, pwd_ref, lw_ref, la_ref):
    c = pl.program_id(1)
    n = RW_CHUNK
    nb = r_ref.shape[0]

    @pl.when(c == 0)
    def _():
        sbd_ref[...] = jnp.zeros_like(sbd_ref)
        pr_ref[...] = jnp.zeros_like(pr_ref)
        pk_ref[...] = jnp.zeros_like(pk_ref)
        pv_ref[...] = jnp.zeros_like(pv_ref)
        pwd_ref[...] = jnp.zeros_like(pwd_ref)

    row = lax.broadcasted_iota(jnp.int32, (n, 128), 0)

    def shift(z_ref, prev_ref, i, sl, mu_lo):
        z = z_ref[i, :, sl]
        zp = jnp.where(row == 0, prev_ref[i, :, sl], pltpu.roll(z, 1, axis=0))
        prev_ref[i, :, sl] = z[n - 1:n, :]
        return z + (zp - z) * mu_ref[:, mu_lo:mu_lo + 128]

    for i in range(nb):
        lora_w, lora_a = _rw_lora(shift(wdad_ref, pwd_ref, i, slice(0, 128), 3 * D_MODEL), w2a2_ref[...])
        lw_ref[i] = lora_w
        la_ref[i] = lora_a
    ones_bd = ones_ref[...]
    lanes = [slice(q * 128, (q + 1) * 128) for q in range(RW_HEADS // 2)]
    pairs = [(i, q, sl) for i in range(nb) for q, sl in enumerate(lanes)]
    rs, ks, vs, lws, avs, bvs = _rw_prep_all(
        [(shift(r_ref, pr_ref, i, sl, q * 128), shift(k_ref, pk_ref, i, sl, D_MODEL + q * 128),
          shift(v_ref, pv_ref, i, sl, 2 * D_MODEL + q * 128), lw_ref[i, :, sl], la_ref[i, :, sl],
          w0_ref[:, sl], a0_ref[:, sl], kkw_ref[:, sl], kaw_ref[:, sl]) for i, q, sl in pairs], ones_bd)
    o_all, sbd_new = _rw_chunks(rs, ks, vs, lws, avs, bvs, [sbd_ref[i, q] for i, q, sl in pairs], tril_ref[...])
    for (i, q, sl), new in zip(pairs, sbd_new):
        sbd_ref[i, q] = new
    outs = _rw_posts(o_all, rs, ks, vs, [g_ref[i, :, sl] for i, q, sl in pairs],
                     [rkw_ref[:, sl] for i, q, sl in pairs], [lnw_ref[:, sl] for i, q, sl in pairs],
                     [lnb_ref[:, sl] for i, q, sl in pairs], ones_bd)
    for (i, q, sl), out in zip(pairs, outs):
        o_ref[i, :, sl] = out.astype(o_ref.dtype)

    @pl.when(c == pl.num_programs(1) - 1)
    def _():
        s_ref[...] = sbd_ref[...]


def _rw_consts():
    tril = jnp.asarray(np.tril(np.ones((RW_CHUNK, RW_CHUNK))), BF16)
    ones_bd = jnp.asarray(np.kron(np.eye(2), np.ones((RW_N, RW_N))), BF16)
    return tril, ones_bd


def _rwkv_prompt(proj, p, bsz, t, l):
    n = RW_CHUNK
    n_t = t // n
    nb = RW_NB if bsz % RW_NB == 0 else 1
    tril, ones_bd = _rw_consts()
    proj3 = proj.reshape(bsz, t, NCOL)
    col = lambda blk: (lambda b, c: (b, c, blk))
    const = lambda shape: pl.BlockSpec(shape, lambda b, c: (0, 0))
    layer = lambda shape: pl.BlockSpec((None,) + shape, lambda b, c: (l, 0, 0))
    vec = layer((1, D_MODEL))
    prev = pltpu.VMEM((nb, 1, D_MODEL), F32)
    tile = pltpu.VMEM((nb, n, D_MODEL), F32)
    o, sbd = pl.pallas_call(
        _rwkv_prompt_kernel,
        grid=(bsz // nb, n_t),
        in_specs=[
            pl.BlockSpec((nb, n, D_MODEL), col(C_R)), pl.BlockSpec((nb, n, D_MODEL), col(C_K)),
            pl.BlockSpec((nb, n, D_MODEL), col(C_V)), pl.BlockSpec((nb, n, D_MODEL), col(C_GR)),
            pl.BlockSpec((nb, n, 128), lambda b, c: (b, c, C_WDAD)),
            layer((1, RW_SHIFT)),
            vec, vec, vec, vec, vec, vec, vec,
            layer((128, D_MODEL)), const((n, n)), const((128, 128)),
        ],
        out_specs=[
            pl.BlockSpec((nb, n, D_MODEL), lambda b, c: (b, c, 0)),
            pl.BlockSpec((nb, RW_HEADS // 2, 128, 128), lambda b, c: (b, 0, 0, 0)),
        ],
        out_shape=[
            jax.ShapeDtypeStruct((bsz, t, D_MODEL), BF16),
            jax.ShapeDtypeStruct((bsz, RW_HEADS // 2, 128, 128), F32),
        ],
        scratch_shapes=[pltpu.VMEM((nb, RW_HEADS // 2, 128, 128), F32), prev, prev, prev,
                        pltpu.VMEM((nb, 1, 128), F32), tile, tile],
        compiler_params=_params(("parallel", "arbitrary")),
        name="rwkv_prompt",
    )(proj3, proj3, proj3, proj3, proj3, p["mu"], p["w0"], p["a0"], p["kk"], p["ka"], p["rk"], p["lnw"], p["lnb"],
      p["w2a2"], tril, ones_bd)
    s6 = sbd.reshape(bsz, RW_HEADS // 2, 2, RW_N, 2, RW_N)
    state = jnp.stack([s6[:, :, 0, :, 0, :], s6[:, :, 1, :, 1, :]], axis=2)
    return o.reshape(bsz * t, D_MODEL), state.reshape(bsz, RW_HEADS, RW_N, RW_N)


def _rwkv_sample_kernel(r_ref, k_ref, v_ref, g_ref, wdad_ref, sh_ref, mu_ref, w0_ref, a0_ref, kkw_ref, kaw_ref,
                        rkw_ref, lnw_ref, lnb_ref, w2a2_ref, ones_ref, s_ref, o_ref, so_ref,
                        rs_ref, ks_ref, vs_ref, ob_ref, kr_ref, wr_ref, ar_ref, br_ref):
    ones_bd = ones_ref[...]
    tb = SAMPLE_TB

    def store_rows(ref, sl, x):
        for i in range(tb):
            ref[i, :, sl] = x[i:i + 1, :]

    def shift(z, lo, width):
        return z + (sh_ref[:, lo:lo + width] - z) * mu_ref[:, lo:lo + width]

    wdad = shift(wdad_ref[...], 3 * D_MODEL, 128)
    lora_w, lora_a = _rw_lora(wdad, w2a2_ref[...])
    pairs = [slice(q * 128, (q + 1) * 128) for q in range(RW_HEADS // 2)]
    rs, ks, vs, lws, avs, bvs = _rw_prep_all(
        [(shift(r_ref[:, sl], q * 128, 128), shift(k_ref[:, sl], D_MODEL + q * 128, 128),
          shift(v_ref[:, sl], 2 * D_MODEL + q * 128, 128), lora_w[:, sl], lora_a[:, sl], w0_ref[:, sl],
          a0_ref[:, sl], kkw_ref[:, sl], kaw_ref[:, sl]) for q, sl in enumerate(pairs)], ones_bd)
    for q, sl in enumerate(pairs):
        rs_ref[:, sl] = rs[q]
        ks_ref[:, sl] = ks[q]
        vs_ref[:, sl] = vs[q]
        store_rows(kr_ref, sl, ks[q])
        store_rows(wr_ref, sl, jnp.exp(lws[q]))
        store_rows(ar_ref, sl, avs[q])
        store_rows(br_ref, sl, bvs[q])
    ob_ref[...] = jnp.zeros_like(ob_ref)
    rows = lax.broadcasted_iota(jnp.int32, (tb, 1), 0)

    hs = range(RW_HEADS)
    lanes = [slice(h * RW_N, (h + 1) * RW_N) for h in hs]
    unroll = 2

    def per_seqs(i, carry):
        bs = [i * unroll + j for j in range(unroll)]
        work = [(b, rows == b, h) for b in bs for h in hs]

        def row_of(ref, b, h):
            pair = slice((h // 2) * 128, (h // 2 + 1) * 128)
            return ref[b, :, pair][:, (h % 2) * RW_N:(h % 2 + 1) * RW_N]

        vcol = [_col_bcast(jnp.where(sel, vs_ref[:, lanes[h]], 0.0), RW_N) for b, sel, h in work]
        s_old = [s_ref[b, h] for b, sel, h in work]
        sa = [jnp.sum(s * row_of(ar_ref, b, h), axis=-1, keepdims=True) for s, (b, sel, h) in zip(s_old, work)]
        s_new = [s_old[n] * row_of(wr_ref, b, h) + sa[n] * row_of(br_ref, b, h) + vcol[n] * row_of(kr_ref, b, h)
                 for n, (b, sel, h) in enumerate(work)]
        for s, (b, sel, h) in zip(s_new, work):
            so_ref[b, h] = s
        outs = [_mm(jnp.where(sel, rs_ref[:, lanes[h]], 0.0), s, NT) for s, (b, sel, h) in zip(s_new, work)]
        for h in hs:
            ob_ref[:, lanes[h]] += sum(outs[j * RW_HEADS + h] for j in range(unroll))
        return carry

    lax.fori_loop(0, tb // unroll, per_seqs, 0)
    outs = _rw_posts([ob_ref[:, sl] for sl in pairs], rs, ks, vs, [g_ref[:, sl] for sl in pairs],
                     [rkw_ref[:, sl] for sl in pairs], [lnw_ref[:, sl] for sl in pairs],
                     [lnb_ref[:, sl] for sl in pairs], ones_bd)
    for sl, out in zip(pairs, outs):
        o_ref[:, sl] = out


def _rwkv_sample(proj, p, shift_state, state, l, new_state):
    bsz = state.shape[1]
    tb = SAMPLE_TB
    _, ones_bd = _rw_consts()
    col = lambda blk: (lambda i: (i, blk))
    layer = lambda shape: pl.BlockSpec((None,) + shape, lambda i: (l, 0, 0))
    vec = layer((1, D_MODEL))
    tile = pltpu.VMEM((tb, D_MODEL), F32)
    sspec = pl.BlockSpec((None, tb, RW_HEADS, RW_N, RW_N), lambda i: (l, i, 0, 0, 0))
    return _sample_call(
        _rwkv_sample_kernel, "rwkv_sample", bsz,
        [
            pl.BlockSpec((tb, D_MODEL), col(C_R)), pl.BlockSpec((tb, D_MODEL), col(C_K)),
            pl.BlockSpec((tb, D_MODEL), col(C_V)), pl.BlockSpec((tb, D_MODEL), col(C_GR)),
            pl.BlockSpec((tb, 128), lambda i: (i, C_WDAD)),
            pl.BlockSpec((None, tb, RW_SHIFT), lambda i: (l, i, 0)),
            layer((1, RW_SHIFT)),
            vec, vec, vec, vec, vec, vec, vec,
            layer((128, D_MODEL)),
            pl.BlockSpec((128, 128), lambda i: (0, 0)),
            sspec,
        ],
        (proj, proj, proj, proj, proj, shift_state, p["mu"], p["w0"], p["a0"], p["kk"], p["ka"], p["rk"],
         p["lnw"], p["lnb"], p["w2a2"], ones_bd, state),
        sspec, state.shape, [tile] * 4 + [pltpu.VMEM((tb, 1, D_MODEL), F32)] * 4, new_state)


def _mb_post(y, xs, z, d_rep, nw):
    y = (y + d_rep * xs) * _silu(z)
    ms = jnp.mean(y * y, axis=-1, keepdims=True)
    return y * lax.rsqrt(ms + RMS_EPS) * nw


def _ssd_prompt_kernel(x_ref, bc_ref, z_ref, dt_ref, cw_ref, cb_ref, dtb_ref, alog_ref, d_ref, nw_ref, tril_ref,
                       exp_ref, o_ref, s_ref, h_ref, px_ref, pbc_ref):
    c = pl.program_id(1)
    n = MB_CHUNK

    @pl.when(c == 0)
    def _():
        h_ref[...] = jnp.zeros_like(h_ref)
        px_ref[...] = jnp.zeros_like(px_ref)
        pbc_ref[...] = jnp.zeros_like(pbc_ref)

    def conv(u_ref, prev_ref, sl, wsl):
        u = u_ref[:, sl]
        prev = prev_ref[:, sl]
        w = cw_ref[:, wsl]
        row = lax.broadcasted_iota(jnp.int32, (8, u.shape[1]), 0)
        acc = u * w[MB_CONV - 1:MB_CONV, :] + cb_ref[:, wsl]
        for s in range(1, MB_CONV):
            us = pltpu.roll(u, s, axis=0)
            top = jnp.where(row < s, pltpu.roll(prev, s, axis=0), us[0:8])
            us = jnp.concatenate([top, us[8:]], axis=0)
            acc = acc + us * w[MB_CONV - 1 - s:MB_CONV - s, :]
        prev_ref[:, sl] = u[n - 8:n, :]
        return _silu(acc)

    tril = tril_ref[...]
    ti = lax.broadcasted_iota(jnp.int32, (n, n), 0)
    si = lax.broadcasted_iota(jnp.int32, (n, n), 1)
    incl = si <= ti
    lane = lax.broadcasted_iota(jnp.int32, (n, 4 * MB_P), 1)
    nbc = MB_GROUPS * MB_N
    groups = range(MB_GROUPS)
    gx = [slice(g * 256, (g + 1) * 256) for g in groups]

    spread = exp_ref[...]
    dt_c = _softplus(dt_ref[...] + dtb_ref[...])
    a_c = dt_c * (-jnp.exp(alog_ref[...]))
    cs_c = _mm_exact(tril, a_c)
    cs_rows = _mm_exact((ti <= si).astype(BF16), a_c, TN, x_is_lhs=True)
    dt_all = _mm_exact(spread, dt_c, x_is_lhs=True)
    cs_all = _mm_exact(spread, cs_c, x_is_lhs=True)

    xs = [conv(x_ref, px_ref, gx[g], gx[g]) for g in groups]
    bm = [conv(bc_ref, pbc_ref, slice(g * MB_N, (g + 1) * MB_N),
               slice(D_MODEL + g * MB_N, D_MODEL + (g + 1) * MB_N)) for g in groups]
    cm = [conv(bc_ref, pbc_ref, slice(nbc + g * MB_N, nbc + (g + 1) * MB_N),
               slice(D_MODEL + nbc + g * MB_N, D_MODEL + nbc + (g + 1) * MB_N)) for g in groups]
    xdt = [xs[g] * dt_all[:, gx[g]] for g in groups]
    cs = [cs_all[:, gx[g]] for g in groups]
    gmat = [_mm(cm[g], bm[g], NT) for g in groups]
    h_all = [h_ref[g] for g in groups]
    y = [jnp.exp(cs[g]) * _mm(cm[g], h_all[g], NT) for g in groups]
    for g in groups:
        for e in range(4):
            hd = slice(e * MB_P, (e + 1) * MB_P)
            hrow = slice(4 * g + e, 4 * g + e + 1)
            lmat = jnp.where(incl, jnp.exp(cs[g][:, hd] - cs_rows[hrow, :]), 0.0)
            me = (lane >= e * MB_P) & (lane < (e + 1) * MB_P)
            y[g] = y[g] + _mm(gmat[g] * lmat, jnp.where(me, xdt[g], 0.0))
    for g in groups:
        cs_end = cs[g][n - 1:n, :]
        upd = _mm(xdt[g] * jnp.exp(cs_end - cs[g]), bm[g], TN)
        decay = [jnp.exp(cs_rows[4 * g + e:4 * g + e + 1, n - 1:n]) for e in range(4)]
        h_ref[g] = jnp.concatenate([h_all[g][e * MB_P:(e + 1) * MB_P] * decay[e] for e in range(4)], axis=0) + upd
    for g in groups:
        o_ref[:, gx[g]] = _mb_post(y[g], xs[g], z_ref[:, gx[g]], d_ref[:, gx[g]],
                                   nw_ref[:, gx[g]]).astype(o_ref.dtype)

    @pl.when(c == pl.num_programs(1) - 1)
    def _():
        for g in range(MB_GROUPS):
            for e in range(4):
                s_ref[0, 4 * g + e] = h_ref[g, e * MB_P:(e + 1) * MB_P, :]


def _head_spread():
    m = np.zeros((128, D_MODEL), np.float32)
    for h in range(MB_HEADS):
        m[h, h * MB_P:(h + 1) * MB_P] = 1.0
    return jnp.asarray(m, BF16)


def _ssd_prompt(proj, p, bsz, t, l):
    n = MB_CHUNK
    n_t = t // n
    tril = jnp.asarray(np.tril(np.ones((n, n))), BF16)
    col = lambda blk: (lambda b, c: (b * n_t + c, blk))
    const = lambda shape: pl.BlockSpec(shape, lambda b, c: (0, 0))
    layer = lambda shape: pl.BlockSpec((None,) + shape, lambda b, c: (l, 0, 0))
    vec = layer((1, D_MODEL))
    return pl.pallas_call(
        _ssd_prompt_kernel,
        grid=(bsz, n_t),
        in_specs=[
            pl.BlockSpec((n, D_MODEL), col(C_X)), pl.BlockSpec((n, D_MODEL), col(C_BC)),
            pl.BlockSpec((n, D_MODEL), col(C_Z)), pl.BlockSpec((n, 128), col(C_DT)),
            layer((MB_CONV, MB_XBC)), layer((1, MB_XBC)),
            layer((1, 128)), layer((1, 128)), vec, vec,
            const((n, n)), const((128, D_MODEL)),
        ],
        out_specs=[
            pl.BlockSpec((n, D_MODEL), lambda b, c: (b * n_t + c, 0)),
            pl.BlockSpec((1, MB_HEADS, MB_P, MB_N), lambda b, c: (b, 0, 0, 0)),
        ],
        out_shape=[
            jax.ShapeDtypeStruct((bsz * t, D_MODEL), BF16),
            jax.ShapeDtypeStruct((bsz, MB_HEADS, MB_P, MB_N), F32),
        ],
        scratch_shapes=[pltpu.VMEM((MB_GROUPS, 4 * MB_P, MB_N), F32), pltpu.VMEM((8, D_MODEL), F32),
                        pltpu.VMEM((8, D_MODEL), F32)],
        compiler_params=_params(("parallel", "arbitrary")),
        name="ssd_prompt",
    )(proj, proj, proj, proj, p["cw"], p["cb"], p["dtb"], p["alog"], p["d"], p["nw"], tril, _head_spread())


def _ssd_sample_kernel(xbc_ref, z_ref, dt_ref, cs_ref, cw_ref, cb_ref, dtb_ref, alog_ref, d_ref, nw_ref, exp_ref,
                       s_ref, o_ref, so_ref, xs_ref, xd_ref, bs_ref, cms_ref, dec_ref, ob_ref):
    tb = SAMPLE_TB
    w = cw_ref[...]
    acc = xbc_ref[...] * w[MB_CONV - 1:MB_CONV, :] + cb_ref[...]
    for i in range(MB_CONV - 1):
        acc = acc + cs_ref[i] * w[i:i + 1, :]
    xbc = _silu(acc)
    xs = xbc[:, :D_MODEL]
    dt = _softplus(dt_ref[...] + dtb_ref[...])
    xs_ref[...] = xs
    xd_ref[...] = xs * _mm_exact(exp_ref[...], dt, x_is_lhs=True)
    bs_ref[...] = xbc[:, D_MODEL:D_MODEL + MB_GROUPS * MB_N]
    cms_ref[...] = xbc[:, D_MODEL + MB_GROUPS * MB_N:]
    dec = jnp.exp(dt * (-jnp.exp(alog_ref[...])))
    for i in range(tb):
        dec_ref[i] = dec[i:i + 1, :]
    ob_ref[...] = jnp.zeros_like(ob_ref)
    rows = lax.broadcasted_iota(jnp.int32, (tb, 1), 0)

    hs = range(MB_HEADS)
    lanes = [slice(h * MB_P, (h + 1) * MB_P) for h in hs]
    grp = [slice((h // 4) * MB_N, (h // 4 + 1) * MB_N) for h in hs]
    unroll = 2

    def per_seqs(i, carry):
        work = [(i * unroll + j, rows == i * unroll + j, h) for j in range(unroll) for h in hs]

        def dec_of(b, h):
            return dec_ref[b][:, h:h + 1]

        upd = [_mm(jnp.where(sel, xd_ref[:, lanes[h]], 0.0), bs_ref[:, grp[h]], TN) for b, sel, h in work]
        h_new = [dec_of(b, h) * s_ref[b, h] + upd[n] for n, (b, sel, h) in enumerate(work)]
        for s, (b, sel, h) in zip(h_new, work):
            so_ref[b, h] = s
        outs = [_mm(jnp.where(sel, cms_ref[:, grp[h]], 0.0), s, NT) for s, (b, sel, h) in zip(h_new, work)]
        for h in hs:
            ob_ref[:, lanes[h]] += sum(outs[j * MB_HEADS + h] for j in range(unroll))
        return carry

    lax.fori_loop(0, tb // unroll, per_seqs, 0)
    for g in range(MB_GROUPS):
        sl = slice(g * 256, (g + 1) * 256)
        o_ref[:, sl] = _mb_post(ob_ref[:, sl], xs_ref[:, sl], z_ref[:, sl], d_ref[:, sl], nw_ref[:, sl])


def _ssd_sample(proj, p, conv_state, state, l, new_state):
    bsz = state.shape[1]
    tb = SAMPLE_TB
    layer = lambda shape: pl.BlockSpec((None,) + shape, lambda i: (l, 0, 0))
    vec = layer((1, D_MODEL))
    tile = pltpu.VMEM((tb, D_MODEL), F32)
    half = pltpu.VMEM((tb, MB_GROUPS * MB_N), F32)
    sspec = pl.BlockSpec((None, tb, MB_HEADS, MB_P, MB_N), lambda i: (l, i, 0, 0, 0))
    return _sample_call(
        _ssd_sample_kernel, "ssd_sample", bsz,
        [
            pl.BlockSpec((tb, MB_XBC), lambda i: (i, C_X // 2)),
            pl.BlockSpec((tb, D_MODEL), lambda i: (i, C_Z)),
            pl.BlockSpec((tb, 128), lambda i: (i, C_DT)),
            pl.BlockSpec((None, MB_CONV - 1, tb, MB_XBC), lambda i: (l, 0, i, 0)),
            layer((MB_CONV, MB_XBC)), layer((1, MB_XBC)),
            layer((1, 128)), layer((1, 128)), vec, vec,
            pl.BlockSpec((128, D_MODEL), lambda i: (0, 0)),
            sspec,
        ],
        (proj, proj, proj, conv_state, p["cw"], p["cb"], p["dtb"], p["alog"], p["d"], p["nw"], _head_spread(),
         state),
        sspec, state.shape, [tile, tile, half, half, pltpu.VMEM((tb, 1, 128), F32), tile], new_state)


def _prep_w_in(w_in):
    o = np.cumsum([0, 1024, 1024, 1024, 1024, RW_SHIFT, 1024, 1024, MB_XBC, MB_HEADS, 3 * D_MODEL])
    zrw, grw, zmb, xbc, dtc, gate = o[4], o[5], o[6], o[7], o[8], o[9]
    parts = [
        w_in[..., :zrw], w_in[..., zrw:zrw + 3 * D_MODEL], w_in[..., grw:zmb], w_in[..., xbc:dtc],
        w_in[..., zmb:xbc], w_in[..., gate:],
        w_in[..., zrw + 3 * D_MODEL:grw],
        w_in[..., dtc:gate], jnp.zeros(w_in.shape[:-1] + (128 - MB_HEADS,), w_in.dtype),
    ]
    return jnp.concatenate(parts, axis=-1).astype(BF16)


def _rep(v):
    return jnp.repeat(v, MB_P, axis=-1)[:, None, :]


def _pad_heads(v):
    return jnp.pad(v, ((0, 0), (0, 128 - MB_HEADS)))[:, None, :]


def kernel(x_prompt, x_sample, state_hgrn, state_rwkv, state_rwkv_shift, state_ssm, state_conv, norm_w, w_in, hg_lb, hg_norm_w, rw_mu, rw_w0, rw_w2, rw_a0, rw_a2, rw_k_k, rw_k_a, rw_r_k, rw_ln_w, rw_ln_b, mb_conv_w, mb_conv_b, mb_dt_bias, mb_A_log, mb_D, mb_norm_w, w_o_hg, w_o_rw, w_o_mb, w_out, final_norm_w):
    bp, t, _ = x_prompt.shape
    bs = x_sample.shape[0]
    sm = jax.nn.softmax(hg_lb.astype(F32), axis=0)
    row = lambda a: a[:, None, :]
    lbs = row(jnp.cumsum(sm, axis=0) - sm[0:1])
    w_in_r = _prep_w_in(w_in)
    rw_p = dict(mu=row(rw_mu), w0=row(rw_w0), a0=row(rw_a0), kk=row(rw_k_k), ka=row(rw_k_a),
                rk=rw_r_k.reshape(DEPTH, 1, D_MODEL), lnw=row(rw_ln_w), lnb=row(rw_ln_b),
                w2a2=jnp.concatenate([rw_w2, rw_a2], axis=1))
    mb_p = dict(cw=mb_conv_w, cb=row(mb_conv_b), dtb=_pad_heads(mb_dt_bias), alog=_pad_heads(mb_A_log), d=_rep(mb_D),
                nw=row(mb_norm_w))
    nw_in, nw_hg = row(norm_w), row(hg_norm_w)
    whg, wrw, wmb, wout = (w.astype(BF16) for w in (w_o_hg, w_o_rw, w_o_mb, w_out))
    fw = final_norm_w[None, :]
    conv_t = jnp.swapaxes(state_conv, 1, 2)
    lora = slice(C_WDAD * 128, (C_WDAD + 1) * 128)

    hp = x_prompt.reshape(bp * t, D_MODEL)
    hs = x_sample.reshape(bs, D_MODEL)
    p_states, s_small = [], []
    s_hg = s_rw = s_ssm = None
    yp = ys = None
    for l in range(DEPTH):
        final = l == DEPTH - 1

        proj = _inproj(hp, nw_in, w_in_r, l)
        o_hg, p_hg = _hgrn_prompt(proj, lbs, nw_hg, bp, t, l)
        o_rw, p_rw = _rwkv_prompt(proj, rw_p, bp, t, l)
        o_mb, p_ssm = _ssd_prompt(proj, mb_p, bp, t, l)
        proj3 = proj.reshape(bp, t, NCOL)
        p_shift = jnp.concatenate([proj3[:, -1, C_R * 1024:C_GR * 1024], proj3[:, -1, lora]], axis=-1)
        p_conv = proj3[:, t - (MB_CONV - 1):, C_X * 1024:C_Z * 1024]
        res = _merge(o_hg, o_rw, o_mb, proj, hp, whg, wrw, wmb, wout, fw, final, l)
        hp = res[0]
        if final:
            yp = res[1]
        p_states.append((p_hg, p_rw, p_shift, p_ssm, p_conv))

        proj = _inproj(hs, nw_in, w_in_r, l)
        o_hg, s_hg = _hgrn_sample(proj, lbs, nw_hg, state_hgrn, l, s_hg)
        o_rw, s_rw = _rwkv_sample(proj, rw_p, state_rwkv_shift, state_rwkv, l, s_rw)
        o_mb, s_ssm = _ssd_sample(proj, mb_p, conv_t, state_ssm, l, s_ssm)
        s_shift = jnp.concatenate([proj[:, C_R * 1024:C_GR * 1024], proj[:, lora]], axis=-1)
        s_conv = jnp.concatenate([state_conv[l][:, 1:], proj[:, None, C_X * 1024:C_Z * 1024]], axis=1)
        res = _merge(o_hg, o_rw, o_mb, proj, hs, whg, wrw, wmb, wout, fw, final, l)
        hs = res[0]
        if final:
            ys = res[1]
        s_small.append((s_shift, s_conv))

    stack = lambda states, i: jnp.stack([s[i] for s in states])
    return (yp.reshape(bp, t, D_MODEL), ys.reshape(bs, 1, D_MODEL),
            *[stack(p_states, i) for i in range(5)],
            s_hg, s_rw, stack(s_small, 0), s_ssm, stack(s_small, 1))
```

```python
import functools

import numpy as np
import jax
import jax.numpy as jnp
from jax import lax
from jax.experimental import pallas as pl
from jax.experimental.pallas import tpu as pltpu

F32 = jnp.float32
BF16 = jnp.bfloat16
HI = lax.Precision.HIGHEST

D_MODEL = 1024
DEPTH = 2
HG_HEADS, HG_DK, HG_DV = 8, 128, 128
HG_F_MIN = 1e-20
LOG2E = 1.4426950408889634
RW_HEADS, RW_N = 16, 64
RW_LORA = 64
RW_SHIFT = 3 * D_MODEL + 2 * RW_LORA
RW_LN_EPS = 64e-5
RW_DECAY_SCALE = float(np.exp(-0.5))
MB_HEADS, MB_P, MB_GROUPS, MB_N, MB_CONV = 16, 64, 4, 128, 4
MB_XBC = D_MODEL + 2 * MB_GROUPS * MB_N
RMS_EPS = 1e-6

C_Q, C_F, C_I, C_GH, C_R, C_K, C_V, C_GR, C_X, C_BC, C_Z, C_G0, C_G1, C_G2 = range(14)
C_WDAD = 14 * 8
C_DT = 14 * 8 + 1
NCOL = 14 * 1024 + 2 * 128

HG_BLK = 16
HG_NB = 2
RW_CHUNK = 64
RW_NB = 2
MB_CHUNK = 64
SAMPLE_TB = 8

VMEM_LIMIT = 48 * 1024 * 1024

NN = (((1,), (0,)), ((), ()))
NT = (((1,), (1,)), ((), ()))
TN = (((0,), (0,)), ((), ()))


def _mm(a, b, dims=NN, hi=False):
    if hi:
        return lax.dot_general(a.astype(F32), b.astype(F32), dims, precision=HI, preferred_element_type=F32)
    return lax.dot_general(a.astype(BF16), b.astype(BF16), dims, preferred_element_type=F32)


def _mm_exact(e, x, dims=NN, x_is_lhs=False, passes=3):
    e = e.astype(BF16)
    acc = None
    for _ in range(passes):
        p = x.astype(BF16)
        x = x - p.astype(F32)
        t = (lax.dot_general(p, e, dims, preferred_element_type=F32) if x_is_lhs
             else lax.dot_general(e, p, dims, preferred_element_type=F32))
        acc = t if acc is None else acc + t
    return acc


def _mm3(a, b, dims=NN):
    a_hi = a.astype(BF16)
    a_lo = (a - a_hi.astype(F32)).astype(BF16)
    b_hi = b.astype(BF16)
    b_lo = (b - b_hi.astype(F32)).astype(BF16)
    dot = lambda x, y: lax.dot_general(x, y, dims, preferred_element_type=F32)
    return dot(a_hi, b_hi) + (dot(a_hi, b_lo) + dot(a_lo, b_hi))


def _seg_many(ones_bd, xs):
    m = xs[0].shape[0]
    out = _mm_exact(ones_bd, jnp.concatenate(xs, axis=0) if len(xs) > 1 else xs[0], x_is_lhs=True, passes=2)
    return [out[i * m:(i + 1) * m] for i in range(len(xs))]


def _sigmoid(x):
    return jax.nn.sigmoid(x)


def _silu(x):
    return x * jax.nn.sigmoid(x)


def _softplus(x):
    return jnp.maximum(x, 0.0) + jnp.log1p(jnp.exp(-jnp.abs(x)))


def _col_bcast(rows, width):
    ones = jnp.ones((rows.shape[0], width), BF16)
    p1 = rows.astype(BF16)
    r1 = rows - p1.astype(F32)
    p2 = r1.astype(BF16)
    p3 = (r1 - p2.astype(F32)).astype(BF16)
    out = lax.dot_general(p1, ones, TN, preferred_element_type=F32)
    out = out + lax.dot_general(p2, ones, TN, preferred_element_type=F32)
    return out + lax.dot_general(p3, ones, TN, preferred_element_type=F32)


def _params(sem):
    return pltpu.CompilerParams(dimension_semantics=sem, vmem_limit_bytes=VMEM_LIMIT)


def _inproj_kernel(x_ref, nw_ref, w_ref, o_ref, xn_ref):
    @pl.when(pl.program_id(1) == 0)
    def _():
        x = x_ref[...]
        ms = jnp.mean(x * x, axis=-1, keepdims=True)
        xn_ref[...] = (x * lax.rsqrt(ms + RMS_EPS) * nw_ref[...]).astype(BF16)

    o_ref[...] = jnp.dot(xn_ref[...], w_ref[...], preferred_element_type=F32)


def _inproj(x2d, nw, w, l):
    m = x2d.shape[0]
    tm = min(2048, m)
    tn = 768
    return pl.pallas_call(
        _inproj_kernel,
        grid=(m // tm, NCOL // tn),
        in_specs=[
            pl.BlockSpec((tm, D_MODEL), lambda i, j: (i, 0)),
            pl.BlockSpec((None, 1, D_MODEL), lambda i, j: (l, 0, 0)),
            pl.BlockSpec((None, D_MODEL, tn), lambda i, j: (l, 0, j)),
        ],
        out_specs=pl.BlockSpec((tm, tn), lambda i, j: (i, j)),
        out_shape=jax.ShapeDtypeStruct((m, NCOL), F32),
        scratch_shapes=[pltpu.VMEM((tm, D_MODEL), BF16)],
        compiler_params=_params(("parallel", "arbitrary")),
        name="inproj",
    )(x2d, nw, w)


def _merge_kernel(ohg, orw, omb, g0, g1, g2, h_ref, whg, wrw, wmb, wout, fw_ref, hn_ref, *y_ref):
    u = _sigmoid(g0[...]) * _mm(ohg[...], whg[...])
    u = u + _sigmoid(g1[...]) * _mm(orw[...], wrw[...])
    u = u + _sigmoid(g2[...]) * _mm(omb[...], wmb[...])
    hn = h_ref[...] + _mm(u, wout[...])
    hn_ref[...] = hn
    if y_ref:
        ms = jnp.mean(hn * hn, axis=-1, keepdims=True)
        y_ref[0][...] = hn * lax.rsqrt(ms + RMS_EPS) * fw_ref[...]


def _merge(ohg, orw, omb, proj, h2d, whg, wrw, wmb, wout, fw, final, l):
    m = h2d.shape[0]
    tm = min(256, m)
    row = lambda i: (i, 0)
    const = lambda i: (0, 0)
    wspec = pl.BlockSpec((None, D_MODEL, D_MODEL), lambda i: (l, 0, 0))
    out_shape = [jax.ShapeDtypeStruct((m, D_MODEL), F32)]
    out_specs = [pl.BlockSpec((tm, D_MODEL), row)]
    if final:
        out_shape.append(jax.ShapeDtypeStruct((m, D_MODEL), F32))
        out_specs.append(pl.BlockSpec((tm, D_MODEL), row))
    return pl.pallas_call(
        _merge_kernel,
        grid=(m // tm,),
        in_specs=[
            pl.BlockSpec((tm, D_MODEL), row), pl.BlockSpec((tm, D_MODEL), row), pl.BlockSpec((tm, D_MODEL), row),
            pl.BlockSpec((tm, D_MODEL), lambda i: (i, C_G0)),
            pl.BlockSpec((tm, D_MODEL), lambda i: (i, C_G1)),
            pl.BlockSpec((tm, D_MODEL), lambda i: (i, C_G2)),
            pl.BlockSpec((tm, D_MODEL), row),
            wspec, wspec, wspec, wspec,
            pl.BlockSpec((1, D_MODEL), const),
        ],
        out_specs=out_specs,
        out_shape=out_shape,
        compiler_params=_params(("parallel",)),
        name="merge",
    )(ohg, orw, omb, proj, proj, proj, h2d, whg, wrw, wmb, wout, fw)


def _sample_call(body, name, bsz, in_specs, args, state_spec, state_shape, scratch, new_state):
    tb = SAMPLE_TB
    aliases = {}
    if new_state is not None:
        in_specs = in_specs + [pl.BlockSpec(memory_space=pl.ANY)]
        args = args + (new_state,)
        aliases = {len(args) - 1: 1}
        inner = body
        body = lambda *refs: inner(*refs[:len(args) - 1], *refs[len(args):])
    return pl.pallas_call(
        body,
        grid=(bsz // tb,),
        in_specs=in_specs,
        out_specs=[pl.BlockSpec((tb, D_MODEL), lambda i: (i, 0)), state_spec],
        out_shape=[jax.ShapeDtypeStruct((bsz, D_MODEL), F32), jax.ShapeDtypeStruct(state_shape, F32)],
        scratch_shapes=scratch,
        input_output_aliases=aliases,
        compiler_params=_params(("parallel",)),
        name=name,
    )(*args)


def _hgrn_prep(q, f, lb):
    sig = _sigmoid(f)
    fg = lb + (1.0 - lb) * sig
    logf = jnp.log(jnp.maximum(fg, HG_F_MIN))
    kx = (1.0 - lb) * (1.0 - sig)
    qx = _silu(q) * (HG_DK ** -0.5)
    return qx, kx, logf


def _hgrn_post(o, g, nw):
    ms = jnp.mean(o * o, axis=-1, keepdims=True)
    return o * lax.rsqrt(ms + RMS_EPS) * nw * _silu(g)


def _hgrn_prompt_kernel(q_ref, f_ref, i_ref, g_ref, lb_ref, nw_ref, tril_ref, o_ref, s_ref,
                        st_ref, qs_ref, ks_ref, bs_ref, ob_ref, *, nblk):
    c = pl.program_id(1)
    nb = q_ref.shape[0]

    @pl.when(c == 0)
    def _():
        st_ref[...] = jnp.zeros_like(st_ref)

    tril = tril_ref[...]
    heads = [(i, h, slice(h * 128, (h + 1) * 128)) for i in range(nb) for h in range(HG_HEADS)]
    for i, h, sl in heads:
        qx, kx, logf = _hgrn_prep(q_ref[i, :, sl], f_ref[i, :, sl], lb_ref[:, sl])
        qs_ref[i, :, sl] = qx
        ks_ref[i, :, sl] = kx
        bs_ref[i, :, sl] = _mm_exact(tril, logf) * LOG2E
    half = HG_BLK // 2
    lane8 = lax.broadcasted_iota(jnp.int32, (half, HG_DK), 1)
    top16 = lax.broadcasted_iota(jnp.int32, (HG_BLK, HG_DK), 0) < half
    lower = (lax.broadcasted_iota(jnp.int32, (HG_BLK, HG_BLK), 1)
             <= lax.broadcasted_iota(jnp.int32, (HG_BLK, HG_BLK), 0))

    def intra_weights(j):
        r0 = j * HG_BLK if isinstance(j, int) else pl.multiple_of(j * HG_BLK, HG_BLK)
        top, bot, blk = pl.ds(r0, half), pl.ds(r0 + half, half), pl.ds(r0, HG_BLK)
        amat = []
        for i, h, sl in heads:
            q_t, q_b, b_t, b_b = qs_ref[i, top, sl], qs_ref[i, bot, sl], bs_ref[i, top, sl], bs_ref[i, bot, sl]
            k16, b16 = ks_ref[i, blk, sl], bs_ref[i, blk, sl]
            a_top = jnp.zeros((half, HG_DK), F32)
            a_bot = jnp.zeros((half, HG_DK), F32)
            for jj in range(half):
                s = jnp.sum(q_t * jnp.exp2(b_t - b16[jj:jj + 1]) * k16[jj:jj + 1], axis=-1, keepdims=True)
                a_top = jnp.where(lane8 == jj, s, a_top)
                jb = half + jj
                s = jnp.sum(q_b * jnp.exp2(b_b - b16[jb:jb + 1]) * k16[jb:jb + 1], axis=-1, keepdims=True)
                a_bot = jnp.where(lane8 == jb, s, a_bot)
            b_mid = b16[half - 1:half]
            k_dec = jnp.where(top16, k16 * jnp.exp2(b_mid - b16), 0.0)
            cross = _mm(q_b * jnp.exp2(b_b - b_mid), k_dec, NT)
            full = jnp.concatenate([a_top[:, :HG_BLK], a_bot[:, :HG_BLK] + cross], axis=0)
            amat.append(jnp.where(lower, full, 0.0))
        return tuple(amat)

    def apply_block(j, amat):
        rows = pl.ds(pl.multiple_of(j * HG_BLK, HG_BLK), HG_BLK)
        idx = range(len(heads))
        qb = [qs_ref[i, rows, sl] for i, h, sl in heads]
        kb = [ks_ref[i, rows, sl] for i, h, sl in heads]
        bb = [bs_ref[i, rows, sl] for i, h, sl in heads]
        vb = [i_ref[i, rows, sl] for i, h, sl in heads]
        st = [st_ref[i, h] for i, h, sl in heads]
        intra = [_mm(amat[n], vb[n]) for n in idx]
        inter = [_mm(qb[n] * jnp.exp2(bb[n]), st[n], NT) for n in idx]
        for n, (i, h, sl) in enumerate(heads):
            b_end = bb[n][HG_BLK - 1:HG_BLK, :]
            st_ref[i, h] = st[n] * jnp.exp2(b_end) + _mm(vb[n], kb[n] * jnp.exp2(b_end - bb[n]), TN)
        for n, (i, h, sl) in enumerate(heads):
            ob_ref[i, rows, sl] = intra[n] + inter[n]

    def step(j, amats):
        apply_block(j - 1, amats)
        return intra_weights(j)

    apply_block(nblk - 1, lax.fori_loop(1, nblk, step, intra_weights(0)))
    nw = nw_ref[...]
    for i, h, sl in heads:
        o_ref[i, :, sl] = _hgrn_post(ob_ref[i, :, sl], g_ref[i, :, sl], nw).astype(o_ref.dtype)

    @pl.when(c == pl.num_programs(1) - 1)
    def _():
        for i, h, sl in heads:
            s_ref[i, h] = st_ref[i, h].T


def _hgrn_prompt(proj, lb, nw, bsz, t, l):
    tt = min(256, t)
    n_t = t // tt
    nb = HG_NB if bsz % HG_NB == 0 else 1
    tril = jnp.asarray(np.kron(np.eye(tt // HG_BLK), np.tril(np.ones((HG_BLK, HG_BLK)))), BF16)
    proj3 = proj.reshape(bsz, t, NCOL)
    col = lambda blk: (lambda b, c: (b, c, blk))
    const = lambda shape: pl.BlockSpec(shape, lambda b, c: (0, 0))
    layer = lambda shape: pl.BlockSpec((None,) + shape, lambda b, c: (l, 0, 0))
    tile = pltpu.VMEM((nb, tt, D_MODEL), F32)
    o, s = pl.pallas_call(
        functools.partial(_hgrn_prompt_kernel, nblk=tt // HG_BLK),
        grid=(bsz // nb, n_t),
        in_specs=[
            pl.BlockSpec((nb, tt, D_MODEL), col(C_Q)), pl.BlockSpec((nb, tt, D_MODEL), col(C_F)),
            pl.BlockSpec((nb, tt, D_MODEL), col(C_I)), pl.BlockSpec((nb, tt, D_MODEL), col(C_GH)),
            layer((1, D_MODEL)), layer((1, 128)), const((tt, tt)),
        ],
        out_specs=[
            pl.BlockSpec((nb, tt, D_MODEL), lambda b, c: (b, c, 0)),
            pl.BlockSpec((nb, HG_HEADS, HG_DK, HG_DV), lambda b, c: (b, 0, 0, 0)),
        ],
        out_shape=[
            jax.ShapeDtypeStruct((bsz, t, D_MODEL), BF16),
            jax.ShapeDtypeStruct((bsz, HG_HEADS, HG_DK, HG_DV), F32),
        ],
        scratch_shapes=[pltpu.VMEM((nb, HG_HEADS, HG_DV, HG_DK), F32), tile, tile, tile, tile],
        compiler_params=_params(("parallel", "arbitrary")),
        name="hgrn_prompt",
    )(proj3, proj3, proj3, proj3, lb, nw, tril)
    return o.reshape(bsz * t, D_MODEL), s


def _hgrn_sample_kernel(q_ref, f_ref, i_ref, g_ref, lb_ref, nw_ref, s_ref, o_ref, so_ref,
                        qs_ref, ks_ref, fs_ref, ob_ref):
    qx, kx, logf = _hgrn_prep(q_ref[...], f_ref[...], lb_ref[...])
    qs_ref[...] = qx
    ks_ref[...] = kx
    fs_ref[...] = jnp.exp(logf)
    ob_ref[...] = jnp.zeros_like(ob_ref)
    rows = lax.broadcasted_iota(jnp.int32, (SAMPLE_TB, 1), 0)

    heads = [slice(h * 128, (h + 1) * 128) for h in range(HG_HEADS)]

    unroll = 2

    def per_seqs(i, carry):
        work = [(i * unroll + j, rows == i * unroll + j, h, sl) for j in range(unroll)
                for h, sl in enumerate(heads)]
        fcol = [_col_bcast(jnp.where(sel, fs_ref[:, sl], 0.0), HG_DV) for b, sel, h, sl in work]
        kv = [_mm(jnp.where(sel, ks_ref[:, sl], 0.0), i_ref[:, sl], TN) for b, sel, h, sl in work]
        s_new = [fcol[n] * s_ref[b, h] + kv[n] for n, (b, sel, h, sl) in enumerate(work)]
        for s, (b, sel, h, sl) in zip(s_new, work):
            so_ref[b, h] = s
        outs = [_mm(jnp.where(sel, qs_ref[:, sl], 0.0), s) for s, (b, sel, h, sl) in zip(s_new, work)]
        for h, sl in enumerate(heads):
            ob_ref[:, sl] += sum(outs[j * HG_HEADS + h] for j in range(unroll))
        return carry

    lax.fori_loop(0, SAMPLE_TB // unroll, per_seqs, 0)
    nw = nw_ref[...]
    for h in range(HG_HEADS):
        sl = slice(h * 128, (h + 1) * 128)
        o_ref[:, sl] = _hgrn_post(ob_ref[:, sl], g_ref[:, sl], nw)


def _hgrn_sample(proj, lb, nw, state, l, new_state):
    bsz = state.shape[1]
    tb = SAMPLE_TB
    col = lambda blk: (lambda i: (i, blk))
    tile = pltpu.VMEM((tb, D_MODEL), F32)
    sspec = pl.BlockSpec((None, tb, HG_HEADS, HG_DK, HG_DV), lambda i: (l, i, 0, 0, 0))
    return _sample_call(
        _hgrn_sample_kernel, "hgrn_sample", bsz,
        [
            pl.BlockSpec((tb, D_MODEL), col(C_Q)), pl.BlockSpec((tb, D_MODEL), col(C_F)),
            pl.BlockSpec((tb, D_MODEL), col(C_I)), pl.BlockSpec((tb, D_MODEL), col(C_GH)),
            pl.BlockSpec((None, 1, D_MODEL), lambda i: (l, 0, 0)),
            pl.BlockSpec((None, 1, 128), lambda i: (l, 0, 0)),
            sspec,
        ],
        (proj, proj, proj, proj, lb, nw, state), sspec, state.shape, [tile, tile, tile, tile], new_state)


def _rw_lora(wdad, w2a2):
    lo = lax.broadcasted_iota(jnp.int32, wdad.shape, 1) < RW_LORA
    lora_w = _mm3(jnp.where(lo, jnp.tanh(wdad), 0.0), w2a2)
    lora_a = _mm3(jnp.where(lo, 0.0, wdad), w2a2)
    return lora_w, lora_a


def _rw_prep_all(items, ones_bd):
    rs, ks, vs, lws, a_sigs, kks = [], [], [], [], [], []
    for r, k0, v, lora_w, lora_a, w0, a0, kk_w, ka_w in items:
        lws.append(-RW_DECAY_SCALE * _sigmoid(w0 + lora_w))
        a_sig = _sigmoid(a0 + lora_a)
        a_sigs.append(a_sig)
        kks.append(k0 * kk_w)
        rs.append(r)
        vs.append(v)
        ks.append(k0 * (1.0 + (a_sig - 1.0) * ka_w))
    sq = _seg_many(ones_bd, [kk * kk for kk in kks])
    kks = [kk / jnp.maximum(jnp.sqrt(s), 1e-12) for kk, s in zip(kks, sq)]
    return rs, ks, vs, lws, [-kk for kk in kks], [kk * a for kk, a in zip(kks, a_sigs)]


def _rw_posts(os_, rs, ks, vs, gs, rk_ws, lnws, lnbs, ones_bd):
    idx = range(len(os_))
    sums = _seg_many(ones_bd, list(os_) + [rs[i] * ks[i] * rk_ws[i] for i in idx])
    ds = [os_[i] - sums[i] * (1.0 / RW_N) for i in idx]
    bonus = sums[len(os_):]
    var = [s * (1.0 / RW_N) for s in _seg_many(ones_bd, [d * d for d in ds])]
    return [(ds[i] * lax.rsqrt(var[i] + RW_LN_EPS) * lnws[i] + lnbs[i] + bonus[i] * vs[i]) * _silu(gs[i])
            for i in idx]


def _rw_chunks(rs, ks, vs, lws, avs, bvs, sbds, tril):
    n = RW_CHUNK
    pairs = range(len(rs))
    lane = lax.broadcasted_iota(jnp.int32, (n, 128), 1)
    row = lax.broadcasted_iota(jnp.int32, (n, 128), 0)
    m0 = lane < RW_N
    col = jnp.where(m0, lane, lane - RW_N)
    strict = col < row
    incl = col <= row
    eye = (col == row).astype(F32)

    def bd(x):
        xb = x.astype(BF16)
        zero = jnp.zeros_like(xb)
        return jnp.concatenate([jnp.where(m0, xb, zero), jnp.where(m0, zero, xb)], axis=0)

    cls = [_mm_exact(tril, lws[q]) for q in pairs]
    a_h = [avs[q] * jnp.exp(cls[q] - lws[q]) for q in pairs]
    r_h = [rs[q] * jnp.exp(cls[q]) for q in pairs]
    b_c = [bvs[q] * jnp.exp(-cls[q]) for q in pairs]
    k_c = [ks[q] * jnp.exp(-cls[q]) for q in pairs]
    lhs = [jnp.concatenate([a_h[q], r_h[q]], axis=0) for q in pairs]
    from_state = [_mm(lhs[q], sbds[q], NT) for q in pairs]
    m_bk = [_mm(lhs[q], jnp.concatenate([bd(b_c[q]), bd(k_c[q])], axis=0), NT) for q in pairs]
    m_ab = [jnp.where(strict, m[:n, :128], 0.0) for m in m_bk]
    m_ak = [jnp.where(strict, m[:n, 128:], 0.0) for m in m_bk]
    m_rb = [jnp.where(incl, m[n:, :128], 0.0) for m in m_bk]
    m_rk = [jnp.where(incl, m[n:, 128:], 0.0) for m in m_bk]
    ps = [_mm(m, bd(m)) for m in m_ab]
    xs = [eye + m for m in m_ab]
    for level in range(1, 5):
        both = [_mm(jnp.concatenate([ps[q], xs[q]], axis=0), bd(ps[q])) for q in pairs]
        ps = [m[:n] for m in both]
        xs = [xs[q] + both[q][n:] for q in pairs]
    xs = [xs[q] + _mm(xs[q], bd(ps[q])) for q in pairs]
    from_v = [_mm(jnp.concatenate([m_ak[q], m_rk[q]], axis=0), bd(vs[q])) for q in pairs]
    w_all = [from_state[q][:n] + from_v[q][:n] for q in pairs]
    u_all = [_mm(xs[q], bd(w_all[q])) for q in pairs]
    o_all = [from_state[q][n:] + from_v[q][n:] + _mm(m_rb[q], bd(u_all[q])) for q in pairs]

    vi = lax.broadcasted_iota(jnp.int32, (128, 128), 0)
    ki = lax.broadcasted_iota(jnp.int32, (128, 128), 1)
    same_head = (vi < RW_N) == (ki < RW_N)
    new = []
    for q in pairs:
        cl_end = cls[q][n - 1:n, :]
        e_end = jnp.exp(cl_end - cls[q])
        upd = _mm(jnp.concatenate([u_all[q], vs[q]], axis=0),
                  jnp.concatenate([bvs[q] * e_end, ks[q] * e_end], axis=0), TN)
        new.append(sbds[q] * jnp.exp(cl_end) + jnp.where(same_head, upd, 0.0))
    return o_all, new


def _rwkv_prompt_kernel(r_ref, k_ref, v_ref, g_ref, wdad_ref, mu_ref, w0_ref, a0_ref, kkw_ref, kaw_ref, rkw_ref,
                        lnw_ref, lnb_ref, w2a2_ref, tril_ref, ones_ref, o_ref, s_ref,
                        sbd_ref, pr_ref, pk_ref, pv_ref, pwd_ref, lw_ref, la_ref):
    c = pl.program_id(1)
    n = RW_CHUNK
    nb = r_ref.shape[0]

    @pl.when(c == 0)
    def _():
        sbd_ref[...] = jnp.zeros_like(sbd_ref)
        pr_ref[...] = jnp.zeros_like(pr_ref)
        pk_ref[...] = jnp.zeros_like(pk_ref)
        pv_ref[...] = jnp.zeros_like(pv_ref)
        pwd_ref[...] = jnp.zeros_like(pwd_ref)

    row = lax.broadcasted_iota(jnp.int32, (n, 128), 0)

    def shift(z_ref, prev_ref, i, sl, mu_lo):
        z = z_ref[i, :, sl]
        zp = jnp.where(row == 0, prev_ref[i, :, sl], pltpu.roll(z, 1, axis=0))
        prev_ref[i, :, sl] = z[n - 1:n, :]
        return z + (zp - z) * mu_ref[:, mu_lo:mu_lo + 128]

    for i in range(nb):
        lora_w, lora_a = _rw_lora(shift(wdad_ref, pwd_ref, i, slice(0, 128), 3 * D_MODEL), w2a2_ref[...])
        lw_ref[i] = lora_w
        la_ref[i] = lora_a
    ones_bd = ones_ref[...]
    lanes = [slice(q * 128, (q + 1) * 128) for q in range(RW_HEADS // 2)]
    pairs = [(i, q, sl) for i in range(nb) for q, sl in enumerate(lanes)]
    rs, ks, vs, lws, avs, bvs = _rw_prep_all(
        [(shift(r_ref, pr_ref, i, sl, q * 128), shift(k_ref, pk_ref, i, sl, D_MODEL + q * 128),
          shift(v_ref, pv_ref, i, sl, 2 * D_MODEL + q * 128), lw_ref[i, :, sl], la_ref[i, :, sl],
          w0_ref[:, sl], a0_ref[:, sl], kkw_ref[:, sl], kaw_ref[:, sl]) for i, q, sl in pairs], ones_bd)
    o_all, sbd_new = _rw_chunks(rs, ks, vs, lws, avs, bvs, [sbd_ref[i, q] for i, q, sl in pairs], tril_ref[...])
    for (i, q, sl), new in zip(pairs, sbd_new):
        sbd_ref[i, q] = new
    outs = _rw_posts(o_all, rs, ks, vs, [g_ref[i, :, sl] for i, q, sl in pairs],
                     [rkw_ref[:, sl] for i, q, sl in pairs], [lnw_ref[:, sl] for i, q, sl in pairs],
                     [lnb_ref[:, sl] for i, q, sl in pairs], ones_bd)
    for (i, q, sl), out in zip(pairs, outs):
        o_ref[i, :, sl] = out.astype(o_ref.dtype)

    @pl.when(c == pl.num_programs(1) - 1)
    def _():
        s_ref[...] = sbd_ref[...]


def _rw_consts():
    tril = jnp.asarray(np.tril(np.ones((RW_CHUNK, RW_CHUNK))), BF16)
    ones_bd = jnp.asarray(np.kron(np.eye(2), np.ones((RW_N, RW_N))), BF16)
    return tril, ones_bd


def _rwkv_prompt(proj, p, bsz, t, l):
    n = RW_CHUNK
    n_t = t // n
    nb = RW_NB if bsz % RW_NB == 0 else 1
    tril, ones_bd = _rw_consts()
    proj3 = proj.reshape(bsz, t, NCOL)
    col = lambda blk: (lambda b, c: (b, c, blk))
    const = lambda shape: pl.BlockSpec(shape, lambda b, c: (0, 0))
    layer = lambda shape: pl.BlockSpec((None,) + shape, lambda b, c: (l, 0, 0))
    vec = layer((1, D_MODEL))
    prev = pltpu.VMEM((nb, 1, D_MODEL), F32)
    tile = pltpu.VMEM((nb, n, D_MODEL), F32)
    o, sbd = pl.pallas_call(
        _rwkv_prompt_kernel,
        grid=(bsz // nb, n_t),
        in_specs=[
            pl.BlockSpec((nb, n, D_MODEL), col(C_R)), pl.BlockSpec((nb, n, D_MODEL), col(C_K)),
            pl.BlockSpec((nb, n, D_MODEL), col(C_V)), pl.BlockSpec((nb, n, D_MODEL), col(C_GR)),
            pl.BlockSpec((nb, n, 128), lambda b, c: (b, c, C_WDAD)),
            layer((1, RW_SHIFT)),
            vec, vec, vec, vec, vec, vec, vec,
            layer((128, D_MODEL)), const((n, n)), const((128, 128)),
        ],
        out_specs=[
            pl.BlockSpec((nb, n, D_MODEL), lambda b, c: (b, c, 0)),
            pl.BlockSpec((nb, RW_HEADS // 2, 128, 128), lambda b, c: (b, 0, 0, 0)),
        ],
        out_shape=[
            jax.ShapeDtypeStruct((bsz, t, D_MODEL), BF16),
            jax.ShapeDtypeStruct((bsz, RW_HEADS // 2, 128, 128), F32),
        ],
        scratch_shapes=[pltpu.VMEM((nb, RW_HEADS // 2, 128, 128), F32), prev, prev, prev,
                        pltpu.VMEM((nb, 1, 128), F32), tile, tile],
        compiler_params=_params(("parallel", "arbitrary")),
        name="rwkv_prompt",
    )(proj3, proj3, proj3, proj3, proj3, p["mu"], p["w0"], p["a0"], p["kk"], p["ka"], p["rk"], p["lnw"], p["lnb"],
      p["w2a2"], tril, ones_bd)
    s6 = sbd.reshape(bsz, RW_HEADS // 2, 2, RW_N, 2, RW_N)
    state = jnp.stack([s6[:, :, 0, :, 0, :], s6[:, :, 1, :, 1, :]], axis=2)
    return o.reshape(bsz * t, D_MODEL), state.reshape(bsz, RW_HEADS, RW_N, RW_N)


def _rwkv_sample_kernel(r_ref, k_ref, v_ref, g_ref, wdad_ref, sh_ref, mu_ref, w0_ref, a0_ref, kkw_ref, kaw_ref,
                        rkw_ref, lnw_ref, lnb_ref, w2a2_ref, ones_ref, s_ref, o_ref, so_ref,
                        rs_ref, ks_ref, vs_ref, ob_ref, kr_ref, wr_ref, ar_ref, br_ref):
    ones_bd = ones_ref[...]
    tb = SAMPLE_TB

    def store_rows(ref, sl, x):
        for i in range(tb):
            ref[i, :, sl] = x[i:i + 1, :]

    def shift(z, lo, width):
        return z + (sh_ref[:, lo:lo + width] - z) * mu_ref[:, lo:lo + width]

    wdad = shift(wdad_ref[...], 3 * D_MODEL, 128)
    lora_w, lora_a = _rw_lora(wdad, w2a2_ref[...])
    pairs = [slice(q * 128, (q + 1) * 128) for q in range(RW_HEADS // 2)]
    rs, ks, vs, lws, avs, bvs = _rw_prep_all(
        [(shift(r_ref[:, sl], q * 128, 128), shift(k_ref[:, sl], D_MODEL + q * 128, 128),
          shift(v_ref[:, sl], 2 * D_MODEL + q * 128, 128), lora_w[:, sl], lora_a[:, sl], w0_ref[:, sl],
          a0_ref[:, sl], kkw_ref[:, sl], kaw_ref[:, sl]) for q, sl in enumerate(pairs)], ones_bd)
    for q, sl in enumerate(pairs):
        rs_ref[:, sl] = rs[q]
        ks_ref[:, sl] = ks[q]
        vs_ref[:, sl] = vs[q]
        store_rows(kr_ref, sl, ks[q])
        store_rows(wr_ref, sl, jnp.exp(lws[q]))
        store_rows(ar_ref, sl, avs[q])
        store_rows(br_ref, sl, bvs[q])
    ob_ref[...] = jnp.zeros_like(ob_ref)
    rows = lax.broadcasted_iota(jnp.int32, (tb, 1), 0)

    hs = range(RW_HEADS)
    lanes = [slice(h * RW_N, (h + 1) * RW_N) for h in hs]
    unroll = 2

    def per_seqs(i, carry):
        bs = [i * unroll + j for j in range(unroll)]
        work = [(b, rows == b, h) for b in bs for h in hs]

        def row_of(ref, b, h):
            pair = slice((h // 2) * 128, (h // 2 + 1) * 128)
            return ref[b, :, pair][:, (h % 2) * RW_N:(h % 2 + 1) * RW_N]

        vcol = [_col_bcast(jnp.where(sel, vs_ref[:, lanes[h]], 0.0), RW_N) for b, sel, h in work]
        s_old = [s_ref[b, h] for b, sel, h in work]
        sa = [jnp.sum(s * row_of(ar_ref, b, h), axis=-1, keepdims=True) for s, (b, sel, h) in zip(s_old, work)]
        s_new = [s_old[n] * row_of(wr_ref, b, h) + sa[n] * row_of(br_ref, b, h) + vcol[n] * row_of(kr_ref, b, h)
                 for n, (b, sel, h) in enumerate(work)]
        for s, (b, sel, h) in zip(s_new, work):
            so_ref[b, h] = s
        outs = [_mm(jnp.where(sel, rs_ref[:, lanes[h]], 0.0), s, NT) for s, (b, sel, h) in zip(s_new, work)]
        for h in hs:
            ob_ref[:, lanes[h]] += sum(outs[j * RW_HEADS + h] for j in range(unroll))
        return carry

    lax.fori_loop(0, tb // unroll, per_seqs, 0)
    outs = _rw_posts([ob_ref[:, sl] for sl in pairs], rs, ks, vs, [g_ref[:, sl] for sl in pairs],
                     [rkw_ref[:, sl] for sl in pairs], [lnw_ref[:, sl] for sl in pairs],
                     [lnb_ref[:, sl] for sl in pairs], ones_bd)
    for sl, out in zip(pairs, outs):
        o_ref[:, sl] = out


def _rwkv_sample(proj, p, shift_state, state, l, new_state):
    bsz = state.shape[1]
    tb = SAMPLE_TB
    _, ones_bd = _rw_consts()
    col = lambda blk: (lambda i: (i, blk))
    layer = lambda shape: pl.BlockSpec((None,) + shape, lambda i: (l, 0, 0))
    vec = layer((1, D_MODEL))
    tile = pltpu.VMEM((tb, D_MODEL), F32)
    sspec = pl.BlockSpec((None, tb, RW_HEADS, RW_N, RW_N), lambda i: (l, i, 0, 0, 0))
    return _sample_call(
        _rwkv_sample_kernel, "rwkv_sample", bsz,
        [
            pl.BlockSpec((tb, D_MODEL), col(C_R)), pl.BlockSpec((tb, D_MODEL), col(C_K)),
            pl.BlockSpec((tb, D_MODEL), col(C_V)), pl.BlockSpec((tb, D_MODEL), col(C_GR)),
            pl.BlockSpec((tb, 128), lambda i: (i, C_WDAD)),
            pl.BlockSpec((None, tb, RW_SHIFT), lambda i: (l, i, 0)),
            layer((1, RW_SHIFT)),
            vec, vec, vec, vec, vec, vec, vec,
            layer((128, D_MODEL)),
            pl.BlockSpec((128, 128), lambda i: (0, 0)),
            sspec,
        ],
        (proj, proj, proj, proj, proj, shift_state, p["mu"], p["w0"], p["a0"], p["kk"], p["ka"], p["rk"],
         p["lnw"], p["lnb"], p["w2a2"], ones_bd, state),
        sspec, state.shape, [tile] * 4 + [pltpu.VMEM((tb, 1, D_MODEL), F32)] * 4, new_state)


def _mb_post(y, xs, z, d_rep, nw):
    y = (y + d_rep * xs) * _silu(z)
    ms = jnp.mean(y * y, axis=-1, keepdims=True)
    return y * lax.rsqrt(ms + RMS_EPS) * nw


def _ssd_prompt_kernel(x_ref, bc_ref, z_ref, dt_ref, cw_ref, cb_ref, dtb_ref, alog_ref, d_ref, nw_ref, tril_ref,
                       exp_ref, o_ref, s_ref, h_ref, px_ref, pbc_ref):
    c = pl.program_id(1)
    n = MB_CHUNK

    @pl.when(c == 0)
    def _():
        h_ref[...] = jnp.zeros_like(h_ref)
        px_ref[...] = jnp.zeros_like(px_ref)
        pbc_ref[...] = jnp.zeros_like(pbc_ref)

    def conv(u_ref, prev_ref, sl, wsl):
        u = u_ref[:, sl]
        prev = prev_ref[:, sl]
        w = cw_ref[:, wsl]
        row = lax.broadcasted_iota(jnp.int32, (8, u.shape[1]), 0)
        acc = u * w[MB_CONV - 1:MB_CONV, :] + cb_ref[:, wsl]
        for s in range(1, MB_CONV):
            us = pltpu.roll(u, s, axis=0)
            top = jnp.where(row < s, pltpu.roll(prev, s, axis=0), us[0:8])
            us = jnp.concatenate([top, us[8:]], axis=0)
            acc = acc + us * w[MB_CONV - 1 - s:MB_CONV - s, :]
        prev_ref[:, sl] = u[n - 8:n, :]
        return _silu(acc)

    tril = tril_ref[...]
    ti = lax.broadcasted_iota(jnp.int32, (n, n), 0)
    si = lax.broadcasted_iota(jnp.int32, (n, n), 1)
    incl = si <= ti
    lane = lax.broadcasted_iota(jnp.int32, (n, 4 * MB_P), 1)
    nbc = MB_GROUPS * MB_N
    groups = range(MB_GROUPS)
    gx = [slice(g * 256, (g + 1) * 256) for g in groups]

    spread = exp_ref[...]
    dt_c = _softplus(dt_ref[...] + dtb_ref[...])
    a_c = dt_c * (-jnp.exp(alog_ref[...]))
    cs_c = _mm_exact(tril, a_c)
    cs_rows = _mm_exact((ti <= si).astype(BF16), a_c, TN, x_is_lhs=True)
    dt_all = _mm_exact(spread, dt_c, x_is_lhs=True)
    cs_all = _mm_exact(spread, cs_c, x_is_lhs=True)

    xs = [conv(x_ref, px_ref, gx[g], gx[g]) for g in groups]
    bm = [conv(bc_ref, pbc_ref, slice(g * MB_N, (g + 1) * MB_N),
               slice(D_MODEL + g * MB_N, D_MODEL + (g + 1) * MB_N)) for g in groups]
    cm = [conv(bc_ref, pbc_ref, slice(nbc + g * MB_N, nbc + (g + 1) * MB_N),
               slice(D_MODEL + nbc + g * MB_N, D_MODEL + nbc + (g + 1) * MB_N)) for g in groups]
    xdt = [xs[g] * dt_all[:, gx[g]] for g in groups]
    cs = [cs_all[:, gx[g]] for g in groups]
    gmat = [_mm(cm[g], bm[g], NT) for g in groups]
    h_all = [h_ref[g] for g in groups]
    y = [jnp.exp(cs[g]) * _mm(cm[g], h_all[g], NT) for g in groups]
    for g in groups:
        for e in range(4):
            hd = slice(e * MB_P, (e + 1) * MB_P)
            hrow = slice(4 * g + e, 4 * g + e + 1)
            lmat = jnp.where(incl, jnp.exp(cs[g][:, hd] - cs_rows[hrow, :]), 0.0)
            me = (lane >= e * MB_P) & (lane < (e + 1) * MB_P)
            y[g] = y[g] + _mm(gmat[g] * lmat, jnp.where(me, xdt[g], 0.0))
    for g in groups:
        cs_end = cs[g][n - 1:n, :]
        upd = _mm(xdt[g] * jnp.exp(cs_end - cs[g]), bm[g], TN)
        decay = [jnp.exp(cs_rows[4 * g + e:4 * g + e + 1, n - 1:n]) for e in range(4)]
        h_ref[g] = jnp.concatenate([h_all[g][e * MB_P:(e + 1) * MB_P] * decay[e] for e in range(4)], axis=0) + upd
    for g in groups:
        o_ref[:, gx[g]] = _mb_post(y[g], xs[g], z_ref[:, gx[g]], d_ref[:, gx[g]],
                                   nw_ref[:, gx[g]]).astype(o_ref.dtype)

    @pl.when(c == pl.num_programs(1) - 1)
    def _():
        for g in range(MB_GROUPS):
            for e in range(4):
                s_ref[0, 4 * g + e] = h_ref[g, e * MB_P:(e + 1) * MB_P, :]


def _head_spread():
    m = np.zeros((128, D_MODEL), np.float32)
    for h in range(MB_HEADS):
        m[h, h * MB_P:(h + 1) * MB_P] = 1.0
    return jnp.asarray(m, BF16)


def _ssd_prompt(proj, p, bsz, t, l):
    n = MB_CHUNK
    n_t = t // n
    tril = jnp.asarray(np.tril(np.ones((n, n))), BF16)
    col = lambda blk: (lambda b, c: (b * n_t + c, blk))
    const = lambda shape: pl.BlockSpec(shape, lambda b, c: (0, 0))
    layer = lambda shape: pl.BlockSpec((None,) + shape, lambda b, c: (l, 0, 0))
    vec = layer((1, D_MODEL))
    return pl.pallas_call(
        _ssd_prompt_kernel,
        grid=(bsz, n_t),
        in_specs=[
            pl.BlockSpec((n, D_MODEL), col(C_X)), pl.BlockSpec((n, D_MODEL), col(C_BC)),
            pl.BlockSpec((n, D_MODEL), col(C_Z)), pl.BlockSpec((n, 128), col(C_DT)),
            layer((MB_CONV, MB_XBC)), layer((1, MB_XBC)),
            layer((1, 128)), layer((1, 128)), vec, vec,
            const((n, n)), const((128, D_MODEL)),
        ],
        out_specs=[
            pl.BlockSpec((n, D_MODEL), lambda b, c: (b * n_t + c, 0)),
            pl.BlockSpec((1, MB_HEADS, MB_P, MB_N), lambda b, c: (b, 0, 0, 0)),
        ],
        out_shape=[
            jax.ShapeDtypeStruct((bsz * t, D_MODEL), BF16),
            jax.ShapeDtypeStruct((bsz, MB_HEADS, MB_P, MB_N), F32),
        ],
        scratch_shapes=[pltpu.VMEM((MB_GROUPS, 4 * MB_P, MB_N), F32), pltpu.VMEM((8, D_MODEL), F32),
                        pltpu.VMEM((8, D_MODEL), F32)],
        compiler_params=_params(("parallel", "arbitrary")),
        name="ssd_prompt",
    )(proj, proj, proj, proj, p["cw"], p["cb"], p["dtb"], p["alog"], p["d"], p["nw"], tril, _head_spread())


def _ssd_sample_kernel(xbc_ref, z_ref, dt_ref, cs_ref, cw_ref, cb_ref, dtb_ref, alog_ref, d_ref, nw_ref, exp_ref,
                       s_ref, o_ref, so_ref, xs_ref, xd_ref, bs_ref, cms_ref, dec_ref, ob_ref):
    tb = SAMPLE_TB
    w = cw_ref[...]
    acc = xbc_ref[...] * w[MB_CONV - 1:MB_CONV, :] + cb_ref[...]
    for i in range(MB_CONV - 1):
        acc = acc + cs_ref[i] * w[i:i + 1, :]
    xbc = _silu(acc)
    xs = xbc[:, :D_MODEL]
    dt = _softplus(dt_ref[...] + dtb_ref[...])
    xs_ref[...] = xs
    xd_ref[...] = xs * _mm_exact(exp_ref[...], dt, x_is_lhs=True)
    bs_ref[...] = xbc[:, D_MODEL:D_MODEL + MB_GROUPS * MB_N]
    cms_ref[...] = xbc[:, D_MODEL + MB_GROUPS * MB_N:]
    dec = jnp.exp(dt * (-jnp.exp(alog_ref[...])))
    for i in range(tb):
        dec_ref[i] = dec[i:i + 1, :]
    ob_ref[...] = jnp.zeros_like(ob_ref)
    rows = lax.broadcasted_iota(jnp.int32, (tb, 1), 0)

    hs = range(MB_HEADS)
    lanes = [slice(h * MB_P, (h + 1) * MB_P) for h in hs]
    grp = [slice((h // 4) * MB_N, (h // 4 + 1) * MB_N) for h in hs]
    unroll = 2

    def per_seqs(i, carry):
        work = [(i * unroll + j, rows == i * unroll + j, h) for j in range(unroll) for h in hs]

        def dec_of(b, h):
            return dec_ref[b][:, h:h + 1]

        upd = [_mm(jnp.where(sel, xd_ref[:, lanes[h]], 0.0), bs_ref[:, grp[h]], TN) for b, sel, h in work]
        h_new = [dec_of(b, h) * s_ref[b, h] + upd[n] for n, (b, sel, h) in enumerate(work)]
        for s, (b, sel, h) in zip(h_new, work):
            so_ref[b, h] = s
        outs = [_mm(jnp.where(sel, cms_ref[:, grp[h]], 0.0), s, NT) for s, (b, sel, h) in zip(h_new, work)]
        for h in hs:
            ob_ref[:, lanes[h]] += sum(outs[j * MB_HEADS + h] for j in range(unroll))
        return carry

    lax.fori_loop(0, tb // unroll, per_seqs, 0)
    for g in range(MB_GROUPS):
        sl = slice(g * 256, (g + 1) * 256)
        o_ref[:, sl] = _mb_post(ob_ref[:, sl], xs_ref[:, sl], z_ref[:, sl], d_ref[:, sl], nw_ref[:, sl])


def _ssd_sample(proj, p, conv_state, state, l, new_state):
    bsz = state.shape[1]
    tb = SAMPLE_TB
    layer = lambda shape: pl.BlockSpec((None,) + shape, lambda i: (l, 0, 0))
    vec = layer((1, D_MODEL))
    tile = pltpu.VMEM((tb, D_MODEL), F32)
    half = pltpu.VMEM((tb, MB_GROUPS * MB_N), F32)
    sspec = pl.BlockSpec((None, tb, MB_HEADS, MB_P, MB_N), lambda i: (l, i, 0, 0, 0))
    return _sample_call(
        _ssd_sample_kernel, "ssd_sample", bsz,
        [
            pl.BlockSpec((tb, MB_XBC), lambda i: (i, C_X // 2)),
            pl.BlockSpec((tb, D_MODEL), lambda i: (i, C_Z)),
            pl.BlockSpec((tb, 128), lambda i: (i, C_DT)),
            pl.BlockSpec((None, MB_CONV - 1, tb, MB_XBC), lambda i: (l, 0, i, 0)),
            layer((MB_CONV, MB_XBC)), layer((1, MB_XBC)),
            layer((1, 128)), layer((1, 128)), vec, vec,
            pl.BlockSpec((128, D_MODEL), lambda i: (0, 0)),
            sspec,
        ],
        (proj, proj, proj, conv_state, p["cw"], p["cb"], p["dtb"], p["alog"], p["d"], p["nw"], _head_spread(),
         state),
        sspec, state.shape, [tile, tile, half, half, pltpu.VMEM((tb, 1, 128), F32), tile], new_state)


def _prep_w_in(w_in):
    o = np.cumsum([0, 1024, 1024, 1024, 1024, RW_SHIFT, 1024, 1024, MB_XBC, MB_HEADS, 3 * D_MODEL])
    zrw, grw, zmb, xbc, dtc, gate = o[4], o[5], o[6], o[7], o[8], o[9]
    parts = [(0, zrw), (zrw, zrw + 3 * D_MODEL), (grw, zmb), (xbc, dtc), (zmb, xbc), (gate, o[10]),
             (zrw + 3 * D_MODEL, grw), (dtc, gate)]
    out = jnp.zeros(w_in.shape[:-1] + (NCOL,), BF16)
    at = 0
    for lo, hi in parts:
        out = lax.dynamic_update_slice_in_dim(out, w_in[..., lo:hi].astype(BF16), at, axis=-1)
        at += hi - lo
    return out


def _rep(v):
    return jnp.repeat(v, MB_P, axis=-1)[:, None, :]


def _pad_heads(v):
    return jnp.pad(v, ((0, 0), (0, 128 - MB_HEADS)))[:, None, :]


def kernel(x_prompt, x_sample, state_hgrn, state_rwkv, state_rwkv_shift, state_ssm, state_conv, norm_w, w_in, hg_lb, hg_norm_w, rw_mu, rw_w0, rw_w2, rw_a0, rw_a2, rw_k_k, rw_k_a, rw_r_k, rw_ln_w, rw_ln_b, mb_conv_w, mb_conv_b, mb_dt_bias, mb_A_log, mb_D, mb_norm_w, w_o_hg, w_o_rw, w_o_mb, w_out, final_norm_w):
    bp, t, _ = x_prompt.shape
    bs = x_sample.shape[0]
    sm = jax.nn.softmax(hg_lb.astype(F32), axis=0)
    row = lambda a: a[:, None, :]
    lbs = row(jnp.cumsum(sm, axis=0) - sm[0:1])
    w_in_r = _prep_w_in(w_in)
    rw_p = dict(mu=row(rw_mu), w0=row(rw_w0), a0=row(rw_a0), kk=row(rw_k_k), ka=row(rw_k_a),
                rk=rw_r_k.reshape(DEPTH, 1, D_MODEL), lnw=row(rw_ln_w), lnb=row(rw_ln_b),
                w2a2=jnp.concatenate([rw_w2, rw_a2], axis=1))
    mb_p = dict(cw=mb_conv_w, cb=row(mb_conv_b), dtb=_pad_heads(mb_dt_bias), alog=_pad_heads(mb_A_log), d=_rep(mb_D),
                nw=row(mb_norm_w))
    nw_in, nw_hg = row(norm_w), row(hg_norm_w)
    whg, wrw, wmb, wout = (w.astype(BF16) for w in (w_o_hg, w_o_rw, w_o_mb, w_out))
    fw = final_norm_w[None, :]
    conv_t = jnp.swapaxes(state_conv, 1, 2)
    lora = slice(C_WDAD * 128, (C_WDAD + 1) * 128)

    hp = x_prompt.reshape(bp * t, D_MODEL)
    hs = x_sample.reshape(bs, D_MODEL)
    p_states, s_small = [], []
    s_hg = s_rw = s_ssm = None
    yp = ys = None
    for l in range(DEPTH):
        final = l == DEPTH - 1

        proj = _inproj(hp, nw_in, w_in_r, l)
        o_hg, p_hg = _hgrn_prompt(proj, lbs, nw_hg, bp, t, l)
        o_rw, p_rw = _rwkv_prompt(proj, rw_p, bp, t, l)
        o_mb, p_ssm = _ssd_prompt(proj, mb_p, bp, t, l)
        proj3 = proj.reshape(bp, t, NCOL)
        p_shift = jnp.concatenate([proj3[:, -1, C_R * 1024:C_GR * 1024], proj3[:, -1, lora]], axis=-1)
        p_conv = proj3[:, t - (MB_CONV - 1):, C_X * 1024:C_Z * 1024]
        res = _merge(o_hg, o_rw, o_mb, proj, hp, whg, wrw, wmb, wout, fw, final, l)
        hp = res[0]
        if final:
            yp = res[1]
        p_states.append((p_hg, p_rw, p_shift, p_ssm, p_conv))

        proj = _inproj(hs, nw_in, w_in_r, l)
        o_hg, s_hg = _hgrn_sample(proj, lbs, nw_hg, state_hgrn, l, s_hg)
        o_rw, s_rw = _rwkv_sample(proj, rw_p, state_rwkv_shift, state_rwkv, l, s_rw)
        o_mb, s_ssm = _ssd_sample(proj, mb_p, conv_t, state_ssm, l, s_ssm)
        s_shift = jnp.concatenate([proj[:, C_R * 1024:C_GR * 1024], proj[:, lora]], axis=-1)
        s_conv = jnp.concatenate([state_conv[l][:, 1:], proj[:, None, C_X * 1024:C_Z * 1024]], axis=1)
        res = _merge(o_hg, o_rw, o_mb, proj, hs, whg, wrw, wmb, wout, fw, final, l)
        hs = res[0]
        if final:
            ys = res[1]
        s_small.append((s_shift, s_conv))

    stack = lambda states, i: jnp.stack([s[i] for s in states])
    return (yp.reshape(bp, t, D_MODEL), ys.reshape(bs, 1, D_MODEL),
            *[stack(p_states, i) for i in range(5)],
            s_hg, s_rw, stack(s_small, 0), s_ssm, stack(s_small, 1))
```

```python
import functools

import numpy as np
import jax
import jax.numpy as jnp
from jax import lax
from jax.experimental import pallas as pl
from jax.experimental.pallas import tpu as pltpu

F32 = jnp.float32
BF16 = jnp.bfloat16
HI = lax.Precision.HIGHEST

D_MODEL = 1024
DEPTH = 2
HG_HEADS, HG_DK, HG_DV = 8, 128, 128
HG_F_MIN = 1e-20
LOG2E = 1.4426950408889634
RW_HEADS, RW_N = 16, 64
RW_LORA = 64
RW_SHIFT = 3 * D_MODEL + 2 * RW_LORA
RW_LN_EPS = 64e-5
RW_DECAY_SCALE = float(np.exp(-0.5))
MB_HEADS, MB_P, MB_GROUPS, MB_N, MB_CONV = 16, 64, 4, 128, 4
MB_XBC = D_MODEL + 2 * MB_GROUPS * MB_N
RMS_EPS = 1e-6

C_Q, C_F, C_I, C_GH, C_R, C_K, C_V, C_GR, C_X, C_BC, C_Z, C_G0, C_G1, C_G2 = range(14)
C_WDAD = 14 * 8
C_DT = 14 * 8 + 1
NCOL = 14 * 1024 + 2 * 128

HG_BLK = 16
HG_NB = 2
RW_CHUNK = 64
RW_NB = 4
MB_CHUNK = 64
MB_NB = 4
SAMPLE_TB = 8

VMEM_LIMIT = 48 * 1024 * 1024

NN = (((1,), (0,)), ((), ()))
NT = (((1,), (1,)), ((), ()))
TN = (((0,), (0,)), ((), ()))


def _mm(a, b, dims=NN, hi=False):
    if hi:
        return lax.dot_general(a.astype(F32), b.astype(F32), dims, precision=HI, preferred_element_type=F32)
    return lax.dot_general(a.astype(BF16), b.astype(BF16), dims, preferred_element_type=F32)


def _mm_exact(e, x, dims=NN, x_is_lhs=False, passes=3):
    e = e.astype(BF16)
    acc = None
    for _ in range(passes):
        p = x.astype(BF16)
        x = x - p.astype(F32)
        t = (lax.dot_general(p, e, dims, preferred_element_type=F32) if x_is_lhs
             else lax.dot_general(e, p, dims, preferred_element_type=F32))
        acc = t if acc is None else acc + t
    return acc


def _mm3(a, b, dims=NN):
    a_hi = a.astype(BF16)
    a_lo = (a - a_hi.astype(F32)).astype(BF16)
    b_hi = b.astype(BF16)
    b_lo = (b - b_hi.astype(F32)).astype(BF16)
    dot = lambda x, y: lax.dot_general(x, y, dims, preferred_element_type=F32)
    return dot(a_hi, b_hi) + (dot(a_hi, b_lo) + dot(a_lo, b_hi))


def _seg_many(ones_bd, xs):
    m = xs[0].shape[0]
    out = _mm_exact(ones_bd, jnp.concatenate(xs, axis=0) if len(xs) > 1 else xs[0], x_is_lhs=True, passes=2)
    return [out[i * m:(i + 1) * m] for i in range(len(xs))]


def _sigmoid(x):
    return jax.nn.sigmoid(x)


def _silu(x):
    return x * jax.nn.sigmoid(x)


def _softplus(x):
    return jnp.maximum(x, 0.0) + jnp.log1p(jnp.exp(-jnp.abs(x)))


def _col_bcast(rows, width):
    ones = jnp.ones((rows.shape[0], width), BF16)
    p1 = rows.astype(BF16)
    r1 = rows - p1.astype(F32)
    p2 = r1.astype(BF16)
    p3 = (r1 - p2.astype(F32)).astype(BF16)
    out = lax.dot_general(p1, ones, TN, preferred_element_type=F32)
    out = out + lax.dot_general(p2, ones, TN, preferred_element_type=F32)
    return out + lax.dot_general(p3, ones, TN, preferred_element_type=F32)


def _params(sem):
    return pltpu.CompilerParams(dimension_semantics=sem, vmem_limit_bytes=VMEM_LIMIT)


def _inproj_kernel(x_ref, nw_ref, w_ref, o_ref, xn_ref):
    @pl.when(pl.program_id(1) == 0)
    def _():
        x = x_ref[...]
        ms = jnp.mean(x * x, axis=-1, keepdims=True)
        xn_ref[...] = (x * lax.rsqrt(ms + RMS_EPS) * nw_ref[...]).astype(BF16)

    o_ref[...] = jnp.dot(xn_ref[...], w_ref[...], preferred_element_type=F32)


def _inproj(x2d, nw, w, l):
    m = x2d.shape[0]
    tm = min(2048, m)
    tn = 768
    return pl.pallas_call(
        _inproj_kernel,
        grid=(m // tm, NCOL // tn),
        in_specs=[
            pl.BlockSpec((tm, D_MODEL), lambda i, j: (i, 0)),
            pl.BlockSpec((None, 1, D_MODEL), lambda i, j: (l, 0, 0)),
            pl.BlockSpec((None, D_MODEL, tn), lambda i, j: (l, 0, j)),
        ],
        out_specs=pl.BlockSpec((tm, tn), lambda i, j: (i, j)),
        out_shape=jax.ShapeDtypeStruct((m, NCOL), F32),
        scratch_shapes=[pltpu.VMEM((tm, D_MODEL), BF16)],
        compiler_params=_params(("parallel", "arbitrary")),
        name="inproj",
    )(x2d, nw, w)


def _merge_kernel(ohg, orw, omb, g0, g1, g2, h_ref, whg, wrw, wmb, wout, fw_ref, hn_ref, *y_ref):
    u = _sigmoid(g0[...]) * _mm(ohg[...], whg[...])
    u = u + _sigmoid(g1[...]) * _mm(orw[...], wrw[...])
    u = u + _sigmoid(g2[...]) * _mm(omb[...], wmb[...])
    hn = h_ref[...] + _mm(u, wout[...])
    hn_ref[...] = hn
    if y_ref:
        ms = jnp.mean(hn * hn, axis=-1, keepdims=True)
        y_ref[0][...] = hn * lax.rsqrt(ms + RMS_EPS) * fw_ref[...]


def _merge(ohg, orw, omb, proj, h2d, whg, wrw, wmb, wout, fw, final, l):
    m = h2d.shape[0]
    tm = min(256, m)
    row = lambda i: (i, 0)
    const = lambda i: (0, 0)
    wspec = pl.BlockSpec((None, D_MODEL, D_MODEL), lambda i: (l, 0, 0))
    out_shape = [jax.ShapeDtypeStruct((m, D_MODEL), F32)]
    out_specs = [pl.BlockSpec((tm, D_MODEL), row)]
    if final:
        out_shape.append(jax.ShapeDtypeStruct((m, D_MODEL), F32))
        out_specs.append(pl.BlockSpec((tm, D_MODEL), row))
    return pl.pallas_call(
        _merge_kernel,
        grid=(m // tm,),
        in_specs=[
            pl.BlockSpec((tm, D_MODEL), row), pl.BlockSpec((tm, D_MODEL), row), pl.BlockSpec((tm, D_MODEL), row),
            pl.BlockSpec((tm, D_MODEL), lambda i: (i, C_G0)),
            pl.BlockSpec((tm, D_MODEL), lambda i: (i, C_G1)),
            pl.BlockSpec((tm, D_MODEL), lambda i: (i, C_G2)),
            pl.BlockSpec((tm, D_MODEL), row),
            wspec, wspec, wspec, wspec,
            pl.BlockSpec((1, D_MODEL), const),
        ],
        out_specs=out_specs,
        out_shape=out_shape,
        compiler_params=_params(("parallel",)),
        name="merge",
    )(ohg, orw, omb, proj, proj, proj, h2d, whg, wrw, wmb, wout, fw)


def _sample_call(body, name, bsz, in_specs, args, state_spec, state_shape, scratch, new_state):
    tb = SAMPLE_TB
    aliases = {}
    if new_state is not None:
        in_specs = in_specs + [pl.BlockSpec(memory_space=pl.ANY)]
        args = args + (new_state,)
        aliases = {len(args) - 1: 1}
        inner = body
        body = lambda *refs: inner(*refs[:len(args) - 1], *refs[len(args):])
    return pl.pallas_call(
        body,
        grid=(bsz // tb,),
        in_specs=in_specs,
        out_specs=[pl.BlockSpec((tb, D_MODEL), lambda i: (i, 0)), state_spec],
        out_shape=[jax.ShapeDtypeStruct((bsz, D_MODEL), F32), jax.ShapeDtypeStruct(state_shape, F32)],
        scratch_shapes=scratch,
        input_output_aliases=aliases,
        compiler_params=_params(("parallel",)),
        name=name,
    )(*args)


def _hgrn_prep(q, f, lb):
    sig = _sigmoid(f)
    fg = lb + (1.0 - lb) * sig
    logf = jnp.log(jnp.maximum(fg, HG_F_MIN))
    kx = (1.0 - lb) * (1.0 - sig)
    qx = _silu(q) * (HG_DK ** -0.5)
    return qx, kx, logf


def _hgrn_post(o, g, nw):
    ms = jnp.mean(o * o, axis=-1, keepdims=True)
    return o * lax.rsqrt(ms + RMS_EPS) * nw * _silu(g)


def _hgrn_prompt_kernel(q_ref, f_ref, i_ref, g_ref, lb_ref, nw_ref, tril_ref, o_ref, s_ref,
                        st_ref, qs_ref, ks_ref, bs_ref, ob_ref, *, nblk):
    c = pl.program_id(1)
    nb = q_ref.shape[0]

    @pl.when(c == 0)
    def _():
        st_ref[...] = jnp.zeros_like(st_ref)

    tril = tril_ref[...]
    heads = [(i, h, slice(h * 128, (h + 1) * 128)) for i in range(nb) for h in range(HG_HEADS)]
    for i, h, sl in heads:
        qx, kx, logf = _hgrn_prep(q_ref[i, :, sl], f_ref[i, :, sl], lb_ref[:, sl])
        qs_ref[i, :, sl] = qx
        ks_ref[i, :, sl] = kx
        bs_ref[i, :, sl] = _mm_exact(tril, logf) * LOG2E
    half = HG_BLK // 2
    lane8 = lax.broadcasted_iota(jnp.int32, (half, HG_DK), 1)
    top16 = lax.broadcasted_iota(jnp.int32, (HG_BLK, HG_DK), 0) < half
    lower = (lax.broadcasted_iota(jnp.int32, (HG_BLK, HG_BLK), 1)
             <= lax.broadcasted_iota(jnp.int32, (HG_BLK, HG_BLK), 0))

    def intra_weights(j):
        r0 = j * HG_BLK if isinstance(j, int) else pl.multiple_of(j * HG_BLK, HG_BLK)
        top, bot, blk = pl.ds(r0, half), pl.ds(r0 + half, half), pl.ds(r0, HG_BLK)
        amat = []
        for i, h, sl in heads:
            q_t, q_b, b_t, b_b = qs_ref[i, top, sl], qs_ref[i, bot, sl], bs_ref[i, top, sl], bs_ref[i, bot, sl]
            k16, b16 = ks_ref[i, blk, sl], bs_ref[i, blk, sl]
            a_top = jnp.zeros((half, HG_DK), F32)
            a_bot = jnp.zeros((half, HG_DK), F32)
            for jj in range(half):
                s = jnp.sum(q_t * jnp.exp2(b_t - b16[jj:jj + 1]) * k16[jj:jj + 1], axis=-1, keepdims=True)
                a_top = jnp.where(lane8 == jj, s, a_top)
                jb = half + jj
                s = jnp.sum(q_b * jnp.exp2(b_b - b16[jb:jb + 1]) * k16[jb:jb + 1], axis=-1, keepdims=True)
                a_bot = jnp.where(lane8 == jb, s, a_bot)
            b_mid = b16[half - 1:half]
            k_dec = jnp.where(top16, k16 * jnp.exp2(b_mid - b16), 0.0)
            cross = _mm(q_b * jnp.exp2(b_b - b_mid), k_dec, NT)
            full = jnp.concatenate([a_top[:, :HG_BLK], a_bot[:, :HG_BLK] + cross], axis=0)
            amat.append(jnp.where(lower, full, 0.0))
        return tuple(amat)

    def apply_block(j, amat):
        rows = pl.ds(pl.multiple_of(j * HG_BLK, HG_BLK), HG_BLK)
        idx = range(len(heads))
        qb = [qs_ref[i, rows, sl] for i, h, sl in heads]
        kb = [ks_ref[i, rows, sl] for i, h, sl in heads]
        bb = [bs_ref[i, rows, sl] for i, h, sl in heads]
        vb = [i_ref[i, rows, sl] for i, h, sl in heads]
        st = [st_ref[i, h] for i, h, sl in heads]
        intra = [_mm(amat[n], vb[n]) for n in idx]
        inter = [_mm(qb[n] * jnp.exp2(bb[n]), st[n], NT) for n in idx]
        for n, (i, h, sl) in enumerate(heads):
            b_end = bb[n][HG_BLK - 1:HG_BLK, :]
            st_ref[i, h] = st[n] * jnp.exp2(b_end) + _mm(vb[n], kb[n] * jnp.exp2(b_end - bb[n]), TN)
        for n, (i, h, sl) in enumerate(heads):
            ob_ref[i, rows, sl] = intra[n] + inter[n]

    def step(j, amats):
        apply_block(j - 1, amats)
        return intra_weights(j)

    apply_block(nblk - 1, lax.fori_loop(1, nblk, step, intra_weights(0)))
    nw = nw_ref[...]
    for i, h, sl in heads:
        o_ref[i, :, sl] = _hgrn_post(ob_ref[i, :, sl], g_ref[i, :, sl], nw).astype(o_ref.dtype)

    @pl.when(c == pl.num_programs(1) - 1)
    def _():
        for i, h, sl in heads:
            s_ref[i, h] = st_ref[i, h].T


def _hgrn_prompt(proj, lb, nw, bsz, t, l):
    tt = min(256, t)
    n_t = t // tt
    nb = HG_NB if bsz % HG_NB == 0 else 1
    tril = jnp.asarray(np.kron(np.eye(tt // HG_BLK), np.tril(np.ones((HG_BLK, HG_BLK)))), BF16)
    proj3 = proj.reshape(bsz, t, NCOL)
    col = lambda blk: (lambda b, c: (b, c, blk))
    const = lambda shape: pl.BlockSpec(shape, lambda b, c: (0, 0))
    layer = lambda shape: pl.BlockSpec((None,) + shape, lambda b, c: (l, 0, 0))
    tile = pltpu.VMEM((nb, tt, D_MODEL), F32)
    o, s = pl.pallas_call(
        functools.partial(_hgrn_prompt_kernel, nblk=tt // HG_BLK),
        grid=(bsz // nb, n_t),
        in_specs=[
            pl.BlockSpec((nb, tt, D_MODEL), col(C_Q)), pl.BlockSpec((nb, tt, D_MODEL), col(C_F)),
            pl.BlockSpec((nb, tt, D_MODEL), col(C_I)), pl.BlockSpec((nb, tt, D_MODEL), col(C_GH)),
            layer((1, D_MODEL)), layer((1, 128)), const((tt, tt)),
        ],
        out_specs=[
            pl.BlockSpec((nb, tt, D_MODEL), lambda b, c: (b, c, 0)),
            pl.BlockSpec((nb, HG_HEADS, HG_DK, HG_DV), lambda b, c: (b, 0, 0, 0)),
        ],
        out_shape=[
            jax.ShapeDtypeStruct((bsz, t, D_MODEL), BF16),
            jax.ShapeDtypeStruct((bsz, HG_HEADS, HG_DK, HG_DV), F32),
        ],
        scratch_shapes=[pltpu.VMEM((nb, HG_HEADS, HG_DV, HG_DK), F32), tile, tile, tile, tile],
        compiler_params=_params(("parallel", "arbitrary")),
        name="hgrn_prompt",
    )(proj3, proj3, proj3, proj3, lb, nw, tril)
    return o.reshape(bsz * t, D_MODEL), s


def _hgrn_sample_kernel(q_ref, f_ref, i_ref, g_ref, lb_ref, nw_ref, s_ref, o_ref, so_ref,
                        qs_ref, ks_ref, fs_ref, ob_ref):
    qx, kx, logf = _hgrn_prep(q_ref[...], f_ref[...], lb_ref[...])
    qs_ref[...] = qx
    ks_ref[...] = kx
    fs_ref[...] = jnp.exp(logf)
    ob_ref[...] = jnp.zeros_like(ob_ref)
    rows = lax.broadcasted_iota(jnp.int32, (SAMPLE_TB, 1), 0)

    heads = [slice(h * 128, (h + 1) * 128) for h in range(HG_HEADS)]

    unroll = 2

    def per_seqs(i, carry):
        work = [(i * unroll + j, rows == i * unroll + j, h, sl) for j in range(unroll)
                for h, sl in enumerate(heads)]
        fcol = [_col_bcast(jnp.where(sel, fs_ref[:, sl], 0.0), HG_DV) for b, sel, h, sl in work]
        kv = [_mm(jnp.where(sel, ks_ref[:, sl], 0.0), i_ref[:, sl], TN) for b, sel, h, sl in work]
        s_new = [fcol[n] * s_ref[b, h] + kv[n] for n, (b, sel, h, sl) in enumerate(work)]
        for s, (b, sel, h, sl) in zip(s_new, work):
            so_ref[b, h] = s
        outs = [_mm(jnp.where(sel, qs_ref[:, sl], 0.0), s) for s, (b, sel, h, sl) in zip(s_new, work)]
        for h, sl in enumerate(heads):
            ob_ref[:, sl] += sum(outs[j * HG_HEADS + h] for j in range(unroll))
        return carry

    lax.fori_loop(0, SAMPLE_TB // unroll, per_seqs, 0)
    nw = nw_ref[...]
    for h in range(HG_HEADS):
        sl = slice(h * 128, (h + 1) * 128)
        o_ref[:, sl] = _hgrn_post(ob_ref[:, sl], g_ref[:, sl], nw)


def _hgrn_sample(proj, lb, nw, state, l, new_state):
    bsz = state.shape[1]
    tb = SAMPLE_TB
    col = lambda blk: (lambda i: (i, blk))
    tile = pltpu.VMEM((tb, D_MODEL), F32)
    sspec = pl.BlockSpec((None, tb, HG_HEADS, HG_DK, HG_DV), lambda i: (l, i, 0, 0, 0))
    return _sample_call(
        _hgrn_sample_kernel, "hgrn_sample", bsz,
        [
            pl.BlockSpec((tb, D_MODEL), col(C_Q)), pl.BlockSpec((tb, D_MODEL), col(C_F)),
            pl.BlockSpec((tb, D_MODEL), col(C_I)), pl.BlockSpec((tb, D_MODEL), col(C_GH)),
            pl.BlockSpec((None, 1, D_MODEL), lambda i: (l, 0, 0)),
            pl.BlockSpec((None, 1, 128), lambda i: (l, 0, 0)),
            sspec,
        ],
        (proj, proj, proj, proj, lb, nw, state), sspec, state.shape, [tile, tile, tile, tile], new_state)


def _rw_lora(wdad, w2a2):
    lo = lax.broadcasted_iota(jnp.int32, wdad.shape, 1) < RW_LORA
    lora_w = _mm3(jnp.where(lo, jnp.tanh(wdad), 0.0), w2a2)
    lora_a = _mm3(jnp.where(lo, 0.0, wdad), w2a2)
    return lora_w, lora_a


def _rw_prep_all(items, ones_bd):
    rs, ks, vs, lws, a_sigs, kks = [], [], [], [], [], []
    for r, k0, v, lora_w, lora_a, w0, a0, kk_w, ka_w in items:
        lws.append(-RW_DECAY_SCALE * _sigmoid(w0 + lora_w))
        a_sig = _sigmoid(a0 + lora_a)
        a_sigs.append(a_sig)
        kks.append(k0 * kk_w)
        rs.append(r)
        vs.append(v)
        ks.append(k0 * (1.0 + (a_sig - 1.0) * ka_w))
    sq = _seg_many(ones_bd, [kk * kk for kk in kks])
    kks = [kk / jnp.maximum(jnp.sqrt(s), 1e-12) for kk, s in zip(kks, sq)]
    return rs, ks, vs, lws, [-kk for kk in kks], [kk * a for kk, a in zip(kks, a_sigs)]


def _rw_posts(os_, rs, ks, vs, gs, rk_ws, lnws, lnbs, ones_bd):
    idx = range(len(os_))
    sums = _seg_many(ones_bd, list(os_) + [rs[i] * ks[i] * rk_ws[i] for i in idx])
    ds = [os_[i] - sums[i] * (1.0 / RW_N) for i in idx]
    bonus = sums[len(os_):]
    var = [s * (1.0 / RW_N) for s in _seg_many(ones_bd, [d * d for d in ds])]
    return [(ds[i] * lax.rsqrt(var[i] + RW_LN_EPS) * lnws[i] + lnbs[i] + bonus[i] * vs[i]) * _silu(gs[i])
            for i in idx]


def _rw_chunks(rs, ks, vs, lws, avs, bvs, sbds, tril):
    n = RW_CHUNK
    pairs = range(len(rs))
    lane = lax.broadcasted_iota(jnp.int32, (n, 128), 1)
    row = lax.broadcasted_iota(jnp.int32, (n, 128), 0)
    m0 = lane < RW_N
    col = jnp.where(m0, lane, lane - RW_N)
    strict = col < row
    incl = col <= row
    eye = (col == row).astype(F32)

    def bd(x):
        xb = x.astype(BF16)
        zero = jnp.zeros_like(xb)
        return jnp.concatenate([jnp.where(m0, xb, zero), jnp.where(m0, zero, xb)], axis=0)

    cls = [_mm_exact(tril, lws[q]) for q in pairs]
    a_h = [avs[q] * jnp.exp(cls[q] - lws[q]) for q in pairs]
    r_h = [rs[q] * jnp.exp(cls[q]) for q in pairs]
    b_c = [bvs[q] * jnp.exp(-cls[q]) for q in pairs]
    k_c = [ks[q] * jnp.exp(-cls[q]) for q in pairs]
    lhs = [jnp.concatenate([a_h[q], r_h[q]], axis=0) for q in pairs]
    from_state = [_mm(lhs[q], sbds[q], NT) for q in pairs]
    m_bk = [_mm(lhs[q], jnp.concatenate([bd(b_c[q]), bd(k_c[q])], axis=0), NT) for q in pairs]
    m_ab = [jnp.where(strict, m[:n, :128], 0.0) for m in m_bk]
    m_ak = [jnp.where(strict, m[:n, 128:], 0.0) for m in m_bk]
    m_rb = [jnp.where(incl, m[n:, :128], 0.0) for m in m_bk]
    m_rk = [jnp.where(incl, m[n:, 128:], 0.0) for m in m_bk]
    ps = [_mm(m, bd(m)) for m in m_ab]
    xs = [eye + m for m in m_ab]
    for level in range(1, 5):
        both = [_mm(jnp.concatenate([ps[q], xs[q]], axis=0), bd(ps[q])) for q in pairs]
        ps = [m[:n] for m in both]
        xs = [xs[q] + both[q][n:] for q in pairs]
    xs = [xs[q] + _mm(xs[q], bd(ps[q])) for q in pairs]
    from_v = [_mm(jnp.concatenate([m_ak[q], m_rk[q]], axis=0), bd(vs[q])) for q in pairs]
    w_all = [from_state[q][:n] + from_v[q][:n] for q in pairs]
    u_all = [_mm(xs[q], bd(w_all[q])) for q in pairs]
    o_all = [from_state[q][n:] + from_v[q][n:] + _mm(m_rb[q], bd(u_all[q])) for q in pairs]

    vi = lax.broadcasted_iota(jnp.int32, (128, 128), 0)
    ki = lax.broadcasted_iota(jnp.int32, (128, 128), 1)
    same_head = (vi < RW_N) == (ki < RW_N)
    new = []
    for q in pairs:
        cl_end = cls[q][n - 1:n, :]
        e_end = jnp.exp(cl_end - cls[q])
        upd = _mm(jnp.concatenate([u_all[q], vs[q]], axis=0),
                  jnp.concatenate([bvs[q] * e_end, ks[q] * e_end], axis=0), TN)
        new.append(sbds[q] * jnp.exp(cl_end) + jnp.where(same_head, upd, 0.0))
    return o_all, new


def _rwkv_prompt_kernel(r_ref, k_ref, v_ref, g_ref, wdad_ref, mu_ref, w0_ref, a0_ref, kkw_ref, kaw_ref, rkw_ref,
                        lnw_ref, lnb_ref, w2a2_ref, tril_ref, ones_ref, o_ref, s_ref,
                        sbd_ref, pr_ref, pk_ref, pv_ref, pwd_ref, lw_ref, la_ref):
    c = pl.program_id(1)
    n = RW_CHUNK
    nb = r_ref.shape[0]

    @pl.when(c == 0)
    def _():
        sbd_ref[...] = jnp.zeros_like(sbd_ref)
        pr_ref[...] = jnp.zeros_like(pr_ref)
        pk_ref[...] = jnp.zeros_like(pk_ref)
        pv_ref[...] = jnp.zeros_like(pv_ref)
        pwd_ref[...] = jnp.zeros_like(pwd_ref)

    row = lax.broadcasted_iota(jnp.int32, (n, 128), 0)

    def shift(z_ref, prev_ref, i, sl, mu_lo):
        z = z_ref[i, :, sl]
        zp = jnp.where(row == 0, prev_ref[i, :, sl], pltpu.roll(z, 1, axis=0))
        prev_ref[i, :, sl] = z[n - 1:n, :]
        return z + (zp - z) * mu_ref[:, mu_lo:mu_lo + 128]

    for i in range(nb):
        lora_w, lora_a = _rw_lora(shift(wdad_ref, pwd_ref, i, slice(0, 128), 3 * D_MODEL), w2a2_ref[...])
        lw_ref[i] = lora_w
        la_ref[i] = lora_a
    ones_bd = ones_ref[...]
    lanes = [slice(q * 128, (q + 1) * 128) for q in range(RW_HEADS // 2)]
    pairs = [(i, q, sl) for i in range(nb) for q, sl in enumerate(lanes)]
    rs, ks, vs, lws, avs, bvs = _rw_prep_all(
        [(shift(r_ref, pr_ref, i, sl, q * 128), shift(k_ref, pk_ref, i, sl, D_MODEL + q * 128),
          shift(v_ref, pv_ref, i, sl, 2 * D_MODEL + q * 128), lw_ref[i, :, sl], la_ref[i, :, sl],
          w0_ref[:, sl], a0_ref[:, sl], kkw_ref[:, sl], kaw_ref[:, sl]) for i, q, sl in pairs], ones_bd)
    o_all, sbd_new = _rw_chunks(rs, ks, vs, lws, avs, bvs, [sbd_ref[i, q] for i, q, sl in pairs], tril_ref[...])
    for (i, q, sl), new in zip(pairs, sbd_new):
        sbd_ref[i, q] = new
    outs = _rw_posts(o_all, rs, ks, vs, [g_ref[i, :, sl] for i, q, sl in pairs],
                     [rkw_ref[:, sl] for i, q, sl in pairs], [lnw_ref[:, sl] for i, q, sl in pairs],
                     [lnb_ref[:, sl] for i, q, sl in pairs], ones_bd)
    for (i, q, sl), out in zip(pairs, outs):
        o_ref[i, :, sl] = out.astype(o_ref.dtype)

    @pl.when(c == pl.num_programs(1) - 1)
    def _():
        s_ref[...] = sbd_ref[...]


def _rw_consts():
    tril = jnp.asarray(np.tril(np.ones((RW_CHUNK, RW_CHUNK))), BF16)
    ones_bd = jnp.asarray(np.kron(np.eye(2), np.ones((RW_N, RW_N))), BF16)
    return tril, ones_bd


def _rwkv_prompt(proj, p, bsz, t, l):
    n = RW_CHUNK
    n_t = t // n
    nb = RW_NB if bsz % RW_NB == 0 else 1
    tril, ones_bd = _rw_consts()
    proj3 = proj.reshape(bsz, t, NCOL)
    col = lambda blk: (lambda b, c: (b, c, blk))
    const = lambda shape: pl.BlockSpec(shape, lambda b, c: (0, 0))
    layer = lambda shape: pl.BlockSpec((None,) + shape, lambda b, c: (l, 0, 0))
    vec = layer((1, D_MODEL))
    prev = pltpu.VMEM((nb, 1, D_MODEL), F32)
    tile = pltpu.VMEM((nb, n, D_MODEL), F32)
    o, sbd = pl.pallas_call(
        _rwkv_prompt_kernel,
        grid=(bsz // nb, n_t),
        in_specs=[
            pl.BlockSpec((nb, n, D_MODEL), col(C_R)), pl.BlockSpec((nb, n, D_MODEL), col(C_K)),
            pl.BlockSpec((nb, n, D_MODEL), col(C_V)), pl.BlockSpec((nb, n, D_MODEL), col(C_GR)),
            pl.BlockSpec((nb, n, 128), lambda b, c: (b, c, C_WDAD)),
            layer((1, RW_SHIFT)),
            vec, vec, vec, vec, vec, vec, vec,
            layer((128, D_MODEL)), const((n, n)), const((128, 128)),
        ],
        out_specs=[
            pl.BlockSpec((nb, n, D_MODEL), lambda b, c: (b, c, 0)),
            pl.BlockSpec((nb, RW_HEADS // 2, 128, 128), lambda b, c: (b, 0, 0, 0)),
        ],
        out_shape=[
            jax.ShapeDtypeStruct((bsz, t, D_MODEL), BF16),
            jax.ShapeDtypeStruct((bsz, RW_HEADS // 2, 128, 128), F32),
        ],
        scratch_shapes=[pltpu.VMEM((nb, RW_HEADS // 2, 128, 128), F32), prev, prev, prev,
                        pltpu.VMEM((nb, 1, 128), F32), tile, tile],
        compiler_params=_params(("parallel", "arbitrary")),
        name="rwkv_prompt",
    )(proj3, proj3, proj3, proj3, proj3, p["mu"], p["w0"], p["a0"], p["kk"], p["ka"], p["rk"], p["lnw"], p["lnb"],
      p["w2a2"], tril, ones_bd)
    s6 = sbd.reshape(bsz, RW_HEADS // 2, 2, RW_N, 2, RW_N)
    state = jnp.stack([s6[:, :, 0, :, 0, :], s6[:, :, 1, :, 1, :]], axis=2)
    return o.reshape(bsz * t, D_MODEL), state.reshape(bsz, RW_HEADS, RW_N, RW_N)


def _rwkv_sample_kernel(r_ref, k_ref, v_ref, g_ref, wdad_ref, sh_ref, mu_ref, w0_ref, a0_ref, kkw_ref, kaw_ref,
                        rkw_ref, lnw_ref, lnb_ref, w2a2_ref, ones_ref, s_ref, o_ref, so_ref,
                        rs_ref, ks_ref, vs_ref, ob_ref, kr_ref, wr_ref, ar_ref, br_ref):
    ones_bd = ones_ref[...]
    tb = SAMPLE_TB

    def store_rows(ref, sl, x):
        for i in range(tb):
            ref[i, :, sl] = x[i:i + 1, :]

    def shift(z, lo, width):
        return z + (sh_ref[:, lo:lo + width] - z) * mu_ref[:, lo:lo + width]

    wdad = shift(wdad_ref[...], 3 * D_MODEL, 128)
    lora_w, lora_a = _rw_lora(wdad, w2a2_ref[...])
    pairs = [slice(q * 128, (q + 1) * 128) for q in range(RW_HEADS // 2)]
    rs, ks, vs, lws, avs, bvs = _rw_prep_all(
        [(shift(r_ref[:, sl], q * 128, 128), shift(k_ref[:, sl], D_MODEL + q * 128, 128),
          shift(v_ref[:, sl], 2 * D_MODEL + q * 128, 128), lora_w[:, sl], lora_a[:, sl], w0_ref[:, sl],
          a0_ref[:, sl], kkw_ref[:, sl], kaw_ref[:, sl]) for q, sl in enumerate(pairs)], ones_bd)
    for q, sl in enumerate(pairs):
        rs_ref[:, sl] = rs[q]
        ks_ref[:, sl] = ks[q]
        vs_ref[:, sl] = vs[q]
        store_rows(kr_ref, sl, ks[q])
        store_rows(wr_ref, sl, jnp.exp(lws[q]))
        store_rows(ar_ref, sl, avs[q])
        store_rows(br_ref, sl, bvs[q])
    ob_ref[...] = jnp.zeros_like(ob_ref)
    rows = lax.broadcasted_iota(jnp.int32, (tb, 1), 0)

    hs = range(RW_HEADS)
    lanes = [slice(h * RW_N, (h + 1) * RW_N) for h in hs]
    unroll = 2

    def per_seqs(i, carry):
        bs = [i * unroll + j for j in range(unroll)]
        work = [(b, rows == b, h) for b in bs for h in hs]

        def row_of(ref, b, h):
            pair = slice((h // 2) * 128, (h // 2 + 1) * 128)
            return ref[b, :, pair][:, (h % 2) * RW_N:(h % 2 + 1) * RW_N]

        vcol = [_col_bcast(jnp.where(sel, vs_ref[:, lanes[h]], 0.0), RW_N) for b, sel, h in work]
        s_old = [s_ref[b, h] for b, sel, h in work]
        sa = [jnp.sum(s * row_of(ar_ref, b, h), axis=-1, keepdims=True) for s, (b, sel, h) in zip(s_old, work)]
        s_new = [s_old[n] * row_of(wr_ref, b, h) + sa[n] * row_of(br_ref, b, h) + vcol[n] * row_of(kr_ref, b, h)
                 for n, (b, sel, h) in enumerate(work)]
        for s, (b, sel, h) in zip(s_new, work):
            so_ref[b, h] = s
        outs = [_mm(jnp.where(sel, rs_ref[:, lanes[h]], 0.0), s, NT) for s, (b, sel, h) in zip(s_new, work)]
        for h in hs:
            ob_ref[:, lanes[h]] += sum(outs[j * RW_HEADS + h] for j in range(unroll))
        return carry

    lax.fori_loop(0, tb // unroll, per_seqs, 0)
    outs = _rw_posts([ob_ref[:, sl] for sl in pairs], rs, ks, vs, [g_ref[:, sl] for sl in pairs],
                     [rkw_ref[:, sl] for sl in pairs], [lnw_ref[:, sl] for sl in pairs],
                     [lnb_ref[:, sl] for sl in pairs], ones_bd)
    for sl, out in zip(pairs, outs):
        o_ref[:, sl] = out


def _rwkv_sample(proj, p, shift_state, state, l, new_state):
    bsz = state.shape[1]
    tb = SAMPLE_TB
    _, ones_bd = _rw_consts()
    col = lambda blk: (lambda i: (i, blk))
    layer = lambda shape: pl.BlockSpec((None,) + shape, lambda i: (l, 0, 0))
    vec = layer((1, D_MODEL))
    tile = pltpu.VMEM((tb, D_MODEL), F32)
    sspec = pl.BlockSpec((None, tb, RW_HEADS, RW_N, RW_N), lambda i: (l, i, 0, 0, 0))
    return _sample_call(
        _rwkv_sample_kernel, "rwkv_sample", bsz,
        [
            pl.BlockSpec((tb, D_MODEL), col(C_R)), pl.BlockSpec((tb, D_MODEL), col(C_K)),
            pl.BlockSpec((tb, D_MODEL), col(C_V)), pl.BlockSpec((tb, D_MODEL), col(C_GR)),
            pl.BlockSpec((tb, 128), lambda i: (i, C_WDAD)),
            pl.BlockSpec((None, tb, RW_SHIFT), lambda i: (l, i, 0)),
            layer((1, RW_SHIFT)),
            vec, vec, vec, vec, vec, vec, vec,
            layer((128, D_MODEL)),
            pl.BlockSpec((128, 128), lambda i: (0, 0)),
            sspec,
        ],
        (proj, proj, proj, proj, proj, shift_state, p["mu"], p["w0"], p["a0"], p["kk"], p["ka"], p["rk"],
         p["lnw"], p["lnb"], p["w2a2"], ones_bd, state),
        sspec, state.shape, [tile] * 4 + [pltpu.VMEM((tb, 1, D_MODEL), F32)] * 4, new_state)


def _mb_post(y, xs, z, d_rep, nw):
    y = (y + d_rep * xs) * _silu(z)
    ms = jnp.mean(y * y, axis=-1, keepdims=True)
    return y * lax.rsqrt(ms + RMS_EPS) * nw


def _ssd_prompt_kernel(x_ref, bc_ref, z_ref, dt_ref, cw_ref, cb_ref, dtb_ref, alog_ref, d_ref, nw_ref, tril_ref,
                       exp_ref, o_ref, s_ref, h_ref, px_ref, pbc_ref):
    c = pl.program_id(1)
    n = MB_CHUNK
    nb = x_ref.shape[0]

    @pl.when(c == 0)
    def _():
        h_ref[...] = jnp.zeros_like(h_ref)
        px_ref[...] = jnp.zeros_like(px_ref)
        pbc_ref[...] = jnp.zeros_like(pbc_ref)

    def conv(u_ref, prev_ref, i, sl, wsl):
        u = u_ref[i, :, sl]
        prev = prev_ref[i, :, sl]
        w = cw_ref[:, wsl]
        row = lax.broadcasted_iota(jnp.int32, (8, u.shape[1]), 0)
        acc = u * w[MB_CONV - 1:MB_CONV, :] + cb_ref[:, wsl]
        for s in range(1, MB_CONV):
            us = pltpu.roll(u, s, axis=0)
            top = jnp.where(row < s, pltpu.roll(prev, s, axis=0), us[0:8])
            us = jnp.concatenate([top, us[8:]], axis=0)
            acc = acc + us * w[MB_CONV - 1 - s:MB_CONV - s, :]
        prev_ref[i, :, sl] = u[n - 8:n, :]
        return _silu(acc)

    tril = tril_ref[...]
    ti = lax.broadcasted_iota(jnp.int32, (n, n), 0)
    si = lax.broadcasted_iota(jnp.int32, (n, n), 1)
    incl = si <= ti
    triu = (ti <= si).astype(BF16)
    lane = lax.broadcasted_iota(jnp.int32, (n, 4 * MB_P), 1)
    nbc = MB_GROUPS * MB_N
    seqs = range(nb)
    groups = [(i, g, slice(g * 256, (g + 1) * 256)) for i in seqs for g in range(MB_GROUPS)]
    idx = range(len(groups))

    spread = exp_ref[...]
    dt_c = [_softplus(dt_ref[i] + dtb_ref[...]) for i in seqs]
    a_c = [dt_c[i] * (-jnp.exp(alog_ref[...])) for i in seqs]
    cs_c = [_mm_exact(tril, a_c[i]) for i in seqs]
    cs_rows = [_mm_exact(triu, a_c[i], TN, x_is_lhs=True) for i in seqs]
    dt_all = [_mm_exact(spread, dt_c[i], x_is_lhs=True) for i in seqs]
    cs_all = [_mm_exact(spread, cs_c[i], x_is_lhs=True) for i in seqs]

    xs = [conv(x_ref, px_ref, i, gx, gx) for i, g, gx in groups]
    bm = [conv(bc_ref, pbc_ref, i, slice(g * MB_N, (g + 1) * MB_N),
               slice(D_MODEL + g * MB_N, D_MODEL + (g + 1) * MB_N)) for i, g, gx in groups]
    cm = [conv(bc_ref, pbc_ref, i, slice(nbc + g * MB_N, nbc + (g + 1) * MB_N),
               slice(D_MODEL + nbc + g * MB_N, D_MODEL + nbc + (g + 1) * MB_N)) for i, g, gx in groups]
    xdt = [xs[k] * dt_all[i][:, gx] for k, (i, g, gx) in enumerate(groups)]
    cs = [cs_all[i][:, gx] for i, g, gx in groups]
    gmat = [_mm(cm[k], bm[k], NT) for k in idx]
    h_all = [h_ref[i, g] for i, g, gx in groups]
    y = [jnp.exp(cs[k]) * _mm(cm[k], h_all[k], NT) for k in idx]
    for k, (i, g, gx) in enumerate(groups):
        for e in range(4):
            hd = slice(e * MB_P, (e + 1) * MB_P)
            hrow = slice(4 * g + e, 4 * g + e + 1)
            lmat = jnp.where(incl, jnp.exp(cs[k][:, hd] - cs_rows[i][hrow, :]), 0.0)
            me = (lane >= e * MB_P) & (lane < (e + 1) * MB_P)
            y[k] = y[k] + _mm(gmat[k] * lmat, jnp.where(me, xdt[k], 0.0))
    for k, (i, g, gx) in enumerate(groups):
        cs_end = cs[k][n - 1:n, :]
        upd = _mm(xdt[k] * jnp.exp(cs_end - cs[k]), bm[k], TN)
        decay = [jnp.exp(cs_rows[i][4 * g + e:4 * g + e + 1, n - 1:n]) for e in range(4)]
        h_ref[i, g] = jnp.concatenate([h_all[k][e * MB_P:(e + 1) * MB_P] * decay[e] for e in range(4)],
                                      axis=0) + upd
    for k, (i, g, gx) in enumerate(groups):
        o_ref[i, :, gx] = _mb_post(y[k], xs[k], z_ref[i, :, gx], d_ref[:, gx], nw_ref[:, gx]).astype(o_ref.dtype)

    @pl.when(c == pl.num_programs(1) - 1)
    def _():
        for i, g, gx in groups:
            for e in range(4):
                s_ref[i, 4 * g + e] = h_ref[i, g, e * MB_P:(e + 1) * MB_P, :]


def _head_spread():
    m = np.zeros((128, D_MODEL), np.float32)
    for h in range(MB_HEADS):
        m[h, h * MB_P:(h + 1) * MB_P] = 1.0
    return jnp.asarray(m, BF16)


def _ssd_prompt(proj, p, bsz, t, l):
    n = MB_CHUNK
    n_t = t // n
    nb = MB_NB if bsz % MB_NB == 0 else 1
    tril = jnp.asarray(np.tril(np.ones((n, n))), BF16)
    proj3 = proj.reshape(bsz, t, NCOL)
    col = lambda blk: (lambda b, c: (b, c, blk))
    const = lambda shape: pl.BlockSpec(shape, lambda b, c: (0, 0))
    layer = lambda shape: pl.BlockSpec((None,) + shape, lambda b, c: (l, 0, 0))
    vec = layer((1, D_MODEL))
    o, s = pl.pallas_call(
        _ssd_prompt_kernel,
        grid=(bsz // nb, n_t),
        in_specs=[
            pl.BlockSpec((nb, n, D_MODEL), col(C_X)), pl.BlockSpec((nb, n, D_MODEL), col(C_BC)),
            pl.BlockSpec((nb, n, D_MODEL), col(C_Z)), pl.BlockSpec((nb, n, 128), col(C_DT)),
            layer((MB_CONV, MB_XBC)), layer((1, MB_XBC)),
            layer((1, 128)), layer((1, 128)), vec, vec,
            const((n, n)), const((128, D_MODEL)),
        ],
        out_specs=[
            pl.BlockSpec((nb, n, D_MODEL), lambda b, c: (b, c, 0)),
            pl.BlockSpec((nb, MB_HEADS, MB_P, MB_N), lambda b, c: (b, 0, 0, 0)),
        ],
        out_shape=[
            jax.ShapeDtypeStruct((bsz, t, D_MODEL), BF16),
            jax.ShapeDtypeStruct((bsz, MB_HEADS, MB_P, MB_N), F32),
        ],
        scratch_shapes=[pltpu.VMEM((nb, MB_GROUPS, 4 * MB_P, MB_N), F32), pltpu.VMEM((nb, 8, D_MODEL), F32),
                        pltpu.VMEM((nb, 8, D_MODEL), F32)],
        compiler_params=_params(("parallel", "arbitrary")),
        name="ssd_prompt",
    )(proj3, proj3, proj3, proj3, p["cw"], p["cb"], p["dtb"], p["alog"], p["d"], p["nw"], tril, _head_spread())
    return o.reshape(bsz * t, D_MODEL), s


def _ssd_sample_kernel(xbc_ref, z_ref, dt_ref, cs_ref, cw_ref, cb_ref, dtb_ref, alog_ref, d_ref, nw_ref, exp_ref,
                       s_ref, o_ref, so_ref, xs_ref, xd_ref, bs_ref, cms_ref, dec_ref, ob_ref):
    tb = SAMPLE_TB
    w = cw_ref[...]
    acc = xbc_ref[...] * w[MB_CONV - 1:MB_CONV, :] + cb_ref[...]
    for i in range(MB_CONV - 1):
        acc = acc + cs_ref[i] * w[i:i + 1, :]
    xbc = _silu(acc)
    xs = xbc[:, :D_MODEL]
    dt = _softplus(dt_ref[...] + dtb_ref[...])
    xs_ref[...] = xs
    xd_ref[...] = xs * _mm_exact(exp_ref[...], dt, x_is_lhs=True)
    bs_ref[...] = xbc[:, D_MODEL:D_MODEL + MB_GROUPS * MB_N]
    cms_ref[...] = xbc[:, D_MODEL + MB_GROUPS * MB_N:]
    dec = jnp.exp(dt * (-jnp.exp(alog_ref[...])))
    for i in range(tb):
        dec_ref[i] = dec[i:i + 1, :]
    ob_ref[...] = jnp.zeros_like(ob_ref)
    rows = lax.broadcasted_iota(jnp.int32, (tb, 1), 0)

    hs = range(MB_HEADS)
    lanes = [slice(h * MB_P, (h + 1) * MB_P) for h in hs]
    grp = [slice((h // 4) * MB_N, (h // 4 + 1) * MB_N) for h in hs]
    unroll = 2

    def per_seqs(i, carry):
        work = [(i * unroll + j, rows == i * unroll + j, h) for j in range(unroll) for h in hs]

        def dec_of(b, h):
            return dec_ref[b][:, h:h + 1]

        upd = [_mm(jnp.where(sel, xd_ref[:, lanes[h]], 0.0), bs_ref[:, grp[h]], TN) for b, sel, h in work]
        h_new = [dec_of(b, h) * s_ref[b, h] + upd[n] for n, (b, sel, h) in enumerate(work)]
        for s, (b, sel, h) in zip(h_new, work):
            so_ref[b, h] = s
        outs = [_mm(jnp.where(sel, cms_ref[:, grp[h]], 0.0), s, NT) for s, (b, sel, h) in zip(h_new, work)]
        for h in hs:
            ob_ref[:, lanes[h]] += sum(outs[j * MB_HEADS + h] for j in range(unroll))
        return carry

    lax.fori_loop(0, tb // unroll, per_seqs, 0)
    for g in range(MB_GROUPS):
        sl = slice(g * 256, (g + 1) * 256)
        o_ref[:, sl] = _mb_post(ob_ref[:, sl], xs_ref[:, sl], z_ref[:, sl], d_ref[:, sl], nw_ref[:, sl])


def _ssd_sample(proj, p, conv_state, state, l, new_state):
    bsz = state.shape[1]
    tb = SAMPLE_TB
    layer = lambda shape: pl.BlockSpec((None,) + shape, lambda i: (l, 0, 0))
    vec = layer((1, D_MODEL))
    tile = pltpu.VMEM((tb, D_MODEL), F32)
    half = pltpu.VMEM((tb, MB_GROUPS * MB_N), F32)
    sspec = pl.BlockSpec((None, tb, MB_HEADS, MB_P, MB_N), lambda i: (l, i, 0, 0, 0))
    return _sample_call(
        _ssd_sample_kernel, "ssd_sample", bsz,
        [
            pl.BlockSpec((tb, MB_XBC), lambda i: (i, C_X // 2)),
            pl.BlockSpec((tb, D_MODEL), lambda i: (i, C_Z)),
            pl.BlockSpec((tb, 128), lambda i: (i, C_DT)),
            pl.BlockSpec((None, MB_CONV - 1, tb, MB_XBC), lambda i: (l, 0, i, 0)),
            layer((MB_CONV, MB_XBC)), layer((1, MB_XBC)),
            layer((1, 128)), layer((1, 128)), vec, vec,
            pl.BlockSpec((128, D_MODEL), lambda i: (0, 0)),
            sspec,
        ],
        (proj, proj, proj, conv_state, p["cw"], p["cb"], p["dtb"], p["alog"], p["d"], p["nw"], _head_spread(),
         state),
        sspec, state.shape, [tile, tile, half, half, pltpu.VMEM((tb, 1, 128), F32), tile], new_state)


def _prep_w_in(w_in):
    o = np.cumsum([0, 1024, 1024, 1024, 1024, RW_SHIFT, 1024, 1024, MB_XBC, MB_HEADS, 3 * D_MODEL])
    zrw, grw, zmb, xbc, dtc, gate = o[4], o[5], o[6], o[7], o[8], o[9]
    parts = [(0, zrw), (zrw, zrw + 3 * D_MODEL), (grw, zmb), (xbc, dtc), (zmb, xbc), (gate, o[10]),
             (zrw + 3 * D_MODEL, grw), (dtc, gate)]
    out = jnp.zeros(w_in.shape[:-1] + (NCOL,), BF16)
    at = 0
    for lo, hi in parts:
        out = lax.dynamic_update_slice_in_dim(out, w_in[..., lo:hi].astype(BF16), at, axis=-1)
        at += hi - lo
    return out


def _rep(v):
    return jnp.repeat(v, MB_P, axis=-1)[:, None, :]


def _pad_heads(v):
    return jnp.pad(v, ((0, 0), (0, 128 - MB_HEADS)))[:, None, :]


def kernel(x_prompt, x_sample, state_hgrn, state_rwkv, state_rwkv_shift, state_ssm, state_conv, norm_w, w_in, hg_lb, hg_norm_w, rw_mu, rw_w0, rw_w2, rw_a0, rw_a2, rw_k_k, rw_k_a, rw_r_k, rw_ln_w, rw_ln_b, mb_conv_w, mb_conv_b, mb_dt_bias, mb_A_log, mb_D, mb_norm_w, w_o_hg, w_o_rw, w_o_mb, w_out, final_norm_w):
    bp, t, _ = x_prompt.shape
    bs = x_sample.shape[0]
    sm = jax.nn.softmax(hg_lb.astype(F32), axis=0)
    row = lambda a: a[:, None, :]
    lbs = row(jnp.cumsum(sm, axis=0) - sm[0:1])
    w_in_r = _prep_w_in(w_in)
    rw_p = dict(mu=row(rw_mu), w0=row(rw_w0), a0=row(rw_a0), kk=row(rw_k_k), ka=row(rw_k_a),
                rk=rw_r_k.reshape(DEPTH, 1, D_MODEL), lnw=row(rw_ln_w), lnb=row(rw_ln_b),
                w2a2=jnp.concatenate([rw_w2, rw_a2], axis=1))
    mb_p = dict(cw=mb_conv_w, cb=row(mb_conv_b), dtb=_pad_heads(mb_dt_bias), alog=_pad_heads(mb_A_log), d=_rep(mb_D),
                nw=row(mb_norm_w))
    nw_in, nw_hg = row(norm_w), row(hg_norm_w)
    whg, wrw, wmb, wout = (w.astype(BF16) for w in (w_o_hg, w_o_rw, w_o_mb, w_out))
    fw = final_norm_w[None, :]
    conv_t = jnp.swapaxes(state_conv, 1, 2)
    lora = slice(C_WDAD * 128, (C_WDAD + 1) * 128)

    hp = x_prompt.reshape(bp * t, D_MODEL)
    hs = x_sample.reshape(bs, D_MODEL)
    p_states, s_small = [], []
    s_hg = s_rw = s_ssm = None
    yp = ys = None
    for l in range(DEPTH):
        final = l == DEPTH - 1

        proj = _inproj(hp, nw_in, w_in_r, l)
        o_hg, p_hg = _hgrn_prompt(proj, lbs, nw_hg, bp, t, l)
        o_rw, p_rw = _rwkv_prompt(proj, rw_p, bp, t, l)
        o_mb, p_ssm = _ssd_prompt(proj, mb_p, bp, t, l)
        proj3 = proj.reshape(bp, t, NCOL)
        p_shift = jnp.concatenate([proj3[:, -1, C_R * 1024:C_GR * 1024], proj3[:, -1, lora]], axis=-1)
        p_conv = proj3[:, t - (MB_CONV - 1):, C_X * 1024:C_Z * 1024]
        res = _merge(o_hg, o_rw, o_mb, proj, hp, whg, wrw, wmb, wout, fw, final, l)
        hp = res[0]
        if final:
            yp = res[1]
        p_states.append((p_hg, p_rw, p_shift, p_ssm, p_conv))

        proj = _inproj(hs, nw_in, w_in_r, l)
        o_hg, s_hg = _hgrn_sample(proj, lbs, nw_hg, state_hgrn, l, s_hg)
        o_rw, s_rw = _rwkv_sample(proj, rw_p, state_rwkv_shift, state_rwkv, l, s_rw)
        o_mb, s_ssm = _ssd_sample(proj, mb_p, conv_t, state_ssm, l, s_ssm)
        s_shift = jnp.concatenate([proj[:, C_R * 1024:C_GR * 1024], proj[:, lora]], axis=-1)
        s_conv = jnp.concatenate([state_conv[l][:, 1:], proj[:, None, C_X * 1024:C_Z * 1024]], axis=1)
        res = _merge(o_hg, o_rw, o_mb, proj, hs, whg, wrw, wmb, wout, fw, final, l)
        hs = res[0]
        if final:
            ys = res[1]
        s_small.append((s_shift, s_conv))

    stack = lambda states, i: jnp.stack([s[i] for s in states])
    return (yp.reshape(bp, t, D_MODEL), ys.reshape(bs, 1, D_MODEL),
            *[stack(p_states, i) for i in range(5)],
            s_hg, s_rw, stack(s_small, 0), s_ssm, stack(s_small, 1))
```

```python
import functools

import numpy as np
import jax
import jax.numpy as jnp
from jax import lax
from jax.experimental import pallas as pl
from jax.experimental.pallas import tpu as pltpu

F32 = jnp.float32
BF16 = jnp.bfloat16
HI = lax.Precision.HIGHEST

D_MODEL = 1024
DEPTH = 2
HG_HEADS, HG_DK, HG_DV = 8, 128, 128
HG_F_MIN = 1e-20
LOG2E = 1.4426950408889634
RW_HEADS, RW_N = 16, 64
RW_LORA = 64
RW_SHIFT = 3 * D_MODEL + 2 * RW_LORA
RW_LN_EPS = 64e-5
RW_DECAY_SCALE = float(np.exp(-0.5))
MB_HEADS, MB_P, MB_GROUPS, MB_N, MB_CONV = 16, 64, 4, 128, 4
MB_XBC = D_MODEL + 2 * MB_GROUPS * MB_N
RMS_EPS = 1e-6

C_Q, C_F, C_I, C_GH, C_R, C_K, C_V, C_GR, C_X, C_BC, C_Z, C_G0, C_G1, C_G2 = range(14)
C_WDAD = 14 * 8
C_DT = 14 * 8 + 1
NCOL = 14 * 1024 + 2 * 128

HG_BLK = 16
HG_NB = 2
RW_CHUNK = 64
RW_NB = 4
MB_CHUNK = 64
MB_NB = 4
SAMPLE_TB = 8

VMEM_LIMIT = 48 * 1024 * 1024

NN = (((1,), (0,)), ((), ()))
NT = (((1,), (1,)), ((), ()))
TN = (((0,), (0,)), ((), ()))


def _mm(a, b, dims=NN, hi=False):
    if hi:
        return lax.dot_general(a.astype(F32), b.astype(F32), dims, precision=HI, preferred_element_type=F32)
    return lax.dot_general(a.astype(BF16), b.astype(BF16), dims, preferred_element_type=F32)


def _mm_exact(e, x, dims=NN, x_is_lhs=False, passes=3):
    e = e.astype(BF16)
    acc = None
    for _ in range(passes):
        p = x.astype(BF16)
        x = x - p.astype(F32)
        t = (lax.dot_general(p, e, dims, preferred_element_type=F32) if x_is_lhs
             else lax.dot_general(e, p, dims, preferred_element_type=F32))
        acc = t if acc is None else acc + t
    return acc


def _mm3(a, b, dims=NN):
    a_hi = a.astype(BF16)
    a_lo = (a - a_hi.astype(F32)).astype(BF16)
    b_hi = b.astype(BF16)
    b_lo = (b - b_hi.astype(F32)).astype(BF16)
    dot = lambda x, y: lax.dot_general(x, y, dims, preferred_element_type=F32)
    return dot(a_hi, b_hi) + (dot(a_hi, b_lo) + dot(a_lo, b_hi))


def _seg_many(ones_bd, xs):
    m = xs[0].shape[0]
    out = _mm_exact(ones_bd, jnp.concatenate(xs, axis=0) if len(xs) > 1 else xs[0], x_is_lhs=True, passes=2)
    return [out[i * m:(i + 1) * m] for i in range(len(xs))]


def _sigmoid(x):
    return jax.nn.sigmoid(x)


def _silu(x):
    return x * jax.nn.sigmoid(x)


def _softplus(x):
    return jnp.maximum(x, 0.0) + jnp.log1p(jnp.exp(-jnp.abs(x)))


def _col_bcast(rows, width):
    ones = jnp.ones((rows.shape[0], width), BF16)
    p1 = rows.astype(BF16)
    r1 = rows - p1.astype(F32)
    p2 = r1.astype(BF16)
    p3 = (r1 - p2.astype(F32)).astype(BF16)
    out = lax.dot_general(p1, ones, TN, preferred_element_type=F32)
    out = out + lax.dot_general(p2, ones, TN, preferred_element_type=F32)
    return out + lax.dot_general(p3, ones, TN, preferred_element_type=F32)


def _params(sem):
    return pltpu.CompilerParams(dimension_semantics=sem, vmem_limit_bytes=VMEM_LIMIT)


def _inproj_kernel(x_ref, nw_ref, w_ref, o_ref, xn_ref):
    @pl.when(pl.program_id(1) == 0)
    def _():
        x = x_ref[...]
        ms = jnp.mean(x * x, axis=-1, keepdims=True)
        xn_ref[...] = (x * lax.rsqrt(ms + RMS_EPS) * nw_ref[...]).astype(BF16)

    o_ref[...] = jnp.dot(xn_ref[...], w_ref[...], preferred_element_type=F32)


def _inproj(x2d, nw, w, l):
    m = x2d.shape[0]
    tm = min(2048, m)
    tn = 768
    return pl.pallas_call(
        _inproj_kernel,
        grid=(m // tm, NCOL // tn),
        in_specs=[
            pl.BlockSpec((tm, D_MODEL), lambda i, j: (i, 0)),
            pl.BlockSpec((None, 1, D_MODEL), lambda i, j: (l, 0, 0)),
            pl.BlockSpec((None, D_MODEL, tn), lambda i, j: (l, 0, j)),
        ],
        out_specs=pl.BlockSpec((tm, tn), lambda i, j: (i, j)),
        out_shape=jax.ShapeDtypeStruct((m, NCOL), F32),
        scratch_shapes=[pltpu.VMEM((tm, D_MODEL), BF16)],
        compiler_params=_params(("parallel", "arbitrary")),
        name="inproj",
    )(x2d, nw, w)


def _merge_kernel(ohg, orw, omb, g0, g1, g2, h_ref, whg, wrw, wmb, wout, fw_ref, hn_ref, *y_ref):
    u = _sigmoid(g0[...]) * _mm(ohg[...], whg[...])
    u = u + _sigmoid(g1[...]) * _mm(orw[...], wrw[...])
    u = u + _sigmoid(g2[...]) * _mm(omb[...], wmb[...])
    hn = h_ref[...] + _mm(u, wout[...])
    hn_ref[...] = hn
    if y_ref:
        ms = jnp.mean(hn * hn, axis=-1, keepdims=True)
        y_ref[0][...] = hn * lax.rsqrt(ms + RMS_EPS) * fw_ref[...]


def _merge(ohg, orw, omb, proj, h2d, whg, wrw, wmb, wout, fw, final, l):
    m = h2d.shape[0]
    tm = min(256, m)
    row = lambda i: (i, 0)
    const = lambda i: (0, 0)
    wspec = pl.BlockSpec((None, D_MODEL, D_MODEL), lambda i: (l, 0, 0))
    out_shape = [jax.ShapeDtypeStruct((m, D_MODEL), F32)]
    out_specs = [pl.BlockSpec((tm, D_MODEL), row)]
    if final:
        out_shape.append(jax.ShapeDtypeStruct((m, D_MODEL), F32))
        out_specs.append(pl.BlockSpec((tm, D_MODEL), row))
    return pl.pallas_call(
        _merge_kernel,
        grid=(m // tm,),
        in_specs=[
            pl.BlockSpec((tm, D_MODEL), row), pl.BlockSpec((tm, D_MODEL), row), pl.BlockSpec((tm, D_MODEL), row),
            pl.BlockSpec((tm, D_MODEL), lambda i: (i, C_G0)),
            pl.BlockSpec((tm, D_MODEL), lambda i: (i, C_G1)),
            pl.BlockSpec((tm, D_MODEL), lambda i: (i, C_G2)),
            pl.BlockSpec((tm, D_MODEL), row),
            wspec, wspec, wspec, wspec,
            pl.BlockSpec((1, D_MODEL), const),
        ],
        out_specs=out_specs,
        out_shape=out_shape,
        compiler_params=_params(("parallel",)),
        name="merge",
    )(ohg, orw, omb, proj, proj, proj, h2d, whg, wrw, wmb, wout, fw)


def _sample_call(body, name, bsz, in_specs, args, state_spec, state_shape, scratch, new_state):
    tb = SAMPLE_TB
    aliases = {}
    if new_state is not None:
        in_specs = in_specs + [pl.BlockSpec(memory_space=pl.ANY)]
        args = args + (new_state,)
        aliases = {len(args) - 1: 1}
        inner = body
        body = lambda *refs: inner(*refs[:len(args) - 1], *refs[len(args):])
    return pl.pallas_call(
        body,
        grid=(bsz // tb,),
        in_specs=in_specs,
        out_specs=[pl.BlockSpec((tb, D_MODEL), lambda i: (i, 0)), state_spec],
        out_shape=[jax.ShapeDtypeStruct((bsz, D_MODEL), F32), jax.ShapeDtypeStruct(state_shape, F32)],
        scratch_shapes=scratch,
        input_output_aliases=aliases,
        compiler_params=_params(("parallel",)),
        name=name,
    )(*args)


def _hgrn_prep(q, f, lb):
    sig = _sigmoid(f)
    fg = lb + (1.0 - lb) * sig
    logf = jnp.log(jnp.maximum(fg, HG_F_MIN))
    kx = (1.0 - lb) * (1.0 - sig)
    qx = _silu(q) * (HG_DK ** -0.5)
    return qx, kx, logf


def _hgrn_post(o, g, nw):
    ms = jnp.mean(o * o, axis=-1, keepdims=True)
    return o * lax.rsqrt(ms + RMS_EPS) * nw * _silu(g)


def _hgrn_prompt_kernel(q_ref, f_ref, i_ref, g_ref, lb_ref, nw_ref, tril_ref, o_ref, s_ref,
                        st_ref, qs_ref, ks_ref, bs_ref, ob_ref, *, nblk):
    c = pl.program_id(1)
    nb = q_ref.shape[0]

    @pl.when(c == 0)
    def _():
        st_ref[...] = jnp.zeros_like(st_ref)

    tril = tril_ref[...]
    heads = [(i, h, slice(h * 128, (h + 1) * 128)) for i in range(nb) for h in range(HG_HEADS)]
    for i, h, sl in heads:
        qx, kx, logf = _hgrn_prep(q_ref[i, :, sl], f_ref[i, :, sl], lb_ref[:, sl])
        qs_ref[i, :, sl] = qx
        ks_ref[i, :, sl] = kx
        bs_ref[i, :, sl] = _mm_exact(tril, logf) * LOG2E
    half = HG_BLK // 2
    lane8 = lax.broadcasted_iota(jnp.int32, (half, HG_DK), 1)
    top16 = lax.broadcasted_iota(jnp.int32, (HG_BLK, HG_DK), 0) < half
    lower = (lax.broadcasted_iota(jnp.int32, (HG_BLK, HG_BLK), 1)
             <= lax.broadcasted_iota(jnp.int32, (HG_BLK, HG_BLK), 0))

    def intra_weights(j):
        r0 = j * HG_BLK if isinstance(j, int) else pl.multiple_of(j * HG_BLK, HG_BLK)
        top, bot, blk = pl.ds(r0, half), pl.ds(r0 + half, half), pl.ds(r0, HG_BLK)
        amat = []
        for i, h, sl in heads:
            q_t, q_b, b_t, b_b = qs_ref[i, top, sl], qs_ref[i, bot, sl], bs_ref[i, top, sl], bs_ref[i, bot, sl]
            k16, b16 = ks_ref[i, blk, sl], bs_ref[i, blk, sl]
            a_top = jnp.zeros((half, HG_DK), F32)
            a_bot = jnp.zeros((half, HG_DK), F32)
            for jj in range(half):
                s = jnp.sum(q_t * jnp.exp2(b_t - b16[jj:jj + 1]) * k16[jj:jj + 1], axis=-1, keepdims=True)
                a_top = jnp.where(lane8 == jj, s, a_top)
                jb = half + jj
                s = jnp.sum(q_b * jnp.exp2(b_b - b16[jb:jb + 1]) * k16[jb:jb + 1], axis=-1, keepdims=True)
                a_bot = jnp.where(lane8 == jb, s, a_bot)
            b_mid = b16[half - 1:half]
            k_dec = jnp.where(top16, k16 * jnp.exp2(b_mid - b16), 0.0)
            cross = _mm(q_b * jnp.exp2(b_b - b_mid), k_dec, NT)
            full = jnp.concatenate([a_top[:, :HG_BLK], a_bot[:, :HG_BLK] + cross], axis=0)
            amat.append(jnp.where(lower, full, 0.0))
        return tuple(amat)

    def apply_block(j, amat):
        rows = pl.ds(pl.multiple_of(j * HG_BLK, HG_BLK), HG_BLK)
        idx = range(len(heads))
        qb = [qs_ref[i, rows, sl] for i, h, sl in heads]
        kb = [ks_ref[i, rows, sl] for i, h, sl in heads]
        bb = [bs_ref[i, rows, sl] for i, h, sl in heads]
        vb = [i_ref[i, rows, sl] for i, h, sl in heads]
        st = [st_ref[i, h] for i, h, sl in heads]
        intra = [_mm(amat[n], vb[n]) for n in idx]
        inter = [_mm(qb[n] * jnp.exp2(bb[n]), st[n], NT) for n in idx]
        for n, (i, h, sl) in enumerate(heads):
            b_end = bb[n][HG_BLK - 1:HG_BLK, :]
            st_ref[i, h] = st[n] * jnp.exp2(b_end) + _mm(vb[n], kb[n] * jnp.exp2(b_end - bb[n]), TN)
        for n, (i, h, sl) in enumerate(heads):
            ob_ref[i, rows, sl] = intra[n] + inter[n]

    def step(j, amats):
        apply_block(j - 1, amats)
        return intra_weights(j)

    apply_block(nblk - 1, lax.fori_loop(1, nblk, step, intra_weights(0)))
    nw = nw_ref[...]
    for i, h, sl in heads:
        o_ref[i, :, sl] = _hgrn_post(ob_ref[i, :, sl], g_ref[i, :, sl], nw).astype(o_ref.dtype)

    @pl.when(c == pl.num_programs(1) - 1)
    def _():
        for i, h, sl in heads:
            s_ref[i, h] = st_ref[i, h].T


def _hgrn_prompt(proj, lb, nw, bsz, t, l):
    tt = min(256, t)
    n_t = t // tt
    nb = HG_NB if bsz % HG_NB == 0 else 1
    tril = jnp.asarray(np.kron(np.eye(tt // HG_BLK), np.tril(np.ones((HG_BLK, HG_BLK)))), BF16)
    proj3 = proj.reshape(bsz, t, NCOL)
    col = lambda blk: (lambda b, c: (b, c, blk))
    const = lambda shape: pl.BlockSpec(shape, lambda b, c: (0, 0))
    layer = lambda shape: pl.BlockSpec((None,) + shape, lambda b, c: (l, 0, 0))
    tile = pltpu.VMEM((nb, tt, D_MODEL), F32)
    o, s = pl.pallas_call(
        functools.partial(_hgrn_prompt_kernel, nblk=tt // HG_BLK),
        grid=(bsz // nb, n_t),
        in_specs=[
            pl.BlockSpec((nb, tt, D_MODEL), col(C_Q)), pl.BlockSpec((nb, tt, D_MODEL), col(C_F)),
            pl.BlockSpec((nb, tt, D_MODEL), col(C_I)), pl.BlockSpec((nb, tt, D_MODEL), col(C_GH)),
            layer((1, D_MODEL)), layer((1, 128)), const((tt, tt)),
        ],
        out_specs=[
            pl.BlockSpec((nb, tt, D_MODEL), lambda b, c: (b, c, 0)),
            pl.BlockSpec((nb, HG_HEADS, HG_DK, HG_DV), lambda b, c: (b, 0, 0, 0)),
        ],
        out_shape=[
            jax.ShapeDtypeStruct((bsz, t, D_MODEL), BF16),
            jax.ShapeDtypeStruct((bsz, HG_HEADS, HG_DK, HG_DV), F32),
        ],
        scratch_shapes=[pltpu.VMEM((nb, HG_HEADS, HG_DV, HG_DK), F32), tile, tile, tile, tile],
        compiler_params=_params(("parallel", "arbitrary")),
        name="hgrn_prompt",
    )(proj3, proj3, proj3, proj3, lb, nw, tril)
    return o.reshape(bsz * t, D_MODEL), s


def _hgrn_sample_kernel(q_ref, f_ref, i_ref, g_ref, lb_ref, nw_ref, s_ref, o_ref, so_ref,
                        qs_ref, ks_ref, fs_ref, ob_ref):
    qx, kx, logf = _hgrn_prep(q_ref[...], f_ref[...], lb_ref[...])
    qs_ref[...] = qx
    ks_ref[...] = kx
    fs_ref[...] = jnp.exp(logf)
    ob_ref[...] = jnp.zeros_like(ob_ref)
    rows = lax.broadcasted_iota(jnp.int32, (SAMPLE_TB, 1), 0)

    heads = [slice(h * 128, (h + 1) * 128) for h in range(HG_HEADS)]

    unroll = 2

    def per_seqs(i, carry):
        work = [(i * unroll + j, rows == i * unroll + j, h, sl) for j in range(unroll)
                for h, sl in enumerate(heads)]
        fcol = [_col_bcast(jnp.where(sel, fs_ref[:, sl], 0.0), HG_DV) for b, sel, h, sl in work]
        kv = [_mm(jnp.where(sel, ks_ref[:, sl], 0.0), i_ref[:, sl], TN) for b, sel, h, sl in work]
        s_new = [fcol[n] * s_ref[b, h] + kv[n] for n, (b, sel, h, sl) in enumerate(work)]
        for s, (b, sel, h, sl) in zip(s_new, work):
            so_ref[b, h] = s
        outs = [_mm(jnp.where(sel, qs_ref[:, sl], 0.0), s) for s, (b, sel, h, sl) in zip(s_new, work)]
        for h, sl in enumerate(heads):
            ob_ref[:, sl] += sum(outs[j * HG_HEADS + h] for j in range(unroll))
        return carry

    lax.fori_loop(0, SAMPLE_TB // unroll, per_seqs, 0)
    nw = nw_ref[...]
    for h in range(HG_HEADS):
        sl = slice(h * 128, (h + 1) * 128)
        o_ref[:, sl] = _hgrn_post(ob_ref[:, sl], g_ref[:, sl], nw)


def _hgrn_sample(proj, lb, nw, state, l, new_state):
    bsz = state.shape[1]
    tb = SAMPLE_TB
    col = lambda blk: (lambda i: (i, blk))
    tile = pltpu.VMEM((tb, D_MODEL), F32)
    sspec = pl.BlockSpec((None, tb, HG_HEADS, HG_DK, HG_DV), lambda i: (l, i, 0, 0, 0))
    return _sample_call(
        _hgrn_sample_kernel, "hgrn_sample", bsz,
        [
            pl.BlockSpec((tb, D_MODEL), col(C_Q)), pl.BlockSpec((tb, D_MODEL), col(C_F)),
            pl.BlockSpec((tb, D_MODEL), col(C_I)), pl.BlockSpec((tb, D_MODEL), col(C_GH)),
            pl.BlockSpec((None, 1, D_MODEL), lambda i: (l, 0, 0)),
            pl.BlockSpec((None, 1, 128), lambda i: (l, 0, 0)),
            sspec,
        ],
        (proj, proj, proj, proj, lb, nw, state), sspec, state.shape, [tile, tile, tile, tile], new_state)


def _rw_lora(wdad, w2a2):
    lo = lax.broadcasted_iota(jnp.int32, wdad.shape, 1) < RW_LORA
    lora_w = _mm3(jnp.where(lo, jnp.tanh(wdad), 0.0), w2a2)
    lora_a = _mm3(jnp.where(lo, 0.0, wdad), w2a2)
    return lora_w, lora_a


def _rw_prep_all(items, ones_bd):
    rs, ks, vs, lws, a_sigs, kks = [], [], [], [], [], []
    for r, k0, v, lora_w, lora_a, w0, a0, kk_w, ka_w in items:
        lws.append(-RW_DECAY_SCALE * _sigmoid(w0 + lora_w))
        a_sig = _sigmoid(a0 + lora_a)
        a_sigs.append(a_sig)
        kks.append(k0 * kk_w)
        rs.append(r)
        vs.append(v)
        ks.append(k0 * (1.0 + (a_sig - 1.0) * ka_w))
    sq = _seg_many(ones_bd, [kk * kk for kk in kks])
    kks = [kk / jnp.maximum(jnp.sqrt(s), 1e-12) for kk, s in zip(kks, sq)]
    return rs, ks, vs, lws, [-kk for kk in kks], [kk * a for kk, a in zip(kks, a_sigs)]


def _rw_posts(os_, rs, ks, vs, gs, rk_ws, lnws, lnbs, ones_bd):
    idx = range(len(os_))
    sums = _seg_many(ones_bd, list(os_) + [rs[i] * ks[i] * rk_ws[i] for i in idx])
    ds = [os_[i] - sums[i] * (1.0 / RW_N) for i in idx]
    bonus = sums[len(os_):]
    var = [s * (1.0 / RW_N) for s in _seg_many(ones_bd, [d * d for d in ds])]
    return [(ds[i] * lax.rsqrt(var[i] + RW_LN_EPS) * lnws[i] + lnbs[i] + bonus[i] * vs[i]) * _silu(gs[i])
            for i in idx]


def _rw_chunks(rs, ks, vs, lws, avs, bvs, sbds, tril):
    n = RW_CHUNK
    pairs = range(len(rs))
    lane = lax.broadcasted_iota(jnp.int32, (n, 128), 1)
    row = lax.broadcasted_iota(jnp.int32, (n, 128), 0)
    m0 = lane < RW_N
    col = jnp.where(m0, lane, lane - RW_N)
    strict = col < row
    incl = col <= row
    eye = (col == row).astype(F32)

    def bd(x):
        xb = x.astype(BF16)
        zero = jnp.zeros_like(xb)
        return jnp.concatenate([jnp.where(m0, xb, zero), jnp.where(m0, zero, xb)], axis=0)

    cls = [_mm_exact(tril, lws[q]) for q in pairs]
    a_h = [avs[q] * jnp.exp(cls[q] - lws[q]) for q in pairs]
    r_h = [rs[q] * jnp.exp(cls[q]) for q in pairs]
    b_c = [bvs[q] * jnp.exp(-cls[q]) for q in pairs]
    k_c = [ks[q] * jnp.exp(-cls[q]) for q in pairs]
    lhs = [jnp.concatenate([a_h[q], r_h[q]], axis=0) for q in pairs]
    from_state = [_mm(lhs[q], sbds[q], NT) for q in pairs]
    m_bk = [_mm(lhs[q], jnp.concatenate([bd(b_c[q]), bd(k_c[q])], axis=0), NT) for q in pairs]
    m_ab = [jnp.where(strict, m[:n, :128], 0.0) for m in m_bk]
    m_ak = [jnp.where(strict, m[:n, 128:], 0.0) for m in m_bk]
    m_rb = [jnp.where(incl, m[n:, :128], 0.0) for m in m_bk]
    m_rk = [jnp.where(incl, m[n:, 128:], 0.0) for m in m_bk]
    ps = [_mm(m, bd(m)) for m in m_ab]
    xs = [eye + m for m in m_ab]
    for level in range(1, 5):
        both = [_mm(jnp.concatenate([ps[q], xs[q]], axis=0), bd(ps[q])) for q in pairs]
        ps = [m[:n] for m in both]
        xs = [xs[q] + both[q][n:] for q in pairs]
    xs = [xs[q] + _mm(xs[q], bd(ps[q])) for q in pairs]
    from_v = [_mm(jnp.concatenate([m_ak[q], m_rk[q]], axis=0), bd(vs[q])) for q in pairs]
    w_all = [from_state[q][:n] + from_v[q][:n] for q in pairs]
    u_all = [_mm(xs[q], bd(w_all[q])) for q in pairs]
    o_all = [from_state[q][n:] + from_v[q][n:] + _mm(m_rb[q], bd(u_all[q])) for q in pairs]

    vi = lax.broadcasted_iota(jnp.int32, (128, 128), 0)
    ki = lax.broadcasted_iota(jnp.int32, (128, 128), 1)
    same_head = (vi < RW_N) == (ki < RW_N)
    new = []
    for q in pairs:
        cl_end = cls[q][n - 1:n, :]
        e_end = jnp.exp(cl_end - cls[q])
        upd = _mm(jnp.concatenate([u_all[q], vs[q]], axis=0),
                  jnp.concatenate([bvs[q] * e_end, ks[q] * e_end], axis=0), TN)
        new.append(sbds[q] * jnp.exp(cl_end) + jnp.where(same_head, upd, 0.0))
    return o_all, new


def _rwkv_prompt_kernel(r_ref, k_ref, v_ref, g_ref, wdad_ref, mu_ref, w0_ref, a0_ref, kkw_ref, kaw_ref, rkw_ref,
                        lnw_ref, lnb_ref, w2a2_ref, tril_ref, ones_ref, o_ref, s_ref,
                        sbd_ref, pr_ref, pk_ref, pv_ref, pwd_ref, lw_ref, la_ref):
    c = pl.program_id(1)
    n = RW_CHUNK
    nb = r_ref.shape[0]

    @pl.when(c == 0)
    def _():
        sbd_ref[...] = jnp.zeros_like(sbd_ref)
        pr_ref[...] = jnp.zeros_like(pr_ref)
        pk_ref[...] = jnp.zeros_like(pk_ref)
        pv_ref[...] = jnp.zeros_like(pv_ref)
        pwd_ref[...] = jnp.zeros_like(pwd_ref)

    row = lax.broadcasted_iota(jnp.int32, (n, 128), 0)

    def shift(z_ref, prev_ref, i, sl, mu_lo):
        z = z_ref[i, :, sl]
        zp = jnp.where(row == 0, prev_ref[i, :, sl], pltpu.roll(z, 1, axis=0))
        prev_ref[i, :, sl] = z[n - 1:n, :]
        return z + (zp - z) * mu_ref[:, mu_lo:mu_lo + 128]

    for i in range(nb):
        lora_w, lora_a = _rw_lora(shift(wdad_ref, pwd_ref, i, slice(0, 128), 3 * D_MODEL), w2a2_ref[...])
        lw_ref[i] = lora_w
        la_ref[i] = lora_a
    ones_bd = ones_ref[...]
    lanes = [slice(q * 128, (q + 1) * 128) for q in range(RW_HEADS // 2)]
    pairs = [(i, q, sl) for i in range(nb) for q, sl in enumerate(lanes)]
    rs, ks, vs, lws, avs, bvs = _rw_prep_all(
        [(shift(r_ref, pr_ref, i, sl, q * 128), shift(k_ref, pk_ref, i, sl, D_MODEL + q * 128),
          shift(v_ref, pv_ref, i, sl, 2 * D_MODEL + q * 128), lw_ref[i, :, sl], la_ref[i, :, sl],
          w0_ref[:, sl], a0_ref[:, sl], kkw_ref[:, sl], kaw_ref[:, sl]) for i, q, sl in pairs], ones_bd)
    o_all, sbd_new = _rw_chunks(rs, ks, vs, lws, avs, bvs, [sbd_ref[i, q] for i, q, sl in pairs], tril_ref[...])
    for (i, q, sl), new in zip(pairs, sbd_new):
        sbd_ref[i, q] = new
    outs = _rw_posts(o_all, rs, ks, vs, [g_ref[i, :, sl] for i, q, sl in pairs],
                     [rkw_ref[:, sl] for i, q, sl in pairs], [lnw_ref[:, sl] for i, q, sl in pairs],
                     [lnb_ref[:, sl] for i, q, sl in pairs], ones_bd)
    for (i, q, sl), out in zip(pairs, outs):
        o_ref[i, :, sl] = out.astype(o_ref.dtype)

    @pl.when(c == pl.num_programs(1) - 1)
    def _():
        s_ref[...] = sbd_ref[...]


def _rw_consts():
    tril = jnp.asarray(np.tril(np.ones((RW_CHUNK, RW_CHUNK))), BF16)
    ones_bd = jnp.asarray(np.kron(np.eye(2), np.ones((RW_N, RW_N))), BF16)
    return tril, ones_bd


def _rwkv_prompt(proj, p, bsz, t, l):
    n = RW_CHUNK
    n_t = t // n
    nb = RW_NB if bsz % RW_NB == 0 else 1
    tril, ones_bd = _rw_consts()
    proj3 = proj.reshape(bsz, t, NCOL)
    col = lambda blk: (lambda b, c: (b, c, blk))
    const = lambda shape: pl.BlockSpec(shape, lambda b, c: (0, 0))
    layer = lambda shape: pl.BlockSpec((None,) + shape, lambda b, c: (l, 0, 0))
    vec = layer((1, D_MODEL))
    prev = pltpu.VMEM((nb, 1, D_MODEL), F32)
    tile = pltpu.VMEM((nb, n, D_MODEL), F32)
    o, sbd = pl.pallas_call(
        _rwkv_prompt_kernel,
        grid=(bsz // nb, n_t),
        in_specs=[
            pl.BlockSpec((nb, n, D_MODEL), col(C_R)), pl.BlockSpec((nb, n, D_MODEL), col(C_K)),
            pl.BlockSpec((nb, n, D_MODEL), col(C_V)), pl.BlockSpec((nb, n, D_MODEL), col(C_GR)),
            pl.BlockSpec((nb, n, 128), lambda b, c: (b, c, C_WDAD)),
            layer((1, RW_SHIFT)),
            vec, vec, vec, vec, vec, vec, vec,
            layer((128, D_MODEL)), const((n, n)), const((128, 128)),
        ],
        out_specs=[
            pl.BlockSpec((nb, n, D_MODEL), lambda b, c: (b, c, 0)),
            pl.BlockSpec((nb, RW_HEADS // 2, 128, 128), lambda b, c: (b, 0, 0, 0)),
        ],
        out_shape=[
            jax.ShapeDtypeStruct((bsz, t, D_MODEL), BF16),
            jax.ShapeDtypeStruct((bsz, RW_HEADS // 2, 128, 128), F32),
        ],
        scratch_shapes=[pltpu.VMEM((nb, RW_HEADS // 2, 128, 128), F32), prev, prev, prev,
                        pltpu.VMEM((nb, 1, 128), F32), tile, tile],
        compiler_params=_params(("parallel", "arbitrary")),
        name="rwkv_prompt",
    )(proj3, proj3, proj3, proj3, proj3, p["mu"], p["w0"], p["a0"], p["kk"], p["ka"], p["rk"], p["lnw"], p["lnb"],
      p["w2a2"], tril, ones_bd)
    s6 = sbd.reshape(bsz, RW_HEADS // 2, 2, RW_N, 2, RW_N)
    state = jnp.stack([s6[:, :, 0, :, 0, :], s6[:, :, 1, :, 1, :]], axis=2)
    return o.reshape(bsz * t, D_MODEL), state.reshape(bsz, RW_HEADS, RW_N, RW_N)


def _rwkv_sample_prep_kernel(r_ref, k_ref, v_ref, wdad_ref, sh_ref, mu_ref, w0_ref, a0_ref, kkw_ref, kaw_ref,
                             w2a2_ref, ones_ref, r_o, k_o, v_o, w_o, a_o, b_o):
    def shift(z, lo, width):
        return z + (sh_ref[:, lo:lo + width] - z) * mu_ref[:, lo:lo + width]

    lora_w, lora_a = _rw_lora(shift(wdad_ref[...], 3 * D_MODEL, 128), w2a2_ref[...])
    pairs = [slice(q * 128, (q + 1) * 128) for q in range(RW_HEADS // 2)]
    rs, ks, vs, lws, avs, bvs = _rw_prep_all(
        [(shift(r_ref[:, sl], q * 128, 128), shift(k_ref[:, sl], D_MODEL + q * 128, 128),
          shift(v_ref[:, sl], 2 * D_MODEL + q * 128, 128), lora_w[:, sl], lora_a[:, sl], w0_ref[:, sl],
          a0_ref[:, sl], kkw_ref[:, sl], kaw_ref[:, sl]) for q, sl in enumerate(pairs)], ones_ref[...])
    for q, sl in enumerate(pairs):
        r_o[:, sl] = rs[q]
        k_o[:, sl] = ks[q]
        v_o[:, sl] = vs[q]
        w_o[:, sl] = jnp.exp(lws[q])
        a_o[:, sl] = avs[q]
        b_o[:, sl] = bvs[q]


def _rwkv_sample_state_kernel(s_ref, w_ref, a_ref, b_ref, k_ref, v_ref, r_ref, o_ref, so_ref):
    row8 = lax.broadcasted_iota(jnp.int32, (8, s_ref.shape[-1]), 0)
    for hh in range(s_ref.shape[0]):
        keys = slice(hh * RW_N, (hh + 1) * RW_N)
        w, a, bk, kk, r = w_ref[keys, :], a_ref[keys, :], b_ref[keys, :], k_ref[keys, :], r_ref[keys, :]

        def eight_values(v8, carry):
            base = pl.multiple_of(v8 * 8, 8)
            vals = v_ref[pl.ds(hh * RW_N + base, 8), :]
            outs = jnp.zeros_like(vals)
            for j in range(8):
                s_old = s_ref[hh, base + j]
                sa = jnp.sum(s_old * a, axis=0, keepdims=True)
                s_new = s_old * w + sa * bk + vals[j:j + 1, :] * kk
                so_ref[hh, base + j] = s_new
                outs = jnp.where(row8 == j, jnp.sum(s_new * r, axis=0, keepdims=True), outs)
            o_ref[pl.ds(hh * RW_N + base, 8), :] = outs
            return carry

        lax.fori_loop(0, RW_N // 8, eight_values, 0)


def _rwkv_sample_post_kernel(o_ref, r_ref, k_ref, v_ref, g_ref, rkw_ref, lnw_ref, lnb_ref, ones_ref, out_ref):
    pairs = [slice(q * 128, (q + 1) * 128) for q in range(RW_HEADS // 2)]
    ref = lambda x: [x[:, sl] for sl in pairs]
    outs = _rw_posts(ref(o_ref), ref(r_ref), ref(k_ref), ref(v_ref), ref(g_ref), ref(rkw_ref), ref(lnw_ref),
                     ref(lnb_ref), ones_ref[...])
    for sl, out in zip(pairs, outs):
        out_ref[:, sl] = out


def _rwkv_sample(proj, p, shift_state, state_t, l, new_state_t):
    bsz = proj.shape[0]
    _, ones_bd = _rw_consts()
    whole = lambda shape: pl.BlockSpec(shape, lambda i: tuple(0 for _ in shape))
    layer = lambda shape: pl.BlockSpec((None,) + shape, lambda i: (l,) + tuple(0 for _ in shape))
    col = lambda blk, w=D_MODEL: pl.BlockSpec((bsz, w), lambda i: (0, blk))
    vec = layer((1, D_MODEL))
    act = jax.ShapeDtypeStruct((bsz, D_MODEL), F32)
    r, k, v, w, a, b = pl.pallas_call(
        _rwkv_sample_prep_kernel,
        grid=(1,),
        in_specs=[col(C_R), col(C_K), col(C_V), col(C_WDAD, 128), layer((bsz, RW_SHIFT)), layer((1, RW_SHIFT)),
                  vec, vec, vec, vec, layer((128, D_MODEL)), whole((128, 128))],
        out_specs=[whole((bsz, D_MODEL))] * 6,
        out_shape=[act] * 6,
        compiler_params=_params(("arbitrary",)),
        name="rwkv_sample_prep",
    )(proj, proj, proj, proj, shift_state, p["mu"], p["w0"], p["a0"], p["kk"], p["ka"], p["w2a2"], ones_bd)

    hb = 2
    rows = pl.BlockSpec((hb * RW_N, bsz), lambda i: (i, 0))
    sspec = pl.BlockSpec((None, hb, RW_N, RW_N, bsz), lambda i: (l, i, 0, 0, 0))
    args = (state_t, w.T, a.T, b.T, k.T, v.T, r.T)
    in_specs = [sspec] + [rows] * 6
    aliases = {}
    body = _rwkv_sample_state_kernel
    if new_state_t is not None:
        in_specs = in_specs + [pl.BlockSpec(memory_space=pl.ANY)]
        args = args + (new_state_t,)
        aliases = {len(args) - 1: 1}
        body = lambda *refs: _rwkv_sample_state_kernel(*refs[:7], *refs[8:])
    o_t, new_state_t = pl.pallas_call(
        body,
        grid=(RW_HEADS // hb,),
        in_specs=in_specs,
        out_specs=[rows, sspec],
        out_shape=[jax.ShapeDtypeStruct((D_MODEL, bsz), F32), jax.ShapeDtypeStruct(state_t.shape, F32)],
        input_output_aliases=aliases,
        compiler_params=_params(("parallel",)),
        name="rwkv_sample_state",
    )(*args)

    o = pl.pallas_call(
        _rwkv_sample_post_kernel,
        grid=(1,),
        in_specs=[whole((bsz, D_MODEL))] * 4 + [col(C_GR), vec, vec, vec, whole((128, 128))],
        out_specs=whole((bsz, D_MODEL)),
        out_shape=act,
        compiler_params=_params(("arbitrary",)),
        name="rwkv_sample_post",
    )(o_t.T, r, k, v, proj, p["rk"], p["lnw"], p["lnb"], ones_bd)
    return o, new_state_t


def _mb_post(y, xs, z, d_rep, nw):
    y = (y + d_rep * xs) * _silu(z)
    ms = jnp.mean(y * y, axis=-1, keepdims=True)
    return y * lax.rsqrt(ms + RMS_EPS) * nw


def _ssd_prompt_kernel(x_ref, bc_ref, z_ref, dt_ref, cw_ref, cb_ref, dtb_ref, alog_ref, d_ref, nw_ref, tril_ref,
                       exp_ref, o_ref, s_ref, h_ref, px_ref, pbc_ref):
    c = pl.program_id(1)
    n = MB_CHUNK
    nb = x_ref.shape[0]

    @pl.when(c == 0)
    def _():
        h_ref[...] = jnp.zeros_like(h_ref)
        px_ref[...] = jnp.zeros_like(px_ref)
        pbc_ref[...] = jnp.zeros_like(pbc_ref)

    def conv(u_ref, prev_ref, i, sl, wsl):
        u = u_ref[i, :, sl]
        prev = prev_ref[i, :, sl]
        w = cw_ref[:, wsl]
        row = lax.broadcasted_iota(jnp.int32, (8, u.shape[1]), 0)
        acc = u * w[MB_CONV - 1:MB_CONV, :] + cb_ref[:, wsl]
        for s in range(1, MB_CONV):
            us = pltpu.roll(u, s, axis=0)
            top = jnp.where(row < s, pltpu.roll(prev, s, axis=0), us[0:8])
            us = jnp.concatenate([top, us[8:]], axis=0)
            acc = acc + us * w[MB_CONV - 1 - s:MB_CONV - s, :]
        prev_ref[i, :, sl] = u[n - 8:n, :]
        return _silu(acc)

    tril = tril_ref[...]
    ti = lax.broadcasted_iota(jnp.int32, (n, n), 0)
    si = lax.broadcasted_iota(jnp.int32, (n, n), 1)
    incl = si <= ti
    triu = (ti <= si).astype(BF16)
    lane = lax.broadcasted_iota(jnp.int32, (n, 4 * MB_P), 1)
    nbc = MB_GROUPS * MB_N
    seqs = range(nb)
    groups = [(i, g, slice(g * 256, (g + 1) * 256)) for i in seqs for g in range(MB_GROUPS)]
    idx = range(len(groups))

    spread = exp_ref[...]
    dt_c = [_softplus(dt_ref[i] + dtb_ref[...]) for i in seqs]
    a_c = [dt_c[i] * (-jnp.exp(alog_ref[...])) for i in seqs]
    cs_c = [_mm_exact(tril, a_c[i]) for i in seqs]
    cs_rows = [_mm_exact(triu, a_c[i], TN, x_is_lhs=True) for i in seqs]
    dt_all = [_mm_exact(spread, dt_c[i], x_is_lhs=True) for i in seqs]
    cs_all = [_mm_exact(spread, cs_c[i], x_is_lhs=True) for i in seqs]

    xs = [conv(x_ref, px_ref, i, gx, gx) for i, g, gx in groups]
    bm = [conv(bc_ref, pbc_ref, i, slice(g * MB_N, (g + 1) * MB_N),
               slice(D_MODEL + g * MB_N, D_MODEL + (g + 1) * MB_N)) for i, g, gx in groups]
    cm = [conv(bc_ref, pbc_ref, i, slice(nbc + g * MB_N, nbc + (g + 1) * MB_N),
               slice(D_MODEL + nbc + g * MB_N, D_MODEL + nbc + (g + 1) * MB_N)) for i, g, gx in groups]
    xdt = [xs[k] * dt_all[i][:, gx] for k, (i, g, gx) in enumerate(groups)]
    cs = [cs_all[i][:, gx] for i, g, gx in groups]
    gmat = [_mm(cm[k], bm[k], NT) for k in idx]
    h_all = [h_ref[i, g] for i, g, gx in groups]
    y = [jnp.exp(cs[k]) * _mm(cm[k], h_all[k], NT) for k in idx]
    for k, (i, g, gx) in enumerate(groups):
        for e in range(4):
            hd = slice(e * MB_P, (e + 1) * MB_P)
            hrow = slice(4 * g + e, 4 * g + e + 1)
            lmat = jnp.where(incl, jnp.exp(cs[k][:, hd] - cs_rows[i][hrow, :]), 0.0)
            me = (lane >= e * MB_P) & (lane < (e + 1) * MB_P)
            y[k] = y[k] + _mm(gmat[k] * lmat, jnp.where(me, xdt[k], 0.0))
    for k, (i, g, gx) in enumerate(groups):
        cs_end = cs[k][n - 1:n, :]
        upd = _mm(xdt[k] * jnp.exp(cs_end - cs[k]), bm[k], TN)
        decay = [jnp.exp(cs_rows[i][4 * g + e:4 * g + e + 1, n - 1:n]) for e in range(4)]
        h_ref[i, g] = jnp.concatenate([h_all[k][e * MB_P:(e + 1) * MB_P] * decay[e] for e in range(4)],
                                      axis=0) + upd
    for k, (i, g, gx) in enumerate(groups):
        o_ref[i, :, gx] = _mb_post(y[k], xs[k], z_ref[i, :, gx], d_ref[:, gx], nw_ref[:, gx]).astype(o_ref.dtype)

    @pl.when(c == pl.num_programs(1) - 1)
    def _():
        for i, g, gx in groups:
            for e in range(4):
                s_ref[i, 4 * g + e] = h_ref[i, g, e * MB_P:(e + 1) * MB_P, :]


def _head_spread():
    m = np.zeros((128, D_MODEL), np.float32)
    for h in range(MB_HEADS):
        m[h, h * MB_P:(h + 1) * MB_P] = 1.0
    return jnp.asarray(m, BF16)


def _ssd_prompt(proj, p, bsz, t, l):
    n = MB_CHUNK
    n_t = t // n
    nb = MB_NB if bsz % MB_NB == 0 else 1
    tril = jnp.asarray(np.tril(np.ones((n, n))), BF16)
    proj3 = proj.reshape(bsz, t, NCOL)
    col = lambda blk: (lambda b, c: (b, c, blk))
    const = lambda shape: pl.BlockSpec(shape, lambda b, c: (0, 0))
    layer = lambda shape: pl.BlockSpec((None,) + shape, lambda b, c: (l, 0, 0))
    vec = layer((1, D_MODEL))
    o, s = pl.pallas_call(
        _ssd_prompt_kernel,
        grid=(bsz // nb, n_t),
        in_specs=[
            pl.BlockSpec((nb, n, D_MODEL), col(C_X)), pl.BlockSpec((nb, n, D_MODEL), col(C_BC)),
            pl.BlockSpec((nb, n, D_MODEL), col(C_Z)), pl.BlockSpec((nb, n, 128), col(C_DT)),
            layer((MB_CONV, MB_XBC)), layer((1, MB_XBC)),
            layer((1, 128)), layer((1, 128)), vec, vec,
            const((n, n)), const((128, D_MODEL)),
        ],
        out_specs=[
            pl.BlockSpec((nb, n, D_MODEL), lambda b, c: (b, c, 0)),
            pl.BlockSpec((nb, MB_HEADS, MB_P, MB_N), lambda b, c: (b, 0, 0, 0)),
        ],
        out_shape=[
            jax.ShapeDtypeStruct((bsz, t, D_MODEL), BF16),
            jax.ShapeDtypeStruct((bsz, MB_HEADS, MB_P, MB_N), F32),
        ],
        scratch_shapes=[pltpu.VMEM((nb, MB_GROUPS, 4 * MB_P, MB_N), F32), pltpu.VMEM((nb, 8, D_MODEL), F32),
                        pltpu.VMEM((nb, 8, D_MODEL), F32)],
        compiler_params=_params(("parallel", "arbitrary")),
        name="ssd_prompt",
    )(proj3, proj3, proj3, proj3, p["cw"], p["cb"], p["dtb"], p["alog"], p["d"], p["nw"], tril, _head_spread())
    return o.reshape(bsz * t, D_MODEL), s


def _ssd_sample_kernel(xbc_ref, z_ref, dt_ref, cs_ref, cw_ref, cb_ref, dtb_ref, alog_ref, d_ref, nw_ref, exp_ref,
                       s_ref, o_ref, so_ref, xs_ref, xd_ref, bs_ref, cms_ref, dec_ref, ob_ref):
    tb = SAMPLE_TB
    w = cw_ref[...]
    acc = xbc_ref[...] * w[MB_CONV - 1:MB_CONV, :] + cb_ref[...]
    for i in range(MB_CONV - 1):
        acc = acc + cs_ref[i] * w[i:i + 1, :]
    xbc = _silu(acc)
    xs = xbc[:, :D_MODEL]
    dt = _softplus(dt_ref[...] + dtb_ref[...])
    xs_ref[...] = xs
    xd_ref[...] = xs * _mm_exact(exp_ref[...], dt, x_is_lhs=True)
    bs_ref[...] = xbc[:, D_MODEL:D_MODEL + MB_GROUPS * MB_N]
    cms_ref[...] = xbc[:, D_MODEL + MB_GROUPS * MB_N:]
    dec = jnp.exp(dt * (-jnp.exp(alog_ref[...])))
    for i in range(tb):
        dec_ref[i] = dec[i:i + 1, :]
    ob_ref[...] = jnp.zeros_like(ob_ref)
    rows = lax.broadcasted_iota(jnp.int32, (tb, 1), 0)

    hs = range(MB_HEADS)
    lanes = [slice(h * MB_P, (h + 1) * MB_P) for h in hs]
    grp = [slice((h // 4) * MB_N, (h // 4 + 1) * MB_N) for h in hs]
    unroll = 2

    def per_seqs(i, carry):
        work = [(i * unroll + j, rows == i * unroll + j, h) for j in range(unroll) for h in hs]

        def dec_of(b, h):
            return dec_ref[b][:, h:h + 1]

        upd = [_mm(jnp.where(sel, xd_ref[:, lanes[h]], 0.0), bs_ref[:, grp[h]], TN) for b, sel, h in work]
        h_new = [dec_of(b, h) * s_ref[b, h] + upd[n] for n, (b, sel, h) in enumerate(work)]
        for s, (b, sel, h) in zip(h_new, work):
            so_ref[b, h] = s
        outs = [_mm(jnp.where(sel, cms_ref[:, grp[h]], 0.0), s, NT) for s, (b, sel, h) in zip(h_new, work)]
        for h in hs:
            ob_ref[:, lanes[h]] += sum(outs[j * MB_HEADS + h] for j in range(unroll))
        return carry

    lax.fori_loop(0, tb // unroll, per_seqs, 0)
    for g in range(MB_GROUPS):
        sl = slice(g * 256, (g + 1) * 256)
        o_ref[:, sl] = _mb_post(ob_ref[:, sl], xs_ref[:, sl], z_ref[:, sl], d_ref[:, sl], nw_ref[:, sl])


def _ssd_sample(proj, p, conv_state, state, l, new_state):
    bsz = state.shape[1]
    tb = SAMPLE_TB
    layer = lambda shape: pl.BlockSpec((None,) + shape, lambda i: (l, 0, 0))
    vec = layer((1, D_MODEL))
    tile = pltpu.VMEM((tb, D_MODEL), F32)
    half = pltpu.VMEM((tb, MB_GROUPS * MB_N), F32)
    sspec = pl.BlockSpec((None, tb, MB_HEADS, MB_P, MB_N), lambda i: (l, i, 0, 0, 0))
    return _sample_call(
        _ssd_sample_kernel, "ssd_sample", bsz,
        [
            pl.BlockSpec((tb, MB_XBC), lambda i: (i, C_X // 2)),
            pl.BlockSpec((tb, D_MODEL), lambda i: (i, C_Z)),
            pl.BlockSpec((tb, 128), lambda i: (i, C_DT)),
            pl.BlockSpec((None, MB_CONV - 1, tb, MB_XBC), lambda i: (l, 0, i, 0)),
            layer((MB_CONV, MB_XBC)), layer((1, MB_XBC)),
            layer((1, 128)), layer((1, 128)), vec, vec,
            pl.BlockSpec((128, D_MODEL), lambda i: (0, 0)),
            sspec,
        ],
        (proj, proj, proj, conv_state, p["cw"], p["cb"], p["dtb"], p["alog"], p["d"], p["nw"], _head_spread(),
         state),
        sspec, state.shape, [tile, tile, half, half, pltpu.VMEM((tb, 1, 128), F32), tile], new_state)


def _prep_w_in(w_in):
    o = np.cumsum([0, 1024, 1024, 1024, 1024, RW_SHIFT, 1024, 1024, MB_XBC, MB_HEADS, 3 * D_MODEL])
    zrw, grw, zmb, xbc, dtc, gate = o[4], o[5], o[6], o[7], o[8], o[9]
    parts = [(0, zrw), (zrw, zrw + 3 * D_MODEL), (grw, zmb), (xbc, dtc), (zmb, xbc), (gate, o[10]),
             (zrw + 3 * D_MODEL, grw), (dtc, gate)]
    out = jnp.zeros(w_in.shape[:-1] + (NCOL,), BF16)
    at = 0
    for lo, hi in parts:
        out = lax.dynamic_update_slice_in_dim(out, w_in[..., lo:hi].astype(BF16), at, axis=-1)
        at += hi - lo
    return out


def _rep(v):
    return jnp.repeat(v, MB_P, axis=-1)[:, None, :]


def _pad_heads(v):
    return jnp.pad(v, ((0, 0), (0, 128 - MB_HEADS)))[:, None, :]


def kernel(x_prompt, x_sample, state_hgrn, state_rwkv, state_rwkv_shift, state_ssm, state_conv, norm_w, w_in, hg_lb, hg_norm_w, rw_mu, rw_w0, rw_w2, rw_a0, rw_a2, rw_k_k, rw_k_a, rw_r_k, rw_ln_w, rw_ln_b, mb_conv_w, mb_conv_b, mb_dt_bias, mb_A_log, mb_D, mb_norm_w, w_o_hg, w_o_rw, w_o_mb, w_out, final_norm_w):
    bp, t, _ = x_prompt.shape
    bs = x_sample.shape[0]
    sm = jax.nn.softmax(hg_lb.astype(F32), axis=0)
    row = lambda a: a[:, None, :]
    lbs = row(jnp.cumsum(sm, axis=0) - sm[0:1])
    w_in_r = _prep_w_in(w_in)
    rw_p = dict(mu=row(rw_mu), w0=row(rw_w0), a0=row(rw_a0), kk=row(rw_k_k), ka=row(rw_k_a),
                rk=rw_r_k.reshape(DEPTH, 1, D_MODEL), lnw=row(rw_ln_w), lnb=row(rw_ln_b),
                w2a2=jnp.concatenate([rw_w2, rw_a2], axis=1))
    mb_p = dict(cw=mb_conv_w, cb=row(mb_conv_b), dtb=_pad_heads(mb_dt_bias), alog=_pad_heads(mb_A_log), d=_rep(mb_D),
                nw=row(mb_norm_w))
    nw_in, nw_hg = row(norm_w), row(hg_norm_w)
    whg, wrw, wmb, wout = (w.astype(BF16) for w in (w_o_hg, w_o_rw, w_o_mb, w_out))
    fw = final_norm_w[None, :]
    conv_t = jnp.swapaxes(state_conv, 1, 2)
    rwkv_t = jnp.transpose(state_rwkv, (0, 2, 3, 4, 1))
    lora = slice(C_WDAD * 128, (C_WDAD + 1) * 128)

    hp = x_prompt.reshape(bp * t, D_MODEL)
    hs = x_sample.reshape(bs, D_MODEL)
    p_states, s_small = [], []
    s_hg = s_rw = s_ssm = None
    yp = ys = None
    for l in range(DEPTH):
        final = l == DEPTH - 1

        proj = _inproj(hp, nw_in, w_in_r, l)
        o_hg, p_hg = _hgrn_prompt(proj, lbs, nw_hg, bp, t, l)
        o_rw, p_rw = _rwkv_prompt(proj, rw_p, bp, t, l)
        o_mb, p_ssm = _ssd_prompt(proj, mb_p, bp, t, l)
        proj3 = proj.reshape(bp, t, NCOL)
        p_shift = jnp.concatenate([proj3[:, -1, C_R * 1024:C_GR * 1024], proj3[:, -1, lora]], axis=-1)
        p_conv = proj3[:, t - (MB_CONV - 1):, C_X * 1024:C_Z * 1024]
        res = _merge(o_hg, o_rw, o_mb, proj, hp, whg, wrw, wmb, wout, fw, final, l)
        hp = res[0]
        if final:
            yp = res[1]
        p_states.append((p_hg, p_rw, p_shift, p_ssm, p_conv))

        proj = _inproj(hs, nw_in, w_in_r, l)
        o_hg, s_hg = _hgrn_sample(proj, lbs, nw_hg, state_hgrn, l, s_hg)
        o_rw, s_rw = _rwkv_sample(proj, rw_p, state_rwkv_shift, rwkv_t, l, s_rw)
        o_mb, s_ssm = _ssd_sample(proj, mb_p, conv_t, state_ssm, l, s_ssm)
        s_shift = jnp.concatenate([proj[:, C_R * 1024:C_GR * 1024], proj[:, lora]], axis=-1)
        s_conv = jnp.concatenate([state_conv[l][:, 1:], proj[:, None, C_X * 1024:C_Z * 1024]], axis=1)
        res = _merge(o_hg, o_rw, o_mb, proj, hs, whg, wrw, wmb, wout, fw, final, l)
        hs = res[0]
        if final:
            ys = res[1]
        s_small.append((s_shift, s_conv))

    stack = lambda states, i: jnp.stack([s[i] for s in states])
    return (yp.reshape(bp, t, D_MODEL), ys.reshape(bs, 1, D_MODEL),
            *[stack(p_states, i) for i in range(5)],
            s_hg, jnp.transpose(s_rw, (0, 4, 1, 2, 3)), stack(s_small, 0), s_ssm, stack(s_small, 1))
```

```python
import functools

import numpy as np
import jax
import jax.numpy as jnp
from jax import lax
from jax.experimental import pallas as pl
from jax.experimental.pallas import tpu as pltpu

F32 = jnp.float32
BF16 = jnp.bfloat16

D_MODEL = 1024
DEPTH = 2
HG_HEADS, HG_DK, HG_DV = 8, 128, 128
HG_F_MIN = 1e-20
LOG2E = 1.4426950408889634
RW_HEADS, RW_N = 16, 64
RW_LORA = 64
RW_SHIFT = 3 * D_MODEL + 2 * RW_LORA
RW_LN_EPS = 64e-5
RW_DECAY_SCALE = float(np.exp(-0.5))
MB_HEADS, MB_P, MB_GROUPS, MB_N, MB_CONV = 16, 64, 4, 128, 4
MB_XBC = D_MODEL + 2 * MB_GROUPS * MB_N
RMS_EPS = 1e-6

C_Q, C_F, C_I, C_GH, C_R, C_K, C_V, C_GR, C_X, C_BC, C_Z, C_G0, C_G1, C_G2 = range(14)
C_WDAD = 14 * 8
C_DT = 14 * 8 + 1
NCOL = 14 * 1024 + 2 * 128

HG_BLK = 16
HG_NB = 2
RW_CHUNK = 64
RW_NB = 4
MB_CHUNK = 64
MB_NB = 4
SAMPLE_TB = 8

VMEM_LIMIT = 48 * 1024 * 1024

NN = (((1,), (0,)), ((), ()))
NT = (((1,), (1,)), ((), ()))
TN = (((0,), (0,)), ((), ()))


def _mm(a, b, dims=NN):
    return lax.dot_general(a.astype(BF16), b.astype(BF16), dims, preferred_element_type=F32)


def _mm_exact(e, x, dims=NN, x_is_lhs=False, passes=3):
    e = e.astype(BF16)
    acc = None
    for _ in range(passes):
        p = x.astype(BF16)
        x = x - p.astype(F32)
        t = (lax.dot_general(p, e, dims, preferred_element_type=F32) if x_is_lhs
             else lax.dot_general(e, p, dims, preferred_element_type=F32))
        acc = t if acc is None else acc + t
    return acc


def _mm3(a, b, dims=NN):
    a_hi = a.astype(BF16)
    a_lo = (a - a_hi.astype(F32)).astype(BF16)
    b_hi = b.astype(BF16)
    b_lo = (b - b_hi.astype(F32)).astype(BF16)
    dot = lambda x, y: lax.dot_general(x, y, dims, preferred_element_type=F32)
    return dot(a_hi, b_hi) + (dot(a_hi, b_lo) + dot(a_lo, b_hi))


def _seg_many(ones_bd, xs):
    m = xs[0].shape[0]
    out = _mm_exact(ones_bd, jnp.concatenate(xs, axis=0) if len(xs) > 1 else xs[0], x_is_lhs=True, passes=2)
    return [out[i * m:(i + 1) * m] for i in range(len(xs))]


def _sigmoid(x):
    return jax.nn.sigmoid(x)


def _silu(x):
    return x * jax.nn.sigmoid(x)


def _softplus(x):
    return jnp.maximum(x, 0.0) + jnp.log1p(jnp.exp(-jnp.abs(x)))


def _col_bcast(rows, width):
    ones = jnp.ones((rows.shape[0], width), BF16)
    p1 = rows.astype(BF16)
    r1 = rows - p1.astype(F32)
    p2 = r1.astype(BF16)
    p3 = (r1 - p2.astype(F32)).astype(BF16)
    out = lax.dot_general(p1, ones, TN, preferred_element_type=F32)
    out = out + lax.dot_general(p2, ones, TN, preferred_element_type=F32)
    return out + lax.dot_general(p3, ones, TN, preferred_element_type=F32)


def _params(sem):
    return pltpu.CompilerParams(dimension_semantics=sem, vmem_limit_bytes=VMEM_LIMIT)


def _inproj_kernel(x_ref, nw_ref, w_ref, o_ref, xn_ref):
    @pl.when(pl.program_id(1) == 0)
    def _():
        x = x_ref[...]
        ms = jnp.mean(x * x, axis=-1, keepdims=True)
        xn_ref[...] = (x * lax.rsqrt(ms + RMS_EPS) * nw_ref[...]).astype(BF16)

    o_ref[...] = jnp.dot(xn_ref[...], w_ref[...], preferred_element_type=F32)


def _inproj(x2d, nw, w, l):
    m = x2d.shape[0]
    tm = min(2048, m)
    tn = 768
    return pl.pallas_call(
        _inproj_kernel,
        grid=(m // tm, NCOL // tn),
        in_specs=[
            pl.BlockSpec((tm, D_MODEL), lambda i, j: (i, 0)),
            pl.BlockSpec((None, 1, D_MODEL), lambda i, j: (l, 0, 0)),
            pl.BlockSpec((None, D_MODEL, tn), lambda i, j: (l, 0, j)),
        ],
        out_specs=pl.BlockSpec((tm, tn), lambda i, j: (i, j)),
        out_shape=jax.ShapeDtypeStruct((m, NCOL), F32),
        scratch_shapes=[pltpu.VMEM((tm, D_MODEL), BF16)],
        compiler_params=_params(("parallel", "arbitrary")),
        name="inproj",
    )(x2d, nw, w)


def _merge_kernel(ohg, orw, omb, g0, g1, g2, h_ref, whg, wrw, wmb, wout, fw_ref, hn_ref, *y_ref):
    u = _sigmoid(g0[...]) * _mm(ohg[...], whg[...])
    u = u + _sigmoid(g1[...]) * _mm(orw[...], wrw[...])
    u = u + _sigmoid(g2[...]) * _mm(omb[...], wmb[...])
    hn = h_ref[...] + _mm(u, wout[...])
    hn_ref[...] = hn
    if y_ref:
        ms = jnp.mean(hn * hn, axis=-1, keepdims=True)
        y_ref[0][...] = hn * lax.rsqrt(ms + RMS_EPS) * fw_ref[...]


def _merge(ohg, orw, omb, proj, h2d, whg, wrw, wmb, wout, fw, final, l):
    m = h2d.shape[0]
    tm = min(256, m)
    row = lambda i: (i, 0)
    const = lambda i: (0, 0)
    wspec = pl.BlockSpec((None, D_MODEL, D_MODEL), lambda i: (l, 0, 0))
    out_shape = [jax.ShapeDtypeStruct((m, D_MODEL), F32)]
    out_specs = [pl.BlockSpec((tm, D_MODEL), row)]
    if final:
        out_shape.append(jax.ShapeDtypeStruct((m, D_MODEL), F32))
        out_specs.append(pl.BlockSpec((tm, D_MODEL), row))
    return pl.pallas_call(
        _merge_kernel,
        grid=(m // tm,),
        in_specs=[
            pl.BlockSpec((tm, D_MODEL), row), pl.BlockSpec((tm, D_MODEL), row), pl.BlockSpec((tm, D_MODEL), row),
            pl.BlockSpec((tm, D_MODEL), lambda i: (i, C_G0)),
            pl.BlockSpec((tm, D_MODEL), lambda i: (i, C_G1)),
            pl.BlockSpec((tm, D_MODEL), lambda i: (i, C_G2)),
            pl.BlockSpec((tm, D_MODEL), row),
            wspec, wspec, wspec, wspec,
            pl.BlockSpec((1, D_MODEL), const),
        ],
        out_specs=out_specs,
        out_shape=out_shape,
        compiler_params=_params(("parallel",)),
        name="merge",
    )(ohg, orw, omb, proj, proj, proj, h2d, whg, wrw, wmb, wout, fw)


def _sample_call(body, name, bsz, in_specs, args, state_spec, state_shape, scratch, new_state):
    tb = SAMPLE_TB
    aliases = {}
    if new_state is not None:
        in_specs = in_specs + [pl.BlockSpec(memory_space=pl.ANY)]
        args = args + (new_state,)
        aliases = {len(args) - 1: 1}
        inner = body
        body = lambda *refs: inner(*refs[:len(args) - 1], *refs[len(args):])
    return pl.pallas_call(
        body,
        grid=(bsz // tb,),
        in_specs=in_specs,
        out_specs=[pl.BlockSpec((tb, D_MODEL), lambda i: (i, 0)), state_spec],
        out_shape=[jax.ShapeDtypeStruct((bsz, D_MODEL), F32), jax.ShapeDtypeStruct(state_shape, F32)],
        scratch_shapes=scratch,
        input_output_aliases=aliases,
        compiler_params=_params(("parallel",)),
        name=name,
    )(*args)


def _hgrn_prep(q, f, lb):
    sig = _sigmoid(f)
    fg = lb + (1.0 - lb) * sig
    logf = jnp.log(jnp.maximum(fg, HG_F_MIN))
    kx = (1.0 - lb) * (1.0 - sig)
    qx = _silu(q) * (HG_DK ** -0.5)
    return qx, kx, logf


def _hgrn_post(o, g, nw):
    ms = jnp.mean(o * o, axis=-1, keepdims=True)
    return o * lax.rsqrt(ms + RMS_EPS) * nw * _silu(g)


def _hgrn_prompt_kernel(q_ref, f_ref, i_ref, g_ref, lb_ref, nw_ref, tril_ref, o_ref, s_ref,
                        st_ref, qs_ref, ks_ref, bs_ref, ob_ref, *, nblk):
    c = pl.program_id(1)
    nb = q_ref.shape[0]

    @pl.when(c == 0)
    def _():
        st_ref[...] = jnp.zeros_like(st_ref)

    tril = tril_ref[...]
    heads = [(i, h, slice(h * 128, (h + 1) * 128)) for i in range(nb) for h in range(HG_HEADS)]
    for i, h, sl in heads:
        qx, kx, logf = _hgrn_prep(q_ref[i, :, sl], f_ref[i, :, sl], lb_ref[:, sl])
        qs_ref[i, :, sl] = qx
        ks_ref[i, :, sl] = kx
        bs_ref[i, :, sl] = _mm_exact(tril, logf) * LOG2E
    half = HG_BLK // 2
    lane8 = lax.broadcasted_iota(jnp.int32, (half, HG_DK), 1)
    top16 = lax.broadcasted_iota(jnp.int32, (HG_BLK, HG_DK), 0) < half
    lower = (lax.broadcasted_iota(jnp.int32, (HG_BLK, HG_BLK), 1)
             <= lax.broadcasted_iota(jnp.int32, (HG_BLK, HG_BLK), 0))

    def intra_weights(j):
        r0 = j * HG_BLK if isinstance(j, int) else pl.multiple_of(j * HG_BLK, HG_BLK)
        top, bot, blk = pl.ds(r0, half), pl.ds(r0 + half, half), pl.ds(r0, HG_BLK)
        amat = []
        for i, h, sl in heads:
            q_t, q_b, b_t, b_b = qs_ref[i, top, sl], qs_ref[i, bot, sl], bs_ref[i, top, sl], bs_ref[i, bot, sl]
            k16, b16 = ks_ref[i, blk, sl], bs_ref[i, blk, sl]
            a_top = jnp.zeros((half, HG_DK), F32)
            a_bot = jnp.zeros((half, HG_DK), F32)
            for jj in range(half):
                s = jnp.sum(q_t * jnp.exp2(b_t - b16[jj:jj + 1]) * k16[jj:jj + 1], axis=-1, keepdims=True)
                a_top = jnp.where(lane8 == jj, s, a_top)
                jb = half + jj
                s = jnp.sum(q_b * jnp.exp2(b_b - b16[jb:jb + 1]) * k16[jb:jb + 1], axis=-1, keepdims=True)
                a_bot = jnp.where(lane8 == jb, s, a_bot)
            b_mid = b16[half - 1:half]
            k_dec = jnp.where(top16, k16 * jnp.exp2(b_mid - b16), 0.0)
            cross = _mm(q_b * jnp.exp2(b_b - b_mid), k_dec, NT)
            full = jnp.concatenate([a_top[:, :HG_BLK], a_bot[:, :HG_BLK] + cross], axis=0)
            amat.append(jnp.where(lower, full, 0.0))
        return tuple(amat)

    def apply_block(j, amat):
        rows = pl.ds(pl.multiple_of(j * HG_BLK, HG_BLK), HG_BLK)
        idx = range(len(heads))
        qb = [qs_ref[i, rows, sl] for i, h, sl in heads]
        kb = [ks_ref[i, rows, sl] for i, h, sl in heads]
        bb = [bs_ref[i, rows, sl] for i, h, sl in heads]
        vb = [i_ref[i, rows, sl] for i, h, sl in heads]
        st = [st_ref[i, h] for i, h, sl in heads]
        intra = [_mm(amat[n], vb[n]) for n in idx]
        inter = [_mm(qb[n] * jnp.exp2(bb[n]), st[n], NT) for n in idx]
        for n, (i, h, sl) in enumerate(heads):
            b_end = bb[n][HG_BLK - 1:HG_BLK, :]
            st_ref[i, h] = st[n] * jnp.exp2(b_end) + _mm(vb[n], kb[n] * jnp.exp2(b_end - bb[n]), TN)
        for n, (i, h, sl) in enumerate(heads):
            ob_ref[i, rows, sl] = intra[n] + inter[n]

    def step(j, amats):
        apply_block(j - 1, amats)
        return intra_weights(j)

    apply_block(nblk - 1, lax.fori_loop(1, nblk, step, intra_weights(0)))
    nw = nw_ref[...]
    for i, h, sl in heads:
        o_ref[i, :, sl] = _hgrn_post(ob_ref[i, :, sl], g_ref[i, :, sl], nw).astype(o_ref.dtype)

    @pl.when(c == pl.num_programs(1) - 1)
    def _():
        for i, h, sl in heads:
            s_ref[i, h] = st_ref[i, h].T


def _hgrn_prompt(proj, lb, nw, bsz, t, l):
    tt = min(256, t)
    n_t = t // tt
    nb = HG_NB if bsz % HG_NB == 0 else 1
    tril = jnp.asarray(np.kron(np.eye(tt // HG_BLK), np.tril(np.ones((HG_BLK, HG_BLK)))), BF16)
    proj3 = proj.reshape(bsz, t, NCOL)
    col = lambda blk: (lambda b, c: (b, c, blk))
    const = lambda shape: pl.BlockSpec(shape, lambda b, c: (0, 0))
    layer = lambda shape: pl.BlockSpec((None,) + shape, lambda b, c: (l, 0, 0))
    tile = pltpu.VMEM((nb, tt, D_MODEL), F32)
    o, s = pl.pallas_call(
        functools.partial(_hgrn_prompt_kernel, nblk=tt // HG_BLK),
        grid=(bsz // nb, n_t),
        in_specs=[
            pl.BlockSpec((nb, tt, D_MODEL), col(C_Q)), pl.BlockSpec((nb, tt, D_MODEL), col(C_F)),
            pl.BlockSpec((nb, tt, D_MODEL), col(C_I)), pl.BlockSpec((nb, tt, D_MODEL), col(C_GH)),
            layer((1, D_MODEL)), layer((1, 128)), const((tt, tt)),
        ],
        out_specs=[
            pl.BlockSpec((nb, tt, D_MODEL), lambda b, c: (b, c, 0)),
            pl.BlockSpec((nb, HG_HEADS, HG_DK, HG_DV), lambda b, c: (b, 0, 0, 0)),
        ],
        out_shape=[
            jax.ShapeDtypeStruct((bsz, t, D_MODEL), BF16),
            jax.ShapeDtypeStruct((bsz, HG_HEADS, HG_DK, HG_DV), F32),
        ],
        scratch_shapes=[pltpu.VMEM((nb, HG_HEADS, HG_DV, HG_DK), F32), tile, tile, tile, tile],
        compiler_params=_params(("parallel", "arbitrary")),
        name="hgrn_prompt",
    )(proj3, proj3, proj3, proj3, lb, nw, tril)
    return o.reshape(bsz * t, D_MODEL), s


def _hgrn_sample_kernel(q_ref, f_ref, i_ref, g_ref, lb_ref, nw_ref, s_ref, o_ref, so_ref,
                        qs_ref, ks_ref, fs_ref, ob_ref):
    qx, kx, logf = _hgrn_prep(q_ref[...], f_ref[...], lb_ref[...])
    qs_ref[...] = qx
    ks_ref[...] = kx
    fs_ref[...] = jnp.exp(logf)
    ob_ref[...] = jnp.zeros_like(ob_ref)
    rows = lax.broadcasted_iota(jnp.int32, (SAMPLE_TB, 1), 0)

    heads = [slice(h * 128, (h + 1) * 128) for h in range(HG_HEADS)]

    unroll = 4

    def per_seqs(i, carry):
        work = [(i * unroll + j, rows == i * unroll + j, h, sl) for j in range(unroll)
                for h, sl in enumerate(heads)]
        fcol = [_col_bcast(jnp.where(sel, fs_ref[:, sl], 0.0), HG_DV) for b, sel, h, sl in work]
        kv = [_mm(jnp.where(sel, ks_ref[:, sl], 0.0), i_ref[:, sl], TN) for b, sel, h, sl in work]
        s_new = [fcol[n] * s_ref[b, h] + kv[n] for n, (b, sel, h, sl) in enumerate(work)]
        for s, (b, sel, h, sl) in zip(s_new, work):
            so_ref[b, h] = s
        outs = [_mm(jnp.where(sel, qs_ref[:, sl], 0.0), s) for s, (b, sel, h, sl) in zip(s_new, work)]
        for h, sl in enumerate(heads):
            ob_ref[:, sl] += sum(outs[j * HG_HEADS + h] for j in range(unroll))
        return carry

    lax.fori_loop(0, SAMPLE_TB // unroll, per_seqs, 0)
    nw = nw_ref[...]
    for h in range(HG_HEADS):
        sl = slice(h * 128, (h + 1) * 128)
        o_ref[:, sl] = _hgrn_post(ob_ref[:, sl], g_ref[:, sl], nw)


def _hgrn_sample(proj, lb, nw, state, l, new_state):
    bsz = state.shape[1]
    tb = SAMPLE_TB
    col = lambda blk: (lambda i: (i, blk))
    tile = pltpu.VMEM((tb, D_MODEL), F32)
    sspec = pl.BlockSpec((None, tb, HG_HEADS, HG_DK, HG_DV), lambda i: (l, i, 0, 0, 0))
    return _sample_call(
        _hgrn_sample_kernel, "hgrn_sample", bsz,
        [
            pl.BlockSpec((tb, D_MODEL), col(C_Q)), pl.BlockSpec((tb, D_MODEL), col(C_F)),
            pl.BlockSpec((tb, D_MODEL), col(C_I)), pl.BlockSpec((tb, D_MODEL), col(C_GH)),
            pl.BlockSpec((None, 1, D_MODEL), lambda i: (l, 0, 0)),
            pl.BlockSpec((None, 1, 128), lambda i: (l, 0, 0)),
            sspec,
        ],
        (proj, proj, proj, proj, lb, nw, state), sspec, state.shape, [tile, tile, tile, tile], new_state)


def _rw_lora(wdad, w2a2):
    lo = lax.broadcasted_iota(jnp.int32, wdad.shape, 1) < RW_LORA
    lora_w = _mm3(jnp.where(lo, jnp.tanh(wdad), 0.0), w2a2)
    lora_a = _mm3(jnp.where(lo, 0.0, wdad), w2a2)
    return lora_w, lora_a


def _rw_prep_all(items, ones_bd):
    rs, ks, vs, lws, a_sigs, kks = [], [], [], [], [], []
    for r, k0, v, lora_w, lora_a, w0, a0, kk_w, ka_w in items:
        lws.append(-RW_DECAY_SCALE * _sigmoid(w0 + lora_w))
        a_sig = _sigmoid(a0 + lora_a)
        a_sigs.append(a_sig)
        kks.append(k0 * kk_w)
        rs.append(r)
        vs.append(v)
        ks.append(k0 * (1.0 + (a_sig - 1.0) * ka_w))
    sq = _seg_many(ones_bd, [kk * kk for kk in kks])
    kks = [kk / jnp.maximum(jnp.sqrt(s), 1e-12) for kk, s in zip(kks, sq)]
    return rs, ks, vs, lws, [-kk for kk in kks], [kk * a for kk, a in zip(kks, a_sigs)]


def _rw_posts(os_, rs, ks, vs, gs, rk_ws, lnws, lnbs, ones_bd):
    idx = range(len(os_))
    sums = _seg_many(ones_bd, list(os_) + [rs[i] * ks[i] * rk_ws[i] for i in idx])
    ds = [os_[i] - sums[i] * (1.0 / RW_N) for i in idx]
    bonus = sums[len(os_):]
    var = [s * (1.0 / RW_N) for s in _seg_many(ones_bd, [d * d for d in ds])]
    return [(ds[i] * lax.rsqrt(var[i] + RW_LN_EPS) * lnws[i] + lnbs[i] + bonus[i] * vs[i]) * _silu(gs[i])
            for i in idx]


def _rw_chunks(rs, ks, vs, lws, avs, bvs, sbds, tril):
    n = RW_CHUNK
    pairs = range(len(rs))
    lane = lax.broadcasted_iota(jnp.int32, (n, 128), 1)
    row = lax.broadcasted_iota(jnp.int32, (n, 128), 0)
    m0 = lane < RW_N
    col = jnp.where(m0, lane, lane - RW_N)
    strict = col < row
    incl = col <= row
    eye = (col == row).astype(F32)

    def bd(x):
        xb = x.astype(BF16)
        zero = jnp.zeros_like(xb)
        return jnp.concatenate([jnp.where(m0, xb, zero), jnp.where(m0, zero, xb)], axis=0)

    cls = [_mm_exact(tril, lws[q]) for q in pairs]
    a_h = [avs[q] * jnp.exp(cls[q] - lws[q]) for q in pairs]
    r_h = [rs[q] * jnp.exp(cls[q]) for q in pairs]
    b_c = [bvs[q] * jnp.exp(-cls[q]) for q in pairs]
    k_c = [ks[q] * jnp.exp(-cls[q]) for q in pairs]
    lhs = [jnp.concatenate([a_h[q], r_h[q]], axis=0) for q in pairs]
    from_state = [_mm(lhs[q], sbds[q], NT) for q in pairs]
    m_bk = [_mm(lhs[q], jnp.concatenate([bd(b_c[q]), bd(k_c[q])], axis=0), NT) for q in pairs]
    m_ab = [jnp.where(strict, m[:n, :128], 0.0) for m in m_bk]
    m_ak = [jnp.where(strict, m[:n, 128:], 0.0) for m in m_bk]
    m_rb = [jnp.where(incl, m[n:, :128], 0.0) for m in m_bk]
    m_rk = [jnp.where(incl, m[n:, 128:], 0.0) for m in m_bk]
    ps = [_mm(m, bd(m)) for m in m_ab]
    xs = [eye + m for m in m_ab]
    for level in range(1, 5):
        both = [_mm(jnp.concatenate([ps[q], xs[q]], axis=0), bd(ps[q])) for q in pairs]
        ps = [m[:n] for m in both]
        xs = [xs[q] + both[q][n:] for q in pairs]
    xs = [xs[q] + _mm(xs[q], bd(ps[q])) for q in pairs]
    from_v = [_mm(jnp.concatenate([m_ak[q], m_rk[q]], axis=0), bd(vs[q])) for q in pairs]
    w_all = [from_state[q][:n] + from_v[q][:n] for q in pairs]
    u_all = [_mm(xs[q], bd(w_all[q])) for q in pairs]
    o_all = [from_state[q][n:] + from_v[q][n:] + _mm(m_rb[q], bd(u_all[q])) for q in pairs]

    vi = lax.broadcasted_iota(jnp.int32, (128, 128), 0)
    ki = lax.broadcasted_iota(jnp.int32, (128, 128), 1)
    same_head = (vi < RW_N) == (ki < RW_N)
    new = []
    for q in pairs:
        cl_end = cls[q][n - 1:n, :]
        e_end = jnp.exp(cl_end - cls[q])
        upd = _mm(jnp.concatenate([u_all[q], vs[q]], axis=0),
                  jnp.concatenate([bvs[q] * e_end, ks[q] * e_end], axis=0), TN)
        new.append(sbds[q] * jnp.exp(cl_end) + jnp.where(same_head, upd, 0.0))
    return o_all, new


def _rwkv_prompt_kernel(r_ref, k_ref, v_ref, g_ref, wdad_ref, mu_ref, w0_ref, a0_ref, kkw_ref, kaw_ref, rkw_ref,
                        lnw_ref, lnb_ref, w2a2_ref, tril_ref, ones_ref, o_ref, s_ref,
                        sbd_ref, pr_ref, pk_ref, pv_ref, pwd_ref, lw_ref, la_ref):
    c = pl.program_id(1)
    n = RW_CHUNK
    nb = r_ref.shape[0]

    @pl.when(c == 0)
    def _():
        sbd_ref[...] = jnp.zeros_like(sbd_ref)
        pr_ref[...] = jnp.zeros_like(pr_ref)
        pk_ref[...] = jnp.zeros_like(pk_ref)
        pv_ref[...] = jnp.zeros_like(pv_ref)
        pwd_ref[...] = jnp.zeros_like(pwd_ref)

    row = lax.broadcasted_iota(jnp.int32, (n, 128), 0)

    def shift(z_ref, prev_ref, i, sl, mu_lo):
        z = z_ref[i, :, sl]
        zp = jnp.where(row == 0, prev_ref[i, :, sl], pltpu.roll(z, 1, axis=0))
        prev_ref[i, :, sl] = z[n - 1:n, :]
        return z + (zp - z) * mu_ref[:, mu_lo:mu_lo + 128]

    for i in range(nb):
        lora_w, lora_a = _rw_lora(shift(wdad_ref, pwd_ref, i, slice(0, 128), 3 * D_MODEL), w2a2_ref[...])
        lw_ref[i] = lora_w
        la_ref[i] = lora_a
    ones_bd = ones_ref[...]
    lanes = [slice(q * 128, (q + 1) * 128) for q in range(RW_HEADS // 2)]
    pairs = [(i, q, sl) for i in range(nb) for q, sl in enumerate(lanes)]
    rs, ks, vs, lws, avs, bvs = _rw_prep_all(
        [(shift(r_ref, pr_ref, i, sl, q * 128), shift(k_ref, pk_ref, i, sl, D_MODEL + q * 128),
          shift(v_ref, pv_ref, i, sl, 2 * D_MODEL + q * 128), lw_ref[i, :, sl], la_ref[i, :, sl],
          w0_ref[:, sl], a0_ref[:, sl], kkw_ref[:, sl], kaw_ref[:, sl]) for i, q, sl in pairs], ones_bd)
    o_all, sbd_new = _rw_chunks(rs, ks, vs, lws, avs, bvs, [sbd_ref[i, q] for i, q, sl in pairs], tril_ref[...])
    for (i, q, sl), new in zip(pairs, sbd_new):
        sbd_ref[i, q] = new
    outs = _rw_posts(o_all, rs, ks, vs, [g_ref[i, :, sl] for i, q, sl in pairs],
                     [rkw_ref[:, sl] for i, q, sl in pairs], [lnw_ref[:, sl] for i, q, sl in pairs],
                     [lnb_ref[:, sl] for i, q, sl in pairs], ones_bd)
    for (i, q, sl), out in zip(pairs, outs):
        o_ref[i, :, sl] = out.astype(o_ref.dtype)

    @pl.when(c == pl.num_programs(1) - 1)
    def _():
        s_ref[...] = sbd_ref[...]


def _rw_consts():
    tril = jnp.asarray(np.tril(np.ones((RW_CHUNK, RW_CHUNK))), BF16)
    ones_bd = jnp.asarray(np.kron(np.eye(2), np.ones((RW_N, RW_N))), BF16)
    return tril, ones_bd


def _rwkv_prompt(proj, p, bsz, t, l):
    n = RW_CHUNK
    n_t = t // n
    nb = RW_NB if bsz % RW_NB == 0 else 1
    tril, ones_bd = _rw_consts()
    proj3 = proj.reshape(bsz, t, NCOL)
    col = lambda blk: (lambda b, c: (b, c, blk))
    const = lambda shape: pl.BlockSpec(shape, lambda b, c: (0, 0))
    layer = lambda shape: pl.BlockSpec((None,) + shape, lambda b, c: (l, 0, 0))
    vec = layer((1, D_MODEL))
    prev = pltpu.VMEM((nb, 1, D_MODEL), F32)
    tile = pltpu.VMEM((nb, n, D_MODEL), F32)
    o, sbd = pl.pallas_call(
        _rwkv_prompt_kernel,
        grid=(bsz // nb, n_t),
        in_specs=[
            pl.BlockSpec((nb, n, D_MODEL), col(C_R)), pl.BlockSpec((nb, n, D_MODEL), col(C_K)),
            pl.BlockSpec((nb, n, D_MODEL), col(C_V)), pl.BlockSpec((nb, n, D_MODEL), col(C_GR)),
            pl.BlockSpec((nb, n, 128), lambda b, c: (b, c, C_WDAD)),
            layer((1, RW_SHIFT)),
            vec, vec, vec, vec, vec, vec, vec,
            layer((128, D_MODEL)), const((n, n)), const((128, 128)),
        ],
        out_specs=[
            pl.BlockSpec((nb, n, D_MODEL), lambda b, c: (b, c, 0)),
            pl.BlockSpec((nb, RW_HEADS // 2, 128, 128), lambda b, c: (b, 0, 0, 0)),
        ],
        out_shape=[
            jax.ShapeDtypeStruct((bsz, t, D_MODEL), BF16),
            jax.ShapeDtypeStruct((bsz, RW_HEADS // 2, 128, 128), F32),
        ],
        scratch_shapes=[pltpu.VMEM((nb, RW_HEADS // 2, 128, 128), F32), prev, prev, prev,
                        pltpu.VMEM((nb, 1, 128), F32), tile, tile],
        compiler_params=_params(("parallel", "arbitrary")),
        name="rwkv_prompt",
    )(proj3, proj3, proj3, proj3, proj3, p["mu"], p["w0"], p["a0"], p["kk"], p["ka"], p["rk"], p["lnw"], p["lnb"],
      p["w2a2"], tril, ones_bd)
    s6 = sbd.reshape(bsz, RW_HEADS // 2, 2, RW_N, 2, RW_N)
    state = jnp.stack([s6[:, :, 0, :, 0, :], s6[:, :, 1, :, 1, :]], axis=2)
    return o.reshape(bsz * t, D_MODEL), state.reshape(bsz, RW_HEADS, RW_N, RW_N)


def _rwkv_sample_prep_kernel(r_ref, k_ref, v_ref, wdad_ref, sh_ref, mu_ref, w0_ref, a0_ref, kkw_ref, kaw_ref,
                             w2a2_ref, ones_ref, r_o, k_o, v_o, w_o, a_o, b_o):
    def shift(z, lo, width):
        return z + (sh_ref[:, lo:lo + width] - z) * mu_ref[:, lo:lo + width]

    lora_w, lora_a = _rw_lora(shift(wdad_ref[...], 3 * D_MODEL, 128), w2a2_ref[...])
    pairs = [slice(q * 128, (q + 1) * 128) for q in range(RW_HEADS // 2)]
    rs, ks, vs, lws, avs, bvs = _rw_prep_all(
        [(shift(r_ref[:, sl], q * 128, 128), shift(k_ref[:, sl], D_MODEL + q * 128, 128),
          shift(v_ref[:, sl], 2 * D_MODEL + q * 128, 128), lora_w[:, sl], lora_a[:, sl], w0_ref[:, sl],
          a0_ref[:, sl], kkw_ref[:, sl], kaw_ref[:, sl]) for q, sl in enumerate(pairs)], ones_ref[...])
    for q, sl in enumerate(pairs):
        r_o[:, sl] = rs[q]
        k_o[:, sl] = ks[q]
        v_o[:, sl] = vs[q]
        w_o[:, sl] = jnp.exp(lws[q])
        a_o[:, sl] = avs[q]
        b_o[:, sl] = bvs[q]


def _rwkv_sample_state_kernel(s_ref, w_ref, a_ref, b_ref, k_ref, v_ref, r_ref, o_ref, so_ref):
    row8 = lax.broadcasted_iota(jnp.int32, (8, s_ref.shape[-1]), 0)
    for hh in range(s_ref.shape[0]):
        keys = slice(hh * RW_N, (hh + 1) * RW_N)
        w, a, bk, kk, r = w_ref[keys, :], a_ref[keys, :], b_ref[keys, :], k_ref[keys, :], r_ref[keys, :]

        def eight_values(v8, carry):
            base = pl.multiple_of(v8 * 8, 8)
            vals = v_ref[pl.ds(hh * RW_N + base, 8), :]
            outs = jnp.zeros_like(vals)
            for j in range(8):
                s_old = s_ref[hh, base + j]
                sa = jnp.sum(s_old * a, axis=0, keepdims=True)
                s_new = s_old * w + sa * bk + vals[j:j + 1, :] * kk
                so_ref[hh, base + j] = s_new
                outs = jnp.where(row8 == j, jnp.sum(s_new * r, axis=0, keepdims=True), outs)
            o_ref[pl.ds(hh * RW_N + base, 8), :] = outs
            return carry

        lax.fori_loop(0, RW_N // 8, eight_values, 0)


def _rwkv_sample_post_kernel(o_ref, r_ref, k_ref, v_ref, g_ref, rkw_ref, lnw_ref, lnb_ref, ones_ref, out_ref):
    pairs = [slice(q * 128, (q + 1) * 128) for q in range(RW_HEADS // 2)]
    ref = lambda x: [x[:, sl] for sl in pairs]
    outs = _rw_posts(ref(o_ref), ref(r_ref), ref(k_ref), ref(v_ref), ref(g_ref), ref(rkw_ref), ref(lnw_ref),
                     ref(lnb_ref), ones_ref[...])
    for sl, out in zip(pairs, outs):
        out_ref[:, sl] = out


def _rwkv_sample(proj, p, shift_state, state_t, l, new_state_t):
    bsz = proj.shape[0]
    _, ones_bd = _rw_consts()
    whole = lambda shape: pl.BlockSpec(shape, lambda i: tuple(0 for _ in shape))
    layer = lambda shape: pl.BlockSpec((None,) + shape, lambda i: (l,) + tuple(0 for _ in shape))
    col = lambda blk, w=D_MODEL: pl.BlockSpec((bsz, w), lambda i: (0, blk))
    vec = layer((1, D_MODEL))
    act = jax.ShapeDtypeStruct((bsz, D_MODEL), F32)
    r, k, v, w, a, b = pl.pallas_call(
        _rwkv_sample_prep_kernel,
        grid=(1,),
        in_specs=[col(C_R), col(C_K), col(C_V), col(C_WDAD, 128), layer((bsz, RW_SHIFT)), layer((1, RW_SHIFT)),
                  vec, vec, vec, vec, layer((128, D_MODEL)), whole((128, 128))],
        out_specs=[whole((bsz, D_MODEL))] * 6,
        out_shape=[act] * 6,
        compiler_params=_params(("arbitrary",)),
        name="rwkv_sample_prep",
    )(proj, proj, proj, proj, shift_state, p["mu"], p["w0"], p["a0"], p["kk"], p["ka"], p["w2a2"], ones_bd)

    hb = 2
    rows = pl.BlockSpec((hb * RW_N, bsz), lambda i: (i, 0))
    sspec = pl.BlockSpec((None, hb, RW_N, RW_N, bsz), lambda i: (l, i, 0, 0, 0))
    args = (state_t, w.T, a.T, b.T, k.T, v.T, r.T)
    in_specs = [sspec] + [rows] * 6
    aliases = {}
    body = _rwkv_sample_state_kernel
    if new_state_t is not None:
        in_specs = in_specs + [pl.BlockSpec(memory_space=pl.ANY)]
        args = args + (new_state_t,)
        aliases = {len(args) - 1: 1}
        body = lambda *refs: _rwkv_sample_state_kernel(*refs[:7], *refs[8:])
    o_t, new_state_t = pl.pallas_call(
        body,
        grid=(RW_HEADS // hb,),
        in_specs=in_specs,
        out_specs=[rows, sspec],
        out_shape=[jax.ShapeDtypeStruct((D_MODEL, bsz), F32), jax.ShapeDtypeStruct(state_t.shape, F32)],
        input_output_aliases=aliases,
        compiler_params=_params(("parallel",)),
        name="rwkv_sample_state",
    )(*args)

    o = pl.pallas_call(
        _rwkv_sample_post_kernel,
        grid=(1,),
        in_specs=[whole((bsz, D_MODEL))] * 4 + [col(C_GR), vec, vec, vec, whole((128, 128))],
        out_specs=whole((bsz, D_MODEL)),
        out_shape=act,
        compiler_params=_params(("arbitrary",)),
        name="rwkv_sample_post",
    )(o_t.T, r, k, v, proj, p["rk"], p["lnw"], p["lnb"], ones_bd)
    return o, new_state_t


def _mb_post(y, xs, z, d_rep, nw):
    y = (y + d_rep * xs) * _silu(z)
    ms = jnp.mean(y * y, axis=-1, keepdims=True)
    return y * lax.rsqrt(ms + RMS_EPS) * nw


def _ssd_prompt_kernel(x_ref, bc_ref, z_ref, dt_ref, cw_ref, cb_ref, dtb_ref, alog_ref, d_ref, nw_ref, tril_ref,
                       exp_ref, o_ref, s_ref, h_ref, px_ref, pbc_ref):
    c = pl.program_id(1)
    n = MB_CHUNK
    nb = x_ref.shape[0]

    @pl.when(c == 0)
    def _():
        h_ref[...] = jnp.zeros_like(h_ref)
        px_ref[...] = jnp.zeros_like(px_ref)
        pbc_ref[...] = jnp.zeros_like(pbc_ref)

    def conv(u_ref, prev_ref, i, sl, wsl):
        u = u_ref[i, :, sl]
        prev = prev_ref[i, :, sl]
        w = cw_ref[:, wsl]
        row = lax.broadcasted_iota(jnp.int32, (8, u.shape[1]), 0)
        acc = u * w[MB_CONV - 1:MB_CONV, :] + cb_ref[:, wsl]
        for s in range(1, MB_CONV):
            us = pltpu.roll(u, s, axis=0)
            top = jnp.where(row < s, pltpu.roll(prev, s, axis=0), us[0:8])
            us = jnp.concatenate([top, us[8:]], axis=0)
            acc = acc + us * w[MB_CONV - 1 - s:MB_CONV - s, :]
        prev_ref[i, :, sl] = u[n - 8:n, :]
        return _silu(acc)

    tril = tril_ref[...]
    ti = lax.broadcasted_iota(jnp.int32, (n, n), 0)
    si = lax.broadcasted_iota(jnp.int32, (n, n), 1)
    incl = si <= ti
    triu = (ti <= si).astype(BF16)
    lane = lax.broadcasted_iota(jnp.int32, (n, 4 * MB_P), 1)
    nbc = MB_GROUPS * MB_N
    seqs = range(nb)
    groups = [(i, g, slice(g * 256, (g + 1) * 256)) for i in seqs for g in range(MB_GROUPS)]
    idx = range(len(groups))

    spread = exp_ref[...]
    dt_c = [_softplus(dt_ref[i] + dtb_ref[...]) for i in seqs]
    a_c = [dt_c[i] * (-jnp.exp(alog_ref[...])) for i in seqs]
    cs_c = [_mm_exact(tril, a_c[i]) for i in seqs]
    cs_rows = [_mm_exact(triu, a_c[i], TN, x_is_lhs=True) for i in seqs]
    dt_all = [_mm_exact(spread, dt_c[i], x_is_lhs=True) for i in seqs]
    cs_all = [_mm_exact(spread, cs_c[i], x_is_lhs=True) for i in seqs]

    xs = [conv(x_ref, px_ref, i, gx, gx) for i, g, gx in groups]
    bm = [conv(bc_ref, pbc_ref, i, slice(g * MB_N, (g + 1) * MB_N),
               slice(D_MODEL + g * MB_N, D_MODEL + (g + 1) * MB_N)) for i, g, gx in groups]
    cm = [conv(bc_ref, pbc_ref, i, slice(nbc + g * MB_N, nbc + (g + 1) * MB_N),
               slice(D_MODEL + nbc + g * MB_N, D_MODEL + nbc + (g + 1) * MB_N)) for i, g, gx in groups]
    xdt = [xs[k] * dt_all[i][:, gx] for k, (i, g, gx) in enumerate(groups)]
    cs = [cs_all[i][:, gx] for i, g, gx in groups]
    gmat = [_mm(cm[k], bm[k], NT) for k in idx]
    h_all = [h_ref[i, g] for i, g, gx in groups]
    y = [jnp.exp(cs[k]) * _mm(cm[k], h_all[k], NT) for k in idx]
    for k, (i, g, gx) in enumerate(groups):
        for e in range(4):
            hd = slice(e * MB_P, (e + 1) * MB_P)
            hrow = slice(4 * g + e, 4 * g + e + 1)
            lmat = jnp.where(incl, jnp.exp(cs[k][:, hd] - cs_rows[i][hrow, :]), 0.0)
            me = (lane >= e * MB_P) & (lane < (e + 1) * MB_P)
            y[k] = y[k] + _mm(gmat[k] * lmat, jnp.where(me, xdt[k], 0.0))
    for k, (i, g, gx) in enumerate(groups):
        cs_end = cs[k][n - 1:n, :]
        upd = _mm(xdt[k] * jnp.exp(cs_end - cs[k]), bm[k], TN)
        decay = [jnp.exp(cs_rows[i][4 * g + e:4 * g + e + 1, n - 1:n]) for e in range(4)]
        h_ref[i, g] = jnp.concatenate([h_all[k][e * MB_P:(e + 1) * MB_P] * decay[e] for e in range(4)],
                                      axis=0) + upd
    for k, (i, g, gx) in enumerate(groups):
        o_ref[i, :, gx] = _mb_post(y[k], xs[k], z_ref[i, :, gx], d_ref[:, gx], nw_ref[:, gx]).astype(o_ref.dtype)

    @pl.when(c == pl.num_programs(1) - 1)
    def _():
        for i, g, gx in groups:
            for e in range(4):
                s_ref[i, 4 * g + e] = h_ref[i, g, e * MB_P:(e + 1) * MB_P, :]


def _head_spread():
    m = np.zeros((128, D_MODEL), np.float32)
    for h in range(MB_HEADS):
        m[h, h * MB_P:(h + 1) * MB_P] = 1.0
    return jnp.asarray(m, BF16)


def _ssd_prompt(proj, p, bsz, t, l):
    n = MB_CHUNK
    n_t = t // n
    nb = MB_NB if bsz % MB_NB == 0 else 1
    tril = jnp.asarray(np.tril(np.ones((n, n))), BF16)
    proj3 = proj.reshape(bsz, t, NCOL)
    col = lambda blk: (lambda b, c: (b, c, blk))
    const = lambda shape: pl.BlockSpec(shape, lambda b, c: (0, 0))
    layer = lambda shape: pl.BlockSpec((None,) + shape, lambda b, c: (l, 0, 0))
    vec = layer((1, D_MODEL))
    o, s = pl.pallas_call(
        _ssd_prompt_kernel,
        grid=(bsz // nb, n_t),
        in_specs=[
            pl.BlockSpec((nb, n, D_MODEL), col(C_X)), pl.BlockSpec((nb, n, D_MODEL), col(C_BC)),
            pl.BlockSpec((nb, n, D_MODEL), col(C_Z)), pl.BlockSpec((nb, n, 128), col(C_DT)),
            layer((MB_CONV, MB_XBC)), layer((1, MB_XBC)),
            layer((1, 128)), layer((1, 128)), vec, vec,
            const((n, n)), const((128, D_MODEL)),
        ],
        out_specs=[
            pl.BlockSpec((nb, n, D_MODEL), lambda b, c: (b, c, 0)),
            pl.BlockSpec((nb, MB_HEADS, MB_P, MB_N), lambda b, c: (b, 0, 0, 0)),
        ],
        out_shape=[
            jax.ShapeDtypeStruct((bsz, t, D_MODEL), BF16),
            jax.ShapeDtypeStruct((bsz, MB_HEADS, MB_P, MB_N), F32),
        ],
        scratch_shapes=[pltpu.VMEM((nb, MB_GROUPS, 4 * MB_P, MB_N), F32), pltpu.VMEM((nb, 8, D_MODEL), F32),
                        pltpu.VMEM((nb, 8, D_MODEL), F32)],
        compiler_params=_params(("parallel", "arbitrary")),
        name="ssd_prompt",
    )(proj3, proj3, proj3, proj3, p["cw"], p["cb"], p["dtb"], p["alog"], p["d"], p["nw"], tril, _head_spread())
    return o.reshape(bsz * t, D_MODEL), s


def _ssd_sample_kernel(xbc_ref, z_ref, dt_ref, cs_ref, cw_ref, cb_ref, dtb_ref, alog_ref, d_ref, nw_ref, exp_ref,
                       s_ref, o_ref, so_ref, xs_ref, xd_ref, bs_ref, cms_ref, dec_ref, ob_ref):
    tb = SAMPLE_TB
    w = cw_ref[...]
    acc = xbc_ref[...] * w[MB_CONV - 1:MB_CONV, :] + cb_ref[...]
    for i in range(MB_CONV - 1):
        acc = acc + cs_ref[i] * w[i:i + 1, :]
    xbc = _silu(acc)
    xs = xbc[:, :D_MODEL]
    dt = _softplus(dt_ref[...] + dtb_ref[...])
    xs_ref[...] = xs
    xd_ref[...] = xs * _mm_exact(exp_ref[...], dt, x_is_lhs=True)
    bs_ref[...] = xbc[:, D_MODEL:D_MODEL + MB_GROUPS * MB_N]
    cms_ref[...] = xbc[:, D_MODEL + MB_GROUPS * MB_N:]
    dec = jnp.exp(dt * (-jnp.exp(alog_ref[...])))
    for i in range(tb):
        dec_ref[i] = dec[i:i + 1, :]
    ob_ref[...] = jnp.zeros_like(ob_ref)
    rows = lax.broadcasted_iota(jnp.int32, (tb, 1), 0)

    hs = range(MB_HEADS)
    lanes = [slice(h * MB_P, (h + 1) * MB_P) for h in hs]
    grp = [slice((h // 4) * MB_N, (h // 4 + 1) * MB_N) for h in hs]
    unroll = 4

    def per_seqs(i, carry):
        work = [(i * unroll + j, rows == i * unroll + j, h) for j in range(unroll) for h in hs]

        def dec_of(b, h):
            return dec_ref[b][:, h:h + 1]

        upd = [_mm(jnp.where(sel, xd_ref[:, lanes[h]], 0.0), bs_ref[:, grp[h]], TN) for b, sel, h in work]
        h_new = [dec_of(b, h) * s_ref[b, h] + upd[n] for n, (b, sel, h) in enumerate(work)]
        for s, (b, sel, h) in zip(h_new, work):
            so_ref[b, h] = s
        outs = [_mm(jnp.where(sel, cms_ref[:, grp[h]], 0.0), s, NT) for s, (b, sel, h) in zip(h_new, work)]
        for h in hs:
            ob_ref[:, lanes[h]] += sum(outs[j * MB_HEADS + h] for j in range(unroll))
        return carry

    lax.fori_loop(0, tb // unroll, per_seqs, 0)
    for g in range(MB_GROUPS):
        sl = slice(g * 256, (g + 1) * 256)
        o_ref[:, sl] = _mb_post(ob_ref[:, sl], xs_ref[:, sl], z_ref[:, sl], d_ref[:, sl], nw_ref[:, sl])


def _ssd_sample(proj, p, conv_state, state, l, new_state):
    bsz = state.shape[1]
    tb = SAMPLE_TB
    layer = lambda shape: pl.BlockSpec((None,) + shape, lambda i: (l, 0, 0))
    vec = layer((1, D_MODEL))
    tile = pltpu.VMEM((tb, D_MODEL), F32)
    half = pltpu.VMEM((tb, MB_GROUPS * MB_N), F32)
    sspec = pl.BlockSpec((None, tb, MB_HEADS, MB_P, MB_N), lambda i: (l, i, 0, 0, 0))
    return _sample_call(
        _ssd_sample_kernel, "ssd_sample", bsz,
        [
            pl.BlockSpec((tb, MB_XBC), lambda i: (i, C_X // 2)),
            pl.BlockSpec((tb, D_MODEL), lambda i: (i, C_Z)),
            pl.BlockSpec((tb, 128), lambda i: (i, C_DT)),
            pl.BlockSpec((None, MB_CONV - 1, tb, MB_XBC), lambda i: (l, 0, i, 0)),
            layer((MB_CONV, MB_XBC)), layer((1, MB_XBC)),
            layer((1, 128)), layer((1, 128)), vec, vec,
            pl.BlockSpec((128, D_MODEL), lambda i: (0, 0)),
            sspec,
        ],
        (proj, proj, proj, conv_state, p["cw"], p["cb"], p["dtb"], p["alog"], p["d"], p["nw"], _head_spread(),
         state),
        sspec, state.shape, [tile, tile, half, half, pltpu.VMEM((tb, 1, 128), F32), tile], new_state)


def _prep_w_in(w_in):
    o = np.cumsum([0, 1024, 1024, 1024, 1024, RW_SHIFT, 1024, 1024, MB_XBC, MB_HEADS, 3 * D_MODEL])
    zrw, grw, zmb, xbc, dtc, gate = o[4], o[5], o[6], o[7], o[8], o[9]
    parts = [(0, zrw), (zrw, zrw + 3 * D_MODEL), (grw, zmb), (xbc, dtc), (zmb, xbc), (gate, o[10]),
             (zrw + 3 * D_MODEL, grw), (dtc, gate)]
    out = jnp.zeros(w_in.shape[:-1] + (NCOL,), BF16)
    at = 0
    for lo, hi in parts:
        out = lax.dynamic_update_slice_in_dim(out, w_in[..., lo:hi].astype(BF16), at, axis=-1)
        at += hi - lo
    return out


def _rep(v):
    return jnp.repeat(v, MB_P, axis=-1)[:, None, :]


def _pad_heads(v):
    return jnp.pad(v, ((0, 0), (0, 128 - MB_HEADS)))[:, None, :]


def kernel(x_prompt, x_sample, state_hgrn, state_rwkv, state_rwkv_shift, state_ssm, state_conv, norm_w, w_in, hg_lb, hg_norm_w, rw_mu, rw_w0, rw_w2, rw_a0, rw_a2, rw_k_k, rw_k_a, rw_r_k, rw_ln_w, rw_ln_b, mb_conv_w, mb_conv_b, mb_dt_bias, mb_A_log, mb_D, mb_norm_w, w_o_hg, w_o_rw, w_o_mb, w_out, final_norm_w):
    bp, t, _ = x_prompt.shape
    bs = x_sample.shape[0]
    sm = jax.nn.softmax(hg_lb.astype(F32), axis=0)
    row = lambda a: a[:, None, :]
    lbs = row(jnp.cumsum(sm, axis=0) - sm[0:1])
    w_in_r = _prep_w_in(w_in)
    rw_p = dict(mu=row(rw_mu), w0=row(rw_w0), a0=row(rw_a0), kk=row(rw_k_k), ka=row(rw_k_a),
                rk=rw_r_k.reshape(DEPTH, 1, D_MODEL), lnw=row(rw_ln_w), lnb=row(rw_ln_b),
                w2a2=jnp.concatenate([rw_w2, rw_a2], axis=1))
    mb_p = dict(cw=mb_conv_w, cb=row(mb_conv_b), dtb=_pad_heads(mb_dt_bias), alog=_pad_heads(mb_A_log), d=_rep(mb_D),
                nw=row(mb_norm_w))
    nw_in, nw_hg = row(norm_w), row(hg_norm_w)
    whg, wrw, wmb, wout = (w.astype(BF16) for w in (w_o_hg, w_o_rw, w_o_mb, w_out))
    fw = final_norm_w[None, :]
    conv_t = jnp.swapaxes(state_conv, 1, 2)
    rwkv_t = jnp.transpose(state_rwkv, (0, 2, 3, 4, 1))
    lora = slice(C_WDAD * 128, (C_WDAD + 1) * 128)

    hp = x_prompt.reshape(bp * t, D_MODEL)
    hs = x_sample.reshape(bs, D_MODEL)
    p_states, s_small = [], []
    s_hg = s_rw = s_ssm = None
    yp = ys = None
    for l in range(DEPTH):
        final = l == DEPTH - 1

        proj = _inproj(hp, nw_in, w_in_r, l)
        o_hg, p_hg = _hgrn_prompt(proj, lbs, nw_hg, bp, t, l)
        o_rw, p_rw = _rwkv_prompt(proj, rw_p, bp, t, l)
        o_mb, p_ssm = _ssd_prompt(proj, mb_p, bp, t, l)
        proj3 = proj.reshape(bp, t, NCOL)
        p_shift = jnp.concatenate([proj3[:, -1, C_R * 1024:C_GR * 1024], proj3[:, -1, lora]], axis=-1)
        p_conv = proj3[:, t - (MB_CONV - 1):, C_X * 1024:C_Z * 1024]
        res = _merge(o_hg, o_rw, o_mb, proj, hp, whg, wrw, wmb, wout, fw, final, l)
        hp = res[0]
        if final:
            yp = res[1]
        p_states.append((p_hg, p_rw, p_shift, p_ssm, p_conv))

        proj = _inproj(hs, nw_in, w_in_r, l)
        o_hg, s_hg = _hgrn_sample(proj, lbs, nw_hg, state_hgrn, l, s_hg)
        o_rw, s_rw = _rwkv_sample(proj, rw_p, state_rwkv_shift, rwkv_t, l, s_rw)
        o_mb, s_ssm = _ssd_sample(proj, mb_p, conv_t, state_ssm, l, s_ssm)
        s_shift = jnp.concatenate([proj[:, C_R * 1024:C_GR * 1024], proj[:, lora]], axis=-1)
        s_conv = jnp.concatenate([state_conv[l][:, 1:], proj[:, None, C_X * 1024:C_Z * 1024]], axis=1)
        res = _merge(o_hg, o_rw, o_mb, proj, hs, whg, wrw, wmb, wout, fw, final, l)
        hs = res[0]
        if final:
            ys = res[1]
        s_small.append((s_shift, s_conv))

    stack = lambda states, i: jnp.stack([s[i] for s in states])
    return (yp.reshape(bp, t, D_MODEL), ys.reshape(bs, 1, D_MODEL),
            *[stack(p_states, i) for i in range(5)],
            s_hg, jnp.transpose(s_rw, (0, 4, 1, 2, 3)), stack(s_small, 0), s_ssm, stack(s_small, 1))
```

```python
import functools

import numpy as np
import jax
import jax.numpy as jnp
from jax import lax
from jax.experimental import pallas as pl
from jax.experimental.pallas import tpu as pltpu

F32 = jnp.float32
BF16 = jnp.bfloat16

D_MODEL = 1024
DEPTH = 2
HG_HEADS, HG_DK, HG_DV = 8, 128, 128
HG_F_MIN = 1e-20
LOG2E = 1.4426950408889634
RW_HEADS, RW_N = 16, 64
RW_LORA = 64
RW_SHIFT = 3 * D_MODEL + 2 * RW_LORA
RW_LN_EPS = 64e-5
RW_DECAY_SCALE = float(np.exp(-0.5))
MB_HEADS, MB_P, MB_GROUPS, MB_N, MB_CONV = 16, 64, 4, 128, 4
MB_XBC = D_MODEL + 2 * MB_GROUPS * MB_N
RMS_EPS = 1e-6

C_Q, C_F, C_I, C_GH, C_R, C_K, C_V, C_GR, C_X, C_BC, C_Z, C_G0, C_G1, C_G2 = range(14)
C_WDAD = 14 * 8
C_DT = 14 * 8 + 1
NCOL = 14 * 1024 + 2 * 128

HG_BLK = 16
HG_NB = 2
RW_CHUNK = 64
RW_NB = 4
MB_CHUNK = 64
MB_NB = 4
SAMPLE_TB = 8

VMEM_LIMIT = 48 * 1024 * 1024

NN = (((1,), (0,)), ((), ()))
NT = (((1,), (1,)), ((), ()))
TN = (((0,), (0,)), ((), ()))


def _mm(a, b, dims=NN):
    return lax.dot_general(a.astype(BF16), b.astype(BF16), dims, preferred_element_type=F32)


def _mm_exact(e, x, dims=NN, x_is_lhs=False, passes=3):
    e = e.astype(BF16)
    acc = None
    for _ in range(passes):
        p = x.astype(BF16)
        x = x - p.astype(F32)
        t = (lax.dot_general(p, e, dims, preferred_element_type=F32) if x_is_lhs
             else lax.dot_general(e, p, dims, preferred_element_type=F32))
        acc = t if acc is None else acc + t
    return acc


def _mm3(a, b, dims=NN):
    a_hi = a.astype(BF16)
    a_lo = (a - a_hi.astype(F32)).astype(BF16)
    b_hi = b.astype(BF16)
    b_lo = (b - b_hi.astype(F32)).astype(BF16)
    dot = lambda x, y: lax.dot_general(x, y, dims, preferred_element_type=F32)
    return dot(a_hi, b_hi) + (dot(a_hi, b_lo) + dot(a_lo, b_hi))


def _seg_many(ones_bd, xs):
    m = xs[0].shape[0]
    out = _mm_exact(ones_bd, jnp.concatenate(xs, axis=0) if len(xs) > 1 else xs[0], x_is_lhs=True, passes=2)
    return [out[i * m:(i + 1) * m] for i in range(len(xs))]


def _sigmoid(x):
    return jax.nn.sigmoid(x)


def _silu(x):
    return x * jax.nn.sigmoid(x)


def _softplus(x):
    return jnp.maximum(x, 0.0) + jnp.log1p(jnp.exp(-jnp.abs(x)))


def _col_bcast(rows, width):
    ones = jnp.ones((rows.shape[0], width), BF16)
    p1 = rows.astype(BF16)
    r1 = rows - p1.astype(F32)
    p2 = r1.astype(BF16)
    p3 = (r1 - p2.astype(F32)).astype(BF16)
    out = lax.dot_general(p1, ones, TN, preferred_element_type=F32)
    out = out + lax.dot_general(p2, ones, TN, preferred_element_type=F32)
    return out + lax.dot_general(p3, ones, TN, preferred_element_type=F32)


def _params(sem):
    return pltpu.CompilerParams(dimension_semantics=sem, vmem_limit_bytes=VMEM_LIMIT)


def _inproj_kernel(x_ref, nw_ref, w_ref, o_ref, xn_ref):
    @pl.when(pl.program_id(1) == 0)
    def _():
        x = x_ref[...]
        ms = jnp.mean(x * x, axis=-1, keepdims=True)
        xn_ref[...] = (x * lax.rsqrt(ms + RMS_EPS) * nw_ref[...]).astype(BF16)

    o_ref[...] = jnp.dot(xn_ref[...], w_ref[...], preferred_element_type=F32)


def _inproj(x2d, nw, w, l):
    m = x2d.shape[0]
    tm = min(2048, m)
    tn = 768
    return pl.pallas_call(
        _inproj_kernel,
        grid=(m // tm, NCOL // tn),
        in_specs=[
            pl.BlockSpec((tm, D_MODEL), lambda i, j: (i, 0)),
            pl.BlockSpec((None, 1, D_MODEL), lambda i, j: (l, 0, 0)),
            pl.BlockSpec((None, D_MODEL, tn), lambda i, j: (l, 0, j)),
        ],
        out_specs=pl.BlockSpec((tm, tn), lambda i, j: (i, j)),
        out_shape=jax.ShapeDtypeStruct((m, NCOL), F32),
        scratch_shapes=[pltpu.VMEM((tm, D_MODEL), BF16)],
        compiler_params=_params(("parallel", "arbitrary")),
        name="inproj",
    )(x2d, nw, w)


def _merge_kernel(ohg, orw, omb, g0, g1, g2, h_ref, whg, wrw, wmb, wout, fw_ref, hn_ref, *y_ref):
    u = _sigmoid(g0[...]) * _mm(ohg[...], whg[...])
    u = u + _sigmoid(g1[...]) * _mm(orw[...], wrw[...])
    u = u + _sigmoid(g2[...]) * _mm(omb[...], wmb[...])
    hn = h_ref[...] + _mm(u, wout[...])
    hn_ref[...] = hn
    if y_ref:
        ms = jnp.mean(hn * hn, axis=-1, keepdims=True)
        y_ref[0][...] = hn * lax.rsqrt(ms + RMS_EPS) * fw_ref[...]


def _merge(ohg, orw, omb, proj, h2d, whg, wrw, wmb, wout, fw, final, l):
    m = h2d.shape[0]
    tm = min(256, m)
    row = lambda i: (i, 0)
    const = lambda i: (0, 0)
    wspec = pl.BlockSpec((None, D_MODEL, D_MODEL), lambda i: (l, 0, 0))
    out_shape = [jax.ShapeDtypeStruct((m, D_MODEL), F32)]
    out_specs = [pl.BlockSpec((tm, D_MODEL), row)]
    if final:
        out_shape.append(jax.ShapeDtypeStruct((m, D_MODEL), F32))
        out_specs.append(pl.BlockSpec((tm, D_MODEL), row))
    return pl.pallas_call(
        _merge_kernel,
        grid=(m // tm,),
        in_specs=[
            pl.BlockSpec((tm, D_MODEL), row), pl.BlockSpec((tm, D_MODEL), row), pl.BlockSpec((tm, D_MODEL), row),
            pl.BlockSpec((tm, D_MODEL), lambda i: (i, C_G0)),
            pl.BlockSpec((tm, D_MODEL), lambda i: (i, C_G1)),
            pl.BlockSpec((tm, D_MODEL), lambda i: (i, C_G2)),
            pl.BlockSpec((tm, D_MODEL), row),
            wspec, wspec, wspec, wspec,
            pl.BlockSpec((1, D_MODEL), const),
        ],
        out_specs=out_specs,
        out_shape=out_shape,
        compiler_params=_params(("parallel",)),
        name="merge",
    )(ohg, orw, omb, proj, proj, proj, h2d, whg, wrw, wmb, wout, fw)


def _sample_call(body, name, bsz, in_specs, args, state_spec, state_shape, scratch, new_state):
    tb = SAMPLE_TB
    aliases = {}
    if new_state is not None:
        in_specs = in_specs + [pl.BlockSpec(memory_space=pl.ANY)]
        args = args + (new_state,)
        aliases = {len(args) - 1: 1}
        inner = body
        body = lambda *refs: inner(*refs[:len(args) - 1], *refs[len(args):])
    return pl.pallas_call(
        body,
        grid=(bsz // tb,),
        in_specs=in_specs,
        out_specs=[pl.BlockSpec((tb, D_MODEL), lambda i: (i, 0)), state_spec],
        out_shape=[jax.ShapeDtypeStruct((bsz, D_MODEL), F32), jax.ShapeDtypeStruct(state_shape, F32)],
        scratch_shapes=scratch,
        input_output_aliases=aliases,
        compiler_params=_params(("parallel",)),
        name=name,
    )(*args)


def _hgrn_prep(q, f, lb):
    sig = _sigmoid(f)
    fg = lb + (1.0 - lb) * sig
    logf = jnp.log(jnp.maximum(fg, HG_F_MIN))
    kx = (1.0 - lb) * (1.0 - sig)
    qx = _silu(q) * (HG_DK ** -0.5)
    return qx, kx, logf


def _hgrn_post(o, g, nw):
    ms = jnp.mean(o * o, axis=-1, keepdims=True)
    return o * lax.rsqrt(ms + RMS_EPS) * nw * _silu(g)


def _hgrn_prompt_kernel(q_ref, f_ref, i_ref, g_ref, lb_ref, nw_ref, tril_ref, o_ref, s_ref,
                        st_ref, qs_ref, ks_ref, bs_ref, ob_ref, *, nblk):
    c = pl.program_id(1)
    nb = q_ref.shape[0]

    @pl.when(c == 0)
    def _():
        st_ref[...] = jnp.zeros_like(st_ref)

    tril = tril_ref[...]
    heads = [(i, h, slice(h * 128, (h + 1) * 128)) for i in range(nb) for h in range(HG_HEADS)]
    for i, h, sl in heads:
        qx, kx, logf = _hgrn_prep(q_ref[i, :, sl], f_ref[i, :, sl], lb_ref[:, sl])
        qs_ref[i, :, sl] = qx
        ks_ref[i, :, sl] = kx
        bs_ref[i, :, sl] = _mm_exact(tril, logf) * LOG2E
    half = HG_BLK // 2
    lane8 = lax.broadcasted_iota(jnp.int32, (half, HG_DK), 1)
    top16 = lax.broadcasted_iota(jnp.int32, (HG_BLK, HG_DK), 0) < half
    lower = (lax.broadcasted_iota(jnp.int32, (HG_BLK, HG_BLK), 1)
             <= lax.broadcasted_iota(jnp.int32, (HG_BLK, HG_BLK), 0))

    def intra_weights(j):
        r0 = j * HG_BLK if isinstance(j, int) else pl.multiple_of(j * HG_BLK, HG_BLK)
        top, bot, blk = pl.ds(r0, half), pl.ds(r0 + half, half), pl.ds(r0, HG_BLK)
        amat = []
        for i, h, sl in heads:
            q_t, q_b, b_t, b_b = qs_ref[i, top, sl], qs_ref[i, bot, sl], bs_ref[i, top, sl], bs_ref[i, bot, sl]
            k16, b16 = ks_ref[i, blk, sl], bs_ref[i, blk, sl]
            a_top = jnp.zeros((half, HG_DK), F32)
            a_bot = jnp.zeros((half, HG_DK), F32)
            for jj in range(half):
                s = jnp.sum(q_t * jnp.exp2(b_t - b16[jj:jj + 1]) * k16[jj:jj + 1], axis=-1, keepdims=True)
                a_top = jnp.where(lane8 == jj, s, a_top)
                jb = half + jj
                s = jnp.sum(q_b * jnp.exp2(b_b - b16[jb:jb + 1]) * k16[jb:jb + 1], axis=-1, keepdims=True)
                a_bot = jnp.where(lane8 == jb, s, a_bot)
            b_mid = b16[half - 1:half]
            k_dec = jnp.where(top16, k16 * jnp.exp2(b_mid - b16), 0.0)
            cross = _mm(q_b * jnp.exp2(b_b - b_mid), k_dec, NT)
            full = jnp.concatenate([a_top[:, :HG_BLK], a_bot[:, :HG_BLK] + cross], axis=0)
            amat.append(jnp.where(lower, full, 0.0))
        return tuple(amat)

    def apply_block(j, amat):
        rows = pl.ds(pl.multiple_of(j * HG_BLK, HG_BLK), HG_BLK)
        idx = range(len(heads))
        qb = [qs_ref[i, rows, sl] for i, h, sl in heads]
        kb = [ks_ref[i, rows, sl] for i, h, sl in heads]
        bb = [bs_ref[i, rows, sl] for i, h, sl in heads]
        vb = [i_ref[i, rows, sl] for i, h, sl in heads]
        st = [st_ref[i, h] for i, h, sl in heads]
        intra = [_mm(amat[n], vb[n]) for n in idx]
        inter = [_mm(qb[n] * jnp.exp2(bb[n]), st[n], NT) for n in idx]
        for n, (i, h, sl) in enumerate(heads):
            b_end = bb[n][HG_BLK - 1:HG_BLK, :]
            st_ref[i, h] = st[n] * jnp.exp2(b_end) + _mm(vb[n], kb[n] * jnp.exp2(b_end - bb[n]), TN)
        for n, (i, h, sl) in enumerate(heads):
            ob_ref[i, rows, sl] = intra[n] + inter[n]

    def step(j, amats):
        apply_block(j - 1, amats)
        return intra_weights(j)

    apply_block(nblk - 1, lax.fori_loop(1, nblk, step, intra_weights(0)))
    nw = nw_ref[...]
    for i, h, sl in heads:
        o_ref[i, :, sl] = _hgrn_post(ob_ref[i, :, sl], g_ref[i, :, sl], nw).astype(o_ref.dtype)

    @pl.when(c == pl.num_programs(1) - 1)
    def _():
        for i, h, sl in heads:
            s_ref[i, h] = st_ref[i, h].T


def _hgrn_prompt(proj, lb, nw, bsz, t, l):
    tt = min(256, t)
    n_t = t // tt
    nb = HG_NB if bsz % HG_NB == 0 else 1
    tril = jnp.asarray(np.kron(np.eye(tt // HG_BLK), np.tril(np.ones((HG_BLK, HG_BLK)))), BF16)
    proj3 = proj.reshape(bsz, t, NCOL)
    col = lambda blk: (lambda b, c: (b, c, blk))
    const = lambda shape: pl.BlockSpec(shape, lambda b, c: (0, 0))
    layer = lambda shape: pl.BlockSpec((None,) + shape, lambda b, c: (l, 0, 0))
    tile = pltpu.VMEM((nb, tt, D_MODEL), F32)
    o, s = pl.pallas_call(
        functools.partial(_hgrn_prompt_kernel, nblk=tt // HG_BLK),
        grid=(bsz // nb, n_t),
        in_specs=[
            pl.BlockSpec((nb, tt, D_MODEL), col(C_Q)), pl.BlockSpec((nb, tt, D_MODEL), col(C_F)),
            pl.BlockSpec((nb, tt, D_MODEL), col(C_I)), pl.BlockSpec((nb, tt, D_MODEL), col(C_GH)),
            layer((1, D_MODEL)), layer((1, 128)), const((tt, tt)),
        ],
        out_specs=[
            pl.BlockSpec((nb, tt, D_MODEL), lambda b, c: (b, c, 0)),
            pl.BlockSpec((nb, HG_HEADS, HG_DK, HG_DV), lambda b, c: (b, 0, 0, 0)),
        ],
        out_shape=[
            jax.ShapeDtypeStruct((bsz, t, D_MODEL), BF16),
            jax.ShapeDtypeStruct((bsz, HG_HEADS, HG_DK, HG_DV), F32),
        ],
        scratch_shapes=[pltpu.VMEM((nb, HG_HEADS, HG_DV, HG_DK), F32), tile, tile, tile, tile],
        compiler_params=_params(("parallel", "arbitrary")),
        name="hgrn_prompt",
    )(proj3, proj3, proj3, proj3, lb, nw, tril)
    return o.reshape(bsz * t, D_MODEL), s


def _hgrn_sample_kernel(q_ref, f_ref, i_ref, g_ref, lb_ref, nw_ref, s_ref, o_ref, so_ref,
                        qs_ref, ks_ref, fs_ref, ob_ref):
    qx, kx, logf = _hgrn_prep(q_ref[...], f_ref[...], lb_ref[...])
    qs_ref[...] = qx
    ks_ref[...] = kx
    fs_ref[...] = jnp.exp(logf)
    ob_ref[...] = jnp.zeros_like(ob_ref)
    rows = lax.broadcasted_iota(jnp.int32, (SAMPLE_TB, 1), 0)

    heads = [slice(h * 128, (h + 1) * 128) for h in range(HG_HEADS)]

    unroll = 4

    def per_seqs(i, carry):
        work = [(i * unroll + j, rows == i * unroll + j, h, sl) for j in range(unroll)
                for h, sl in enumerate(heads)]
        fcol = [_col_bcast(jnp.where(sel, fs_ref[:, sl], 0.0), HG_DV) for b, sel, h, sl in work]
        kv = [_mm(jnp.where(sel, ks_ref[:, sl], 0.0), i_ref[:, sl], TN) for b, sel, h, sl in work]
        s_new = [fcol[n] * s_ref[b, h] + kv[n] for n, (b, sel, h, sl) in enumerate(work)]
        for s, (b, sel, h, sl) in zip(s_new, work):
            so_ref[b, h] = s
        outs = [_mm(jnp.where(sel, qs_ref[:, sl], 0.0), s) for s, (b, sel, h, sl) in zip(s_new, work)]
        for h, sl in enumerate(heads):
            ob_ref[:, sl] += sum(outs[j * HG_HEADS + h] for j in range(unroll))
        return carry

    lax.fori_loop(0, SAMPLE_TB // unroll, per_seqs, 0)
    nw = nw_ref[...]
    for h in range(HG_HEADS):
        sl = slice(h * 128, (h + 1) * 128)
        o_ref[:, sl] = _hgrn_post(ob_ref[:, sl], g_ref[:, sl], nw)


def _hgrn_sample(proj, lb, nw, state, l, new_state):
    bsz = state.shape[1]
    tb = SAMPLE_TB
    col = lambda blk: (lambda i: (i, blk))
    tile = pltpu.VMEM((tb, D_MODEL), F32)
    sspec = pl.BlockSpec((None, tb, HG_HEADS, HG_DK, HG_DV), lambda i: (l, i, 0, 0, 0))
    return _sample_call(
        _hgrn_sample_kernel, "hgrn_sample", bsz,
        [
            pl.BlockSpec((tb, D_MODEL), col(C_Q)), pl.BlockSpec((tb, D_MODEL), col(C_F)),
            pl.BlockSpec((tb, D_MODEL), col(C_I)), pl.BlockSpec((tb, D_MODEL), col(C_GH)),
            pl.BlockSpec((None, 1, D_MODEL), lambda i: (l, 0, 0)),
            pl.BlockSpec((None, 1, 128), lambda i: (l, 0, 0)),
            sspec,
        ],
        (proj, proj, proj, proj, lb, nw, state), sspec, state.shape, [tile, tile, tile, tile], new_state)


def _rw_lora(wdad, w2a2):
    lo = lax.broadcasted_iota(jnp.int32, wdad.shape, 1) < RW_LORA
    lora_w = _mm3(jnp.where(lo, jnp.tanh(wdad), 0.0), w2a2)
    lora_a = _mm3(jnp.where(lo, 0.0, wdad), w2a2)
    return lora_w, lora_a


def _rw_prep_all(items, ones_bd):
    rs, ks, vs, lws, a_sigs, kks = [], [], [], [], [], []
    for r, k0, v, lora_w, lora_a, w0, a0, kk_w, ka_w in items:
        lws.append(-RW_DECAY_SCALE * _sigmoid(w0 + lora_w))
        a_sig = _sigmoid(a0 + lora_a)
        a_sigs.append(a_sig)
        kks.append(k0 * kk_w)
        rs.append(r)
        vs.append(v)
        ks.append(k0 * (1.0 + (a_sig - 1.0) * ka_w))
    sq = _seg_many(ones_bd, [kk * kk for kk in kks])
    kks = [kk / jnp.maximum(jnp.sqrt(s), 1e-12) for kk, s in zip(kks, sq)]
    return rs, ks, vs, lws, [-kk for kk in kks], [kk * a for kk, a in zip(kks, a_sigs)]


def _rw_posts(os_, rs, ks, vs, gs, rk_ws, lnws, lnbs, ones_bd):
    idx = range(len(os_))
    sums = _seg_many(ones_bd, list(os_) + [rs[i] * ks[i] * rk_ws[i] for i in idx])
    ds = [os_[i] - sums[i] * (1.0 / RW_N) for i in idx]
    bonus = sums[len(os_):]
    var = [s * (1.0 / RW_N) for s in _seg_many(ones_bd, [d * d for d in ds])]
    return [(ds[i] * lax.rsqrt(var[i] + RW_LN_EPS) * lnws[i] + lnbs[i] + bonus[i] * vs[i]) * _silu(gs[i])
            for i in idx]


def _rw_chunks(rs, ks, vs, lws, avs, bvs, sbds, tril):
    n = RW_CHUNK
    pairs = range(len(rs))
    lane = lax.broadcasted_iota(jnp.int32, (n, 128), 1)
    row = lax.broadcasted_iota(jnp.int32, (n, 128), 0)
    m0 = lane < RW_N
    col = jnp.where(m0, lane, lane - RW_N)
    strict = col < row
    incl = col <= row
    eye = (col == row).astype(F32)

    def bd(x):
        xb = x.astype(BF16)
        zero = jnp.zeros_like(xb)
        return jnp.concatenate([jnp.where(m0, xb, zero), jnp.where(m0, zero, xb)], axis=0)

    cls = [_mm_exact(tril, lws[q]) for q in pairs]
    a_h = [avs[q] * jnp.exp(cls[q] - lws[q]) for q in pairs]
    r_h = [rs[q] * jnp.exp(cls[q]) for q in pairs]
    b_c = [bvs[q] * jnp.exp(-cls[q]) for q in pairs]
    k_c = [ks[q] * jnp.exp(-cls[q]) for q in pairs]
    lhs = [jnp.concatenate([a_h[q], r_h[q]], axis=0) for q in pairs]
    from_state = [_mm(lhs[q], sbds[q], NT) for q in pairs]
    m_bk = [_mm(lhs[q], jnp.concatenate([bd(b_c[q]), bd(k_c[q])], axis=0), NT) for q in pairs]
    m_ab = [jnp.where(strict, m[:n, :128], 0.0) for m in m_bk]
    m_ak = [jnp.where(strict, m[:n, 128:], 0.0) for m in m_bk]
    m_rb = [jnp.where(incl, m[n:, :128], 0.0) for m in m_bk]
    m_rk = [jnp.where(incl, m[n:, 128:], 0.0) for m in m_bk]
    ps = [_mm(m, bd(m)) for m in m_ab]
    xs = [eye + m for m in m_ab]
    for level in range(1, 5):
        both = [_mm(jnp.concatenate([ps[q], xs[q]], axis=0), bd(ps[q])) for q in pairs]
        ps = [m[:n] for m in both]
        xs = [xs[q] + both[q][n:] for q in pairs]
    xs = [xs[q] + _mm(xs[q], bd(ps[q])) for q in pairs]
    from_v = [_mm(jnp.concatenate([m_ak[q], m_rk[q]], axis=0), bd(vs[q])) for q in pairs]
    w_all = [from_state[q][:n] + from_v[q][:n] for q in pairs]
    u_all = [_mm(xs[q], bd(w_all[q])) for q in pairs]
    o_all = [from_state[q][n:] + from_v[q][n:] + _mm(m_rb[q], bd(u_all[q])) for q in pairs]

    vi = lax.broadcasted_iota(jnp.int32, (128, 128), 0)
    ki = lax.broadcasted_iota(jnp.int32, (128, 128), 1)
    same_head = (vi < RW_N) == (ki < RW_N)
    new = []
    for q in pairs:
        cl_end = cls[q][n - 1:n, :]
        e_end = jnp.exp(cl_end - cls[q])
        upd = _mm(jnp.concatenate([u_all[q], vs[q]], axis=0),
                  jnp.concatenate([bvs[q] * e_end, ks[q] * e_end], axis=0), TN)
        new.append(sbds[q] * jnp.exp(cl_end) + jnp.where(same_head, upd, 0.0))
    return o_all, new


def _rwkv_prompt_kernel(r_ref, k_ref, v_ref, g_ref, wdad_ref, mu_ref, w0_ref, a0_ref, kkw_ref, kaw_ref, rkw_ref,
                        lnw_ref, lnb_ref, w2a2_ref, tril_ref, ones_ref, o_ref, s_ref,
                        sbd_ref, pr_ref, pk_ref, pv_ref, pwd_ref, lw_ref, la_ref):
    c = pl.program_id(1)
    n = RW_CHUNK
    nb = r_ref.shape[0]

    @pl.when(c == 0)
    def _():
        sbd_ref[...] = jnp.zeros_like(sbd_ref)
        pr_ref[...] = jnp.zeros_like(pr_ref)
        pk_ref[...] = jnp.zeros_like(pk_ref)
        pv_ref[...] = jnp.zeros_like(pv_ref)
        pwd_ref[...] = jnp.zeros_like(pwd_ref)

    row = lax.broadcasted_iota(jnp.int32, (n, 128), 0)

    def shift(z_ref, prev_ref, i, sl, mu_lo):
        z = z_ref[i, :, sl]
        zp = jnp.where(row == 0, prev_ref[i, :, sl], pltpu.roll(z, 1, axis=0))
        prev_ref[i, :, sl] = z[n - 1:n, :]
        return z + (zp - z) * mu_ref[:, mu_lo:mu_lo + 128]

    for i in range(nb):
        lora_w, lora_a = _rw_lora(shift(wdad_ref, pwd_ref, i, slice(0, 128), 3 * D_MODEL), w2a2_ref[...])
        lw_ref[i] = lora_w
        la_ref[i] = lora_a
    ones_bd = ones_ref[...]
    lanes = [slice(q * 128, (q + 1) * 128) for q in range(RW_HEADS // 2)]
    pairs = [(i, q, sl) for i in range(nb) for q, sl in enumerate(lanes)]
    rs, ks, vs, lws, avs, bvs = _rw_prep_all(
        [(shift(r_ref, pr_ref, i, sl, q * 128), shift(k_ref, pk_ref, i, sl, D_MODEL + q * 128),
          shift(v_ref, pv_ref, i, sl, 2 * D_MODEL + q * 128), lw_ref[i, :, sl], la_ref[i, :, sl],
          w0_ref[:, sl], a0_ref[:, sl], kkw_ref[:, sl], kaw_ref[:, sl]) for i, q, sl in pairs], ones_bd)
    o_all, sbd_new = _rw_chunks(rs, ks, vs, lws, avs, bvs, [sbd_ref[i, q] for i, q, sl in pairs], tril_ref[...])
    for (i, q, sl), new in zip(pairs, sbd_new):
        sbd_ref[i, q] = new
    outs = _rw_posts(o_all, rs, ks, vs, [g_ref[i, :, sl] for i, q, sl in pairs],
                     [rkw_ref[:, sl] for i, q, sl in pairs], [lnw_ref[:, sl] for i, q, sl in pairs],
                     [lnb_ref[:, sl] for i, q, sl in pairs], ones_bd)
    for (i, q, sl), out in zip(pairs, outs):
        o_ref[i, :, sl] = out.astype(o_ref.dtype)

    @pl.when(c == pl.num_programs(1) - 1)
    def _():
        s_ref[...] = sbd_ref[...]


def _rw_consts():
    tril = jnp.asarray(np.tril(np.ones((RW_CHUNK, RW_CHUNK))), BF16)
    ones_bd = jnp.asarray(np.kron(np.eye(2), np.ones((RW_N, RW_N))), BF16)
    return tril, ones_bd


def _rwkv_prompt(proj, p, bsz, t, l):
    n = RW_CHUNK
    n_t = t // n
    nb = RW_NB if bsz % RW_NB == 0 else 1
    tril, ones_bd = _rw_consts()
    proj3 = proj.reshape(bsz, t, NCOL)
    col = lambda blk: (lambda b, c: (b, c, blk))
    const = lambda shape: pl.BlockSpec(shape, lambda b, c: (0, 0))
    layer = lambda shape: pl.BlockSpec((None,) + shape, lambda b, c: (l, 0, 0))
    vec = layer((1, D_MODEL))
    prev = pltpu.VMEM((nb, 1, D_MODEL), F32)
    tile = pltpu.VMEM((nb, n, D_MODEL), F32)
    o, sbd = pl.pallas_call(
        _rwkv_prompt_kernel,
        grid=(bsz // nb, n_t),
        in_specs=[
            pl.BlockSpec((nb, n, D_MODEL), col(C_R)), pl.BlockSpec((nb, n, D_MODEL), col(C_K)),
            pl.BlockSpec((nb, n, D_MODEL), col(C_V)), pl.BlockSpec((nb, n, D_MODEL), col(C_GR)),
            pl.BlockSpec((nb, n, 128), lambda b, c: (b, c, C_WDAD)),
            layer((1, RW_SHIFT)),
            vec, vec, vec, vec, vec, vec, vec,
            layer((128, D_MODEL)), const((n, n)), const((128, 128)),
        ],
        out_specs=[
            pl.BlockSpec((nb, n, D_MODEL), lambda b, c: (b, c, 0)),
            pl.BlockSpec((nb, RW_HEADS // 2, 128, 128), lambda b, c: (b, 0, 0, 0)),
        ],
        out_shape=[
            jax.ShapeDtypeStruct((bsz, t, D_MODEL), BF16),
            jax.ShapeDtypeStruct((bsz, RW_HEADS // 2, 128, 128), F32),
        ],
        scratch_shapes=[pltpu.VMEM((nb, RW_HEADS // 2, 128, 128), F32), prev, prev, prev,
                        pltpu.VMEM((nb, 1, 128), F32), tile, tile],
        compiler_params=_params(("parallel", "arbitrary")),
        name="rwkv_prompt",
    )(proj3, proj3, proj3, proj3, proj3, p["mu"], p["w0"], p["a0"], p["kk"], p["ka"], p["rk"], p["lnw"], p["lnb"],
      p["w2a2"], tril, ones_bd)
    s6 = sbd.reshape(bsz, RW_HEADS // 2, 2, RW_N, 2, RW_N)
    state = jnp.stack([s6[:, :, 0, :, 0, :], s6[:, :, 1, :, 1, :]], axis=2)
    return o.reshape(bsz * t, D_MODEL), state.reshape(bsz, RW_HEADS, RW_N, RW_N)


def _rwkv_sample_prep_kernel(r_ref, k_ref, v_ref, wdad_ref, sh_ref, mu_ref, w0_ref, a0_ref, kkw_ref, kaw_ref,
                             w2a2_ref, ones_ref, r_o, k_o, v_o, w_o, a_o, b_o):
    def shift(z, lo, width):
        return z + (sh_ref[:, lo:lo + width] - z) * mu_ref[:, lo:lo + width]

    lora_w, lora_a = _rw_lora(shift(wdad_ref[...], 3 * D_MODEL, 128), w2a2_ref[...])
    pairs = [slice(q * 128, (q + 1) * 128) for q in range(RW_HEADS // 2)]
    rs, ks, vs, lws, avs, bvs = _rw_prep_all(
        [(shift(r_ref[:, sl], q * 128, 128), shift(k_ref[:, sl], D_MODEL + q * 128, 128),
          shift(v_ref[:, sl], 2 * D_MODEL + q * 128, 128), lora_w[:, sl], lora_a[:, sl], w0_ref[:, sl],
          a0_ref[:, sl], kkw_ref[:, sl], kaw_ref[:, sl]) for q, sl in enumerate(pairs)], ones_ref[...])
    for q, sl in enumerate(pairs):
        r_o[:, sl] = rs[q]
        k_o[:, sl] = ks[q]
        v_o[:, sl] = vs[q]
        w_o[:, sl] = jnp.exp(lws[q])
        a_o[:, sl] = avs[q]
        b_o[:, sl] = bvs[q]


def _rwkv_sample_state_kernel(s_ref, w_ref, a_ref, b_ref, k_ref, v_ref, r_ref, o_ref, so_ref):
    row8 = lax.broadcasted_iota(jnp.int32, (8, s_ref.shape[-1]), 0)
    for hh in range(s_ref.shape[0]):
        keys = slice(hh * RW_N, (hh + 1) * RW_N)
        w, a, bk, kk, r = w_ref[keys, :], a_ref[keys, :], b_ref[keys, :], k_ref[keys, :], r_ref[keys, :]

        def eight_values(v8, carry):
            base = pl.multiple_of(v8 * 8, 8)
            vals = v_ref[pl.ds(hh * RW_N + base, 8), :]
            outs = jnp.zeros_like(vals)
            for j in range(8):
                s_old = s_ref[hh, base + j]
                sa = jnp.sum(s_old * a, axis=0, keepdims=True)
                s_new = s_old * w + sa * bk + vals[j:j + 1, :] * kk
                so_ref[hh, base + j] = s_new
                outs = jnp.where(row8 == j, jnp.sum(s_new * r, axis=0, keepdims=True), outs)
            o_ref[pl.ds(hh * RW_N + base, 8), :] = outs
            return carry

        lax.fori_loop(0, RW_N // 8, eight_values, 0)


def _rwkv_sample_post_kernel(o_ref, r_ref, k_ref, v_ref, g_ref, rkw_ref, lnw_ref, lnb_ref, ones_ref, out_ref):
    pairs = [slice(q * 128, (q + 1) * 128) for q in range(RW_HEADS // 2)]
    ref = lambda x: [x[:, sl] for sl in pairs]
    outs = _rw_posts(ref(o_ref), ref(r_ref), ref(k_ref), ref(v_ref), ref(g_ref), ref(rkw_ref), ref(lnw_ref),
                     ref(lnb_ref), ones_ref[...])
    for sl, out in zip(pairs, outs):
        out_ref[:, sl] = out


def _rwkv_sample(proj, p, shift_state, state_t, l, new_state_t):
    bsz = proj.shape[0]
    _, ones_bd = _rw_consts()
    whole = lambda shape: pl.BlockSpec(shape, lambda i: tuple(0 for _ in shape))
    layer = lambda shape: pl.BlockSpec((None,) + shape, lambda i: (l,) + tuple(0 for _ in shape))
    col = lambda blk, w=D_MODEL: pl.BlockSpec((bsz, w), lambda i: (0, blk))
    vec = layer((1, D_MODEL))
    act = jax.ShapeDtypeStruct((bsz, D_MODEL), F32)
    r, k, v, w, a, b = pl.pallas_call(
        _rwkv_sample_prep_kernel,
        grid=(1,),
        in_specs=[col(C_R), col(C_K), col(C_V), col(C_WDAD, 128), layer((bsz, RW_SHIFT)), layer((1, RW_SHIFT)),
                  vec, vec, vec, vec, layer((128, D_MODEL)), whole((128, 128))],
        out_specs=[whole((bsz, D_MODEL))] * 6,
        out_shape=[act] * 6,
        compiler_params=_params(("arbitrary",)),
        name="rwkv_sample_prep",
    )(proj, proj, proj, proj, shift_state, p["mu"], p["w0"], p["a0"], p["kk"], p["ka"], p["w2a2"], ones_bd)

    hb = 2
    rows = pl.BlockSpec((hb * RW_N, bsz), lambda i: (i, 0))
    sspec = pl.BlockSpec((None, hb, RW_N, RW_N, bsz), lambda i: (l, i, 0, 0, 0))
    args = (state_t, w.T, a.T, b.T, k.T, v.T, r.T)
    in_specs = [sspec] + [rows] * 6
    aliases = {}
    body = _rwkv_sample_state_kernel
    if new_state_t is not None:
        in_specs = in_specs + [pl.BlockSpec(memory_space=pl.ANY)]
        args = args + (new_state_t,)
        aliases = {len(args) - 1: 1}
        body = lambda *refs: _rwkv_sample_state_kernel(*refs[:7], *refs[8:])
    o_t, new_state_t = pl.pallas_call(
        body,
        grid=(RW_HEADS // hb,),
        in_specs=in_specs,
        out_specs=[rows, sspec],
        out_shape=[jax.ShapeDtypeStruct((D_MODEL, bsz), F32), jax.ShapeDtypeStruct(state_t.shape, F32)],
        input_output_aliases=aliases,
        compiler_params=_params(("parallel",)),
        name="rwkv_sample_state",
    )(*args)

    o = pl.pallas_call(
        _rwkv_sample_post_kernel,
        grid=(1,),
        in_specs=[whole((bsz, D_MODEL))] * 4 + [col(C_GR), vec, vec, vec, whole((128, 128))],
        out_specs=whole((bsz, D_MODEL)),
        out_shape=act,
        compiler_params=_params(("arbitrary",)),
        name="rwkv_sample_post",
    )(o_t.T, r, k, v, proj, p["rk"], p["lnw"], p["lnb"], ones_bd)
    return o, new_state_t


def _mb_post(y, xs, z, d_rep, nw):
    y = (y + d_rep * xs) * _silu(z)
    ms = jnp.mean(y * y, axis=-1, keepdims=True)
    return y * lax.rsqrt(ms + RMS_EPS) * nw


def _ssd_prompt_kernel(x_ref, bc_ref, z_ref, dt_ref, cw_ref, cb_ref, dtb_ref, alog_ref, d_ref, nw_ref, tril_ref,
                       exp_ref, o_ref, s_ref, h_ref, px_ref, pbc_ref):
    c = pl.program_id(1)
    n = MB_CHUNK
    nb = x_ref.shape[0]

    @pl.when(c == 0)
    def _():
        h_ref[...] = jnp.zeros_like(h_ref)
        px_ref[...] = jnp.zeros_like(px_ref)
        pbc_ref[...] = jnp.zeros_like(pbc_ref)

    def conv(u_ref, prev_ref, i, sl, wsl):
        u = u_ref[i, :, sl]
        prev = prev_ref[i, :, sl]
        w = cw_ref[:, wsl]
        row = lax.broadcasted_iota(jnp.int32, (8, u.shape[1]), 0)
        acc = u * w[MB_CONV - 1:MB_CONV, :] + cb_ref[:, wsl]
        for s in range(1, MB_CONV):
            us = pltpu.roll(u, s, axis=0)
            top = jnp.where(row < s, pltpu.roll(prev, s, axis=0), us[0:8])
            us = jnp.concatenate([top, us[8:]], axis=0)
            acc = acc + us * w[MB_CONV - 1 - s:MB_CONV - s, :]
        prev_ref[i, :, sl] = u[n - 8:n, :]
        return _silu(acc)

    tril = tril_ref[...]
    ti = lax.broadcasted_iota(jnp.int32, (n, n), 0)
    si = lax.broadcasted_iota(jnp.int32, (n, n), 1)
    incl = si <= ti
    triu = (ti <= si).astype(BF16)
    lane = lax.broadcasted_iota(jnp.int32, (n, 4 * MB_P), 1)
    nbc = MB_GROUPS * MB_N
    seqs = range(nb)
    groups = [(i, g, slice(g * 256, (g + 1) * 256)) for i in seqs for g in range(MB_GROUPS)]
    idx = range(len(groups))

    spread = exp_ref[...]
    dt_c = [_softplus(dt_ref[i] + dtb_ref[...]) for i in seqs]
    a_c = [dt_c[i] * (-jnp.exp(alog_ref[...])) for i in seqs]
    cs_c = [_mm_exact(tril, a_c[i]) for i in seqs]
    cs_rows = [_mm_exact(triu, a_c[i], TN, x_is_lhs=True) for i in seqs]
    dt_all = [_mm_exact(spread, dt_c[i], x_is_lhs=True) for i in seqs]
    cs_all = [_mm_exact(spread, cs_c[i], x_is_lhs=True) for i in seqs]

    xs = [conv(x_ref, px_ref, i, gx, gx) for i, g, gx in groups]
    bm = [conv(bc_ref, pbc_ref, i, slice(g * MB_N, (g + 1) * MB_N),
               slice(D_MODEL + g * MB_N, D_MODEL + (g + 1) * MB_N)) for i, g, gx in groups]
    cm = [conv(bc_ref, pbc_ref, i, slice(nbc + g * MB_N, nbc + (g + 1) * MB_N),
               slice(D_MODEL + nbc + g * MB_N, D_MODEL + nbc + (g + 1) * MB_N)) for i, g, gx in groups]
    xdt = [xs[k] * dt_all[i][:, gx] for k, (i, g, gx) in enumerate(groups)]
    cs = [cs_all[i][:, gx] for i, g, gx in groups]
    gmat = [_mm(cm[k], bm[k], NT) for k in idx]
    h_all = [h_ref[i, g] for i, g, gx in groups]
    y = [jnp.exp(cs[k]) * _mm(cm[k], h_all[k], NT) for k in idx]
    for k, (i, g, gx) in enumerate(groups):
        for e in range(4):
            hd = slice(e * MB_P, (e + 1) * MB_P)
            hrow = slice(4 * g + e, 4 * g + e + 1)
            lmat = jnp.where(incl, jnp.exp(cs[k][:, hd] - cs_rows[i][hrow, :]), 0.0)
            me = (lane >= e * MB_P) & (lane < (e + 1) * MB_P)
            y[k] = y[k] + _mm(gmat[k] * lmat, jnp.where(me, xdt[k], 0.0))
    for k, (i, g, gx) in enumerate(groups):
        cs_end = cs[k][n - 1:n, :]
        upd = _mm(xdt[k] * jnp.exp(cs_end - cs[k]), bm[k], TN)
        decay = [jnp.exp(cs_rows[i][4 * g + e:4 * g + e + 1, n - 1:n]) for e in range(4)]
        h_ref[i, g] = jnp.concatenate([h_all[k][e * MB_P:(e + 1) * MB_P] * decay[e] for e in range(4)],
                                      axis=0) + upd
    for k, (i, g, gx) in enumerate(groups):
        o_ref[i, :, gx] = _mb_post(y[k], xs[k], z_ref[i, :, gx], d_ref[:, gx], nw_ref[:, gx]).astype(o_ref.dtype)

    @pl.when(c == pl.num_programs(1) - 1)
    def _():
        for i, g, gx in groups:
            for e in range(4):
                s_ref[i, 4 * g + e] = h_ref[i, g, e * MB_P:(e + 1) * MB_P, :]


def _head_spread():
    m = np.zeros((128, D_MODEL), np.float32)
    for h in range(MB_HEADS):
        m[h, h * MB_P:(h + 1) * MB_P] = 1.0
    return jnp.asarray(m, BF16)


def _ssd_prompt(proj, p, bsz, t, l):
    n = MB_CHUNK
    n_t = t // n
    nb = MB_NB if bsz % MB_NB == 0 else 1
    tril = jnp.asarray(np.tril(np.ones((n, n))), BF16)
    proj3 = proj.reshape(bsz, t, NCOL)
    col = lambda blk: (lambda b, c: (b, c, blk))
    const = lambda shape: pl.BlockSpec(shape, lambda b, c: (0, 0))
    layer = lambda shape: pl.BlockSpec((None,) + shape, lambda b, c: (l, 0, 0))
    vec = layer((1, D_MODEL))
    o, s = pl.pallas_call(
        _ssd_prompt_kernel,
        grid=(bsz // nb, n_t),
        in_specs=[
            pl.BlockSpec((nb, n, D_MODEL), col(C_X)), pl.BlockSpec((nb, n, D_MODEL), col(C_BC)),
            pl.BlockSpec((nb, n, D_MODEL), col(C_Z)), pl.BlockSpec((nb, n, 128), col(C_DT)),
            layer((MB_CONV, MB_XBC)), layer((1, MB_XBC)),
            layer((1, 128)), layer((1, 128)), vec, vec,
            const((n, n)), const((128, D_MODEL)),
        ],
        out_specs=[
            pl.BlockSpec((nb, n, D_MODEL), lambda b, c: (b, c, 0)),
            pl.BlockSpec((nb, MB_HEADS, MB_P, MB_N), lambda b, c: (b, 0, 0, 0)),
        ],
        out_shape=[
            jax.ShapeDtypeStruct((bsz, t, D_MODEL), BF16),
            jax.ShapeDtypeStruct((bsz, MB_HEADS, MB_P, MB_N), F32),
        ],
        scratch_shapes=[pltpu.VMEM((nb, MB_GROUPS, 4 * MB_P, MB_N), F32), pltpu.VMEM((nb, 8, D_MODEL), F32),
                        pltpu.VMEM((nb, 8, D_MODEL), F32)],
        compiler_params=_params(("parallel", "arbitrary")),
        name="ssd_prompt",
    )(proj3, proj3, proj3, proj3, p["cw"], p["cb"], p["dtb"], p["alog"], p["d"], p["nw"], tril, _head_spread())
    return o.reshape(bsz * t, D_MODEL), s


def _ssd_sample_kernel(xbc_ref, z_ref, dt_ref, cs_ref, cw_ref, cb_ref, dtb_ref, alog_ref, d_ref, nw_ref, exp_ref,
                       s_ref, o_ref, so_ref, xs_ref, xd_ref, bs_ref, cms_ref, dec_ref, ob_ref):
    tb = SAMPLE_TB
    w = cw_ref[...]
    acc = xbc_ref[...] * w[MB_CONV - 1:MB_CONV, :] + cb_ref[...]
    for i in range(MB_CONV - 1):
        acc = acc + cs_ref[i] * w[i:i + 1, :]
    xbc = _silu(acc)
    xs = xbc[:, :D_MODEL]
    dt = _softplus(dt_ref[...] + dtb_ref[...])
    xs_ref[...] = xs
    xd_ref[...] = xs * _mm_exact(exp_ref[...], dt, x_is_lhs=True)
    bs_ref[...] = xbc[:, D_MODEL:D_MODEL + MB_GROUPS * MB_N]
    cms_ref[...] = xbc[:, D_MODEL + MB_GROUPS * MB_N:]
    dec = jnp.exp(dt * (-jnp.exp(alog_ref[...])))
    for i in range(tb):
        dec_ref[i] = dec[i:i + 1, :]
    ob_ref[...] = jnp.zeros_like(ob_ref)
    rows = lax.broadcasted_iota(jnp.int32, (tb, 1), 0)

    hs = range(MB_HEADS)
    lanes = [slice(h * MB_P, (h + 1) * MB_P) for h in hs]
    grp = [slice((h // 4) * MB_N, (h // 4 + 1) * MB_N) for h in hs]
    unroll = 4

    def per_seqs(i, carry):
        work = [(i * unroll + j, rows == i * unroll + j, h) for j in range(unroll) for h in hs]

        def dec_of(b, h):
            return dec_ref[b][:, h:h + 1]

        upd = [_mm(jnp.where(sel, xd_ref[:, lanes[h]], 0.0), bs_ref[:, grp[h]], TN) for b, sel, h in work]
        h_new = [dec_of(b, h) * s_ref[b, h] + upd[n] for n, (b, sel, h) in enumerate(work)]
        for s, (b, sel, h) in zip(h_new, work):
            so_ref[b, h] = s
        outs = [_mm(jnp.where(sel, cms_ref[:, grp[h]], 0.0), s, NT) for s, (b, sel, h) in zip(h_new, work)]
        for h in hs:
            ob_ref[:, lanes[h]] += sum(outs[j * MB_HEADS + h] for j in range(unroll))
        return carry

    lax.fori_loop(0, tb // unroll, per_seqs, 0)
    for g in range(MB_GROUPS):
        sl = slice(g * 256, (g + 1) * 256)
        o_ref[:, sl] = _mb_post(ob_ref[:, sl], xs_ref[:, sl], z_ref[:, sl], d_ref[:, sl], nw_ref[:, sl])


def _ssd_sample(proj, p, conv_state, state, l, new_state):
    bsz = state.shape[1]
    tb = SAMPLE_TB
    layer = lambda shape: pl.BlockSpec((None,) + shape, lambda i: (l, 0, 0))
    vec = layer((1, D_MODEL))
    tile = pltpu.VMEM((tb, D_MODEL), F32)
    half = pltpu.VMEM((tb, MB_GROUPS * MB_N), F32)
    sspec = pl.BlockSpec((None, tb, MB_HEADS, MB_P, MB_N), lambda i: (l, i, 0, 0, 0))
    return _sample_call(
        _ssd_sample_kernel, "ssd_sample", bsz,
        [
            pl.BlockSpec((tb, MB_XBC), lambda i: (i, C_X // 2)),
            pl.BlockSpec((tb, D_MODEL), lambda i: (i, C_Z)),
            pl.BlockSpec((tb, 128), lambda i: (i, C_DT)),
            pl.BlockSpec((None, MB_CONV - 1, tb, MB_XBC), lambda i: (l, 0, i, 0)),
            layer((MB_CONV, MB_XBC)), layer((1, MB_XBC)),
            layer((1, 128)), layer((1, 128)), vec, vec,
            pl.BlockSpec((128, D_MODEL), lambda i: (0, 0)),
            sspec,
        ],
        (proj, proj, proj, conv_state, p["cw"], p["cb"], p["dtb"], p["alog"], p["d"], p["nw"], _head_spread(),
         state),
        sspec, state.shape, [tile, tile, half, half, pltpu.VMEM((tb, 1, 128), F32), tile], new_state)


def _prep_w_in(w_in):
    o = np.cumsum([0, 1024, 1024, 1024, 1024, RW_SHIFT, 1024, 1024, MB_XBC, MB_HEADS, 3 * D_MODEL])
    zrw, grw, zmb, xbc, dtc, gate = o[4], o[5], o[6], o[7], o[8], o[9]
    parts = [(0, zrw), (zrw, zrw + 3 * D_MODEL), (grw, zmb), (xbc, dtc), (zmb, xbc), (gate, o[10]),
             (zrw + 3 * D_MODEL, grw), (dtc, gate)]
    out = jnp.zeros(w_in.shape[:-1] + (NCOL,), BF16)
    at = 0
    for lo, hi in parts:
        out = lax.dynamic_update_slice_in_dim(out, w_in[..., lo:hi].astype(BF16), at, axis=-1)
        at += hi - lo
    return out


def _rep(v):
    return jnp.repeat(v, MB_P, axis=-1)[:, None, :]


def _pad_heads(v):
    return jnp.pad(v, ((0, 0), (0, 128 - MB_HEADS)))[:, None, :]


def kernel(x_prompt, x_sample, state_hgrn, state_rwkv, state_rwkv_shift, state_ssm, state_conv, norm_w, w_in, hg_lb, hg_norm_w, rw_mu, rw_w0, rw_w2, rw_a0, rw_a2, rw_k_k, rw_k_a, rw_r_k, rw_ln_w, rw_ln_b, mb_conv_w, mb_conv_b, mb_dt_bias, mb_A_log, mb_D, mb_norm_w, w_o_hg, w_o_rw, w_o_mb, w_out, final_norm_w):
    bp, t, _ = x_prompt.shape
    bs = x_sample.shape[0]
    sm = jax.nn.softmax(hg_lb.astype(F32), axis=0)
    row = lambda a: a[:, None, :]
    lbs = row(jnp.cumsum(sm, axis=0) - sm[0:1])
    w_in_r = _prep_w_in(w_in)
    rw_p = dict(mu=row(rw_mu), w0=row(rw_w0), a0=row(rw_a0), kk=row(rw_k_k), ka=row(rw_k_a),
                rk=rw_r_k.reshape(DEPTH, 1, D_MODEL), lnw=row(rw_ln_w), lnb=row(rw_ln_b),
                w2a2=jnp.concatenate([rw_w2, rw_a2], axis=1))
    mb_p = dict(cw=mb_conv_w, cb=row(mb_conv_b), dtb=_pad_heads(mb_dt_bias), alog=_pad_heads(mb_A_log), d=_rep(mb_D),
                nw=row(mb_norm_w))
    nw_in, nw_hg = row(norm_w), row(hg_norm_w)
    whg, wrw, wmb, wout = (w.astype(BF16) for w in (w_o_hg, w_o_rw, w_o_mb, w_out))
    fw = final_norm_w[None, :]
    conv_t = jnp.swapaxes(state_conv, 1, 2)
    rwkv_t = jnp.transpose(state_rwkv, (0, 2, 3, 4, 1))
    lora = slice(C_WDAD * 128, (C_WDAD + 1) * 128)

    hp = x_prompt.reshape(bp * t, D_MODEL)
    hs = x_sample.reshape(bs, D_MODEL)
    p_states, s_small = [], []
    s_hg, s_rw, s_ssm = (jnp.zeros(s.shape, F32) for s in (state_hgrn, rwkv_t, state_ssm))
    yp = ys = None
    for l in range(DEPTH):
        final = l == DEPTH - 1

        proj = _inproj(hp, nw_in, w_in_r, l)
        o_hg, p_hg = _hgrn_prompt(proj, lbs, nw_hg, bp, t, l)
        o_rw, p_rw = _rwkv_prompt(proj, rw_p, bp, t, l)
        o_mb, p_ssm = _ssd_prompt(proj, mb_p, bp, t, l)
        proj3 = proj.reshape(bp, t, NCOL)
        p_shift = jnp.concatenate([proj3[:, -1, C_R * 1024:C_GR * 1024], proj3[:, -1, lora]], axis=-1)
        p_conv = proj3[:, t - (MB_CONV - 1):, C_X * 1024:C_Z * 1024]
        res = _merge(o_hg, o_rw, o_mb, proj, hp, whg, wrw, wmb, wout, fw, final, l)
        hp = res[0]
        if final:
            yp = res[1]
        p_states.append((p_hg, p_rw, p_shift, p_ssm, p_conv))

        proj = _inproj(hs, nw_in, w_in_r, l)
        o_hg, s_hg = _hgrn_sample(proj, lbs, nw_hg, state_hgrn, l, s_hg)
        o_rw, s_rw = _rwkv_sample(proj, rw_p, state_rwkv_shift, rwkv_t, l, s_rw)
        o_mb, s_ssm = _ssd_sample(proj, mb_p, conv_t, state_ssm, l, s_ssm)
        s_shift = jnp.concatenate([proj[:, C_R * 1024:C_GR * 1024], proj[:, lora]], axis=-1)
        s_conv = jnp.concatenate([state_conv[l][:, 1:], proj[:, None, C_X * 1024:C_Z * 1024]], axis=1)
        res = _merge(o_hg, o_rw, o_mb, proj, hs, whg, wrw, wmb, wout, fw, final, l)
        hs = res[0]
        if final:
            ys = res[1]
        s_small.append((s_shift, s_conv))

    stack = lambda states, i: jnp.stack([s[i] for s in states])
    return (yp.reshape(bp, t, D_MODEL), ys.reshape(bs, 1, D_MODEL),
            *[stack(p_states, i) for i in range(5)],
            s_hg, jnp.transpose(s_rw, (0, 4, 1, 2, 3)), stack(s_small, 0), s_ssm, stack(s_small, 1))
```
